```python
import jax
import jax.numpy as jnp
from jax import lax
import numpy as np


D_MODEL = 1024
BATCH = 8
SEQ = 4096
DEPTH = 4

GRID_W = 64
CTX_LEN = 256
NORM_EPS = 1e-6
N_MOD = 6

RW_HEADS = 6
RW_HEAD_DIM = 64
RW_WIDTH = RW_HEADS * RW_HEAD_DIM
RW_DECAY_RANK = 64
RW_A_RANK = 64
RW_GATE_RANK = 128
RW_GN_EPS = 64e-5

GLA_HEADS = 4
GLA_DK = 48
GLA_DV = 96
GLA_KW = GLA_HEADS * GLA_DK
GLA_VW = GLA_HEADS * GLA_DV
GLA_RANK = 16
GLA_TAU = 16.0
GLA_CHUNK = 64

FN_GROUPS = 4
FN_GROUP_DIM = 64
FN_WIDTH = FN_GROUPS * FN_GROUP_DIM

N_BRANCH = 3
N_EXPERTS = 16
EXPERT_HIDDEN = 2048
CAPACITY_FACTOR = 2

CONV_K = 3
CONV_CH = 3 * RW_WIDTH + 2 * GLA_KW + GLA_VW
IN_SPLITS = (RW_WIDTH, RW_WIDTH, RW_WIDTH,
             GLA_KW, GLA_KW, GLA_VW,
             2 * RW_DECAY_RANK, 2 * RW_A_RANK,
             2 * GLA_RANK,
             RW_GATE_RANK,
             GLA_VW,
             FN_WIDTH,
             N_BRANCH * D_MODEL)
N_SCAN_PARTS = 9
IN_COLS = sum(IN_SPLITS)

kernel_name = 'hybrid_rwkv7_gla_fnet_ecmoe_dit'


def _rmsnorm(x, w):
    x32 = x.astype(jnp.float32)
    y = x32 * lax.rsqrt(jnp.mean(x32 * x32, axis=-1, keepdims=True) + NORM_EPS)
    return (y * w.astype(jnp.float32)).astype(x.dtype)


def _heads(x, n):
    return x.reshape(*x.shape[:-1], x.shape[-1] // n, n)


def _to_col_major(x, rows):
    b, t = x.shape[:2]
    rest = x.shape[2:]
    return jnp.swapaxes(x.reshape(b, rows, t // rows, *rest), 1, 2).reshape(b, t, *rest)


def _from_col_major(x, rows):
    b, t = x.shape[:2]
    rest = x.shape[2:]
    return jnp.swapaxes(x.reshape(b, t // rows, rows, *rest), 1, 2).reshape(b, t, *rest)


def _short_conv(x, w, rows):
    b, t, ch = x.shape
    grid = x.reshape(b, rows, t // rows, ch)
    y = lax.conv_general_dilated(grid, w[:, :, None, :].astype(x.dtype), window_strides=(1, 1), padding='SAME',
                                 dimension_numbers=('NHWC', 'HWIO', 'NHWC'), feature_group_count=ch)
    return y.reshape(b, t, ch)


def _rwkv7_scan(r, decay, k, v, kk, bb, s0):
    def step(s, inp):
        r_t, w_t, k_t, v_t, kk_t, b_t = inp
        s = (s * w_t[:, :, None, :]
             - jnp.einsum('bhvk,bhk->bhv', s, kk_t)[..., None] * b_t[:, :, None, :]
             + v_t[..., None] * k_t[:, :, None, :])
        return s, jnp.einsum('bhvk,bhk->bhv', s, r_t)
    xs = tuple(jnp.moveaxis(t, 1, 0) for t in (r, decay, k, v, kk, bb))
    s_fin, out = lax.scan(step, s0, xs)
    return jnp.moveaxis(out, 0, 1), s_fin


def _gla_chunked(q, k, v, log_a, s0):
    b, t = q.shape[:2]
    n = t // GLA_CHUNK
    ch = lambda a: a.reshape(b, n, GLA_CHUNK, *a.shape[2:])
    qc, kc, vc, gc = ch(q), ch(k), ch(v), ch(log_a)
    g_cum = jnp.cumsum(gc, axis=2)
    g_last = g_cum[:, :, -1:]
    q_in = qc * jnp.exp(g_cum)
    k_in = kc * jnp.exp(-g_cum)
    k_tail = kc * jnp.exp(g_last - g_cum)
    mask = jnp.tril(jnp.ones((GLA_CHUNK, GLA_CHUNK), dtype=bool))
    att = jnp.where(mask, jnp.einsum('bnihk,bnjhk->bnhij', q_in, k_in), 0.0)
    o_intra = jnp.einsum('bnhij,bnjhv->bnihv', att, vc)
    kv = jnp.einsum('bnjhk,bnjhv->nbhkv', k_tail, vc)
    dec = jnp.swapaxes(jnp.exp(g_last[:, :, 0]), 0, 1)

    def step(s, inp):
        d_c, kv_c = inp
        return s * d_c[..., None] + kv_c, s
    s_fin, s_prev = lax.scan(step, s0, (dec, kv))
    o_inter = jnp.einsum('bnihk,nbhkv->bnihv', q_in, s_prev)
    return (o_intra + o_inter).reshape(b, t, *v.shape[2:]), s_fin


def _prefix_scan(scan_fn, ins_ctx, ins_lat, s0, reverse):
    flip = (lambda a: jnp.flip(a, axis=1)) if reverse else (lambda a: a)
    o_ctx, s_ctx = scan_fn(*[flip(a) for a in ins_ctx], s0)
    o_lat, _ = scan_fn(*[flip(a) for a in ins_lat], s_ctx)
    return flip(o_ctx), flip(o_lat)


def _rwkv_dir(r, k, v, wd, ad, w0, w_up, a0, a_up, k_k, k_a):
    w_log = -jax.nn.softplus(-(w0 + jnp.tanh(wd) @ w_up)) - 0.5
    decay = jnp.exp(-jnp.exp(w_log))
    a = jax.nn.sigmoid(a0 + ad @ a_up)
    kk = _heads(k * k_k, RW_HEAD_DIM)
    kk = kk / jnp.maximum(jnp.sqrt(jnp.sum(kk * kk, axis=-1, keepdims=True)), 1e-12)
    k_mod = k * (1.0 + (a - 1.0) * k_a)
    hd = lambda a_: _heads(a_, RW_HEAD_DIM)
    return (hd(r), hd(decay), hd(k_mod), hd(v), kk, kk * hd(a))


def _scan_branches(pc, pl, lp):
    b = pl[0].shape[0]
    s0_rw = jnp.zeros((b, RW_HEADS, RW_HEAD_DIM, RW_HEAD_DIM), jnp.float32)
    s0_gla = jnp.zeros((b, GLA_HEADS, GLA_DK, GLA_DV), jnp.float32)

    def rw_in(p, d):
        wd = p[6][..., d * RW_DECAY_RANK:(d + 1) * RW_DECAY_RANK]
        ad = p[7][..., d * RW_A_RANK:(d + 1) * RW_A_RANK]
        return _rwkv_dir(p[0], p[1], p[2], wd, ad, lp['rw_w0'][d], lp['rw_w_up'][d],
                         lp['rw_a0'][d], lp['rw_a_up'][d], lp['rw_k_k'], lp['rw_k_a'])

    def gla_in(p, d):
        q = _heads(jax.nn.silu(p[3]) * GLA_DK ** -0.5, GLA_DK)
        k = _heads(jax.nn.silu(p[4]), GLA_DK)
        v = _heads(jax.nn.silu(p[5]), GLA_DV)
        ad = p[8][..., d * GLA_RANK:(d + 1) * GLA_RANK]
        log_a = jax.nn.log_sigmoid(ad @ lp['gla_a_up'][d] + lp['gla_a_b'][d]) / GLA_TAU
        return (q, k, v, _heads(log_a, GLA_DK))

    def bonus(ins):
        return jnp.sum(ins[0] * ins[2] * lp['rw_r_k'], axis=-1, keepdims=True) * ins[3]

    rw_c = rb_c = rw_l = rb_l = gla_c = gla_l = 0.0
    for d in range(2):
        rev = d == 1
        ic, il = rw_in(pc, d), rw_in(pl, d)
        oc, ol = _prefix_scan(_rwkv7_scan, ic, il, s0_rw, rev)
        rw_c, rw_l = rw_c + oc, rw_l + ol
        rb_c, rb_l = rb_c + bonus(ic), rb_l + bonus(il)
        gc_, gl_ = _prefix_scan(_gla_chunked, gla_in(pc, d), gla_in(pl, d), s0_gla, rev)
        gla_c, gla_l = gla_c + gc_, gla_l + gl_
    return rw_c, rb_c, rw_l, rb_l, gla_c, gla_l


def _rwkv_post(o, bon, gd, lp):
    mu = jnp.mean(o, axis=-1, keepdims=True)
    var = jnp.mean(jnp.square(o - mu), axis=-1, keepdims=True)
    y = ((o - mu) * lax.rsqrt(var + RW_GN_EPS)).reshape(*o.shape[:-2], RW_WIDTH)
    y = y * lp['rw_ln_w'] + lp['rw_ln_b'] + bon.reshape(*o.shape[:-2], RW_WIDTH)
    return y * (jax.nn.sigmoid(gd) @ lp['rw_g_up'])


def _gla_post(o, og, lp):
    y = o * lax.rsqrt(jnp.mean(o * o, axis=-1, keepdims=True) + NORM_EPS) * lp['gla_norm_w']
    return y.reshape(*o.shape[:-2], GLA_VW) * jax.nn.silu(og)


def _fourier(u):
    b, t, _ = u.shape
    g = u.astype(jnp.float32).reshape(b, t, FN_GROUPS, FN_GROUP_DIM)
    return jnp.real(jnp.fft.fftn(g, axes=(1, 3), norm='ortho')).reshape(b, t, FN_WIDTH)


def _merge(rw_y, gla_y, fn_y, gates, lp):
    ga, gb, gc = jnp.split(jax.nn.sigmoid(gates), N_BRANCH, axis=-1)
    m = ga * (rw_y @ lp['proj_a']) + gb * (gla_y @ lp['proj_b']) + gc * (fn_y @ lp['proj_c'])
    return m @ lp['w_out']


def _token_mixers(z_lat, z_ctx, lp, rows, col_major, need_ctx):
    offs = np.cumsum(np.array(IN_SPLITS))[:-1].tolist()

    def prep(z, r):
        conv = _short_conv(z[..., :CONV_CH], lp['conv_w'], r)
        zz = jnp.concatenate([conv, z[..., CONV_CH:]], axis=-1).astype(jnp.float32)
        return jnp.split(zz, offs, axis=-1)

    pl = prep(z_lat, rows)
    pc = prep(z_ctx, 1)
    scan_l = pl[:N_SCAN_PARTS]
    if col_major:
        scan_l = [_to_col_major(a, rows) for a in scan_l]
    rw_c, rb_c, rw_l, rb_l, gla_c, gla_l = _scan_branches(pc[:N_SCAN_PARTS], scan_l, lp)
    if col_major:
        rw_l, rb_l, gla_l = (_from_col_major(rw_l, rows), _from_col_major(rb_l, rows),
                             _from_col_major(gla_l, rows))

    def post(p, rw_o, rb_o, gla_o):
        return _merge(_rwkv_post(rw_o, rb_o, p[9], lp), _gla_post(gla_o, p[10], lp),
                      _fourier(p[11]), p[12], lp)

    m_lat = post(pl, rw_l, rb_l, gla_l).astype(z_lat.dtype)
    m_ctx = post(pc, rw_c, rb_c, gla_c).astype(z_ctx.dtype) if need_ctx else None
    return m_lat, m_ctx


def _ec_moe(u, lp):
    b, t, _ = u.shape
    cap = CAPACITY_FACTOR * t // N_EXPERTS
    aff = jax.nn.softmax((u @ lp['router']).astype(jnp.float32), axis=-1)
    gate, idx = lax.top_k(jnp.swapaxes(aff, 1, 2), cap)
    bidx = jnp.arange(b)[:, None, None]
    xs = u[bidx, idx]
    hid = jax.nn.silu(jnp.einsum('becd,edf->becf', xs, lp['w1'])) * jnp.einsum('becd,edf->becf', xs, lp['w3'])
    ys = jnp.einsum('becf,efd->becd', hid, lp['w2']) * gate[..., None].astype(u.dtype)
    return jnp.zeros_like(u).at[bidx, idx].add(ys.astype(u.dtype))


def _layer(h_lat, h_ctx, mod_lat, mod_ctx, lp, rows, col_major, need_ctx):
    sh1, sc1, g1, sh2, sc2, g2 = jnp.split(mod_lat[:, None, :], N_MOD, axis=-1)
    csh1, csc1, cg1, csh2, csc2, cg2 = jnp.split(mod_ctx, N_MOD, axis=-1)
    u_lat = _rmsnorm(h_lat, lp['norm1']) * (1.0 + sc1) + sh1
    u_ctx = _rmsnorm(h_ctx, lp['norm1']) * (1.0 + csc1) + csh1
    m_lat, m_ctx = _token_mixers(u_lat @ lp['w_in'], u_ctx @ lp['w_in'], lp, rows, col_major, need_ctx)
    h_lat = h_lat + g1 * m_lat
    v_lat = _rmsnorm(h_lat, lp['norm2']) * (1.0 + sc2) + sh2
    h_lat = h_lat + g2 * _ec_moe(v_lat, lp)
    if need_ctx:
        h_ctx = h_ctx + cg1 * m_ctx
        v_ctx = _rmsnorm(h_ctx, lp['norm2']) * (1.0 + csc2) + csh2
        h_ctx = h_ctx + cg2 * _ec_moe(v_ctx, lp)
    return h_lat, h_ctx


def setup_inputs(seed: int = 0) -> dict:
    key = jax.random.key(seed)
    ks = iter(jax.random.split(key, 40))
    nrm = lambda shape, scale: jax.random.normal(next(ks), shape, jnp.float32) * scale
    L, D, E, F = DEPTH, D_MODEL, N_EXPERTS, EXPERT_HIDDEN
    return {
        'x': nrm((BATCH, SEQ, D), 1.0),
        'c': nrm((BATCH, D), 1.0),
        'ctx': nrm((BATCH, CTX_LEN, D), 1.0),
        'c_ctx': nrm((D,), 1.0),
        'w_mod': nrm((L, D, N_MOD * D), 0.5 * D ** -0.5),
        'b_mod': nrm((L, N_MOD * D), 0.01),
        'norm1_w': 1.0 + nrm((L, D), 0.02),
        'norm2_w': 1.0 + nrm((L, D), 0.02),
        'w_in': nrm((L, D, IN_COLS), D ** -0.5),
        'conv_w': nrm((L, CONV_K, CONV_K, CONV_CH), 1.0 / CONV_K),
        'rw_w0': nrm((L, 2, RW_WIDTH), 0.5),
        'rw_w_up': nrm((L, 2, RW_DECAY_RANK, RW_WIDTH), RW_DECAY_RANK ** -0.5),
        'rw_a0': nrm((L, 2, RW_WIDTH), 0.5),
        'rw_a_up': nrm((L, 2, RW_A_RANK, RW_WIDTH), RW_A_RANK ** -0.5),
        'rw_k_k': 0.85 + nrm((L, RW_WIDTH), 0.02),
        'rw_k_a': 1.0 + nrm((L, RW_WIDTH), 0.02),
        'rw_r_k': nrm((L, RW_HEADS, RW_HEAD_DIM), 0.1),
        'rw_g_up': nrm((L, RW_GATE_RANK, RW_WIDTH), RW_GATE_RANK ** -0.5),
        'rw_ln_w': 1.0 + nrm((L, RW_WIDTH), 0.02),
        'rw_ln_b': nrm((L, RW_WIDTH), 0.01),
        'gla_a_up': nrm((L, 2, GLA_RANK, GLA_KW), GLA_RANK ** -0.5),
        'gla_a_b': nrm((L, 2, GLA_KW), 0.1),
        'gla_norm_w': 1.0 + nrm((L, GLA_DV), 0.02),
        'proj_a': nrm((L, RW_WIDTH, D), RW_WIDTH ** -0.5),
        'proj_b': nrm((L, GLA_VW, D), GLA_VW ** -0.5),
        'proj_c': nrm((L, FN_WIDTH, D), FN_WIDTH ** -0.5),
        'w_out': nrm((L, D, D), D ** -0.5),
        'router': nrm((L, D, E), D ** -0.5),
        'exp_w1': nrm((L, E, D, F), D ** -0.5),
        'exp_w3': nrm((L, E, D, F), D ** -0.5),
        'exp_w2': nrm((L, E, F, D), F ** -0.5),
        'final_norm_w': 1.0 + nrm((D,), 0.02),
    }


def reference(x, c, ctx, c_ctx, w_mod, b_mod, norm1_w, norm2_w, w_in, conv_w, rw_w0, rw_w_up, rw_a0,
              rw_a_up, rw_k_k, rw_k_a, rw_r_k, rw_g_up, rw_ln_w, rw_ln_b, gla_a_up, gla_a_b, gla_norm_w,
              proj_a, proj_b, proj_c, w_out, router, exp_w1, exp_w3, exp_w2, final_norm_w):
    rows = x.shape[1] // GRID_W
    h_lat, h_ctx = x, ctx
    for i in range(DEPTH):
        lp = {'norm1': norm1_w[i], 'norm2': norm2_w[i], 'w_in': w_in[i], 'conv_w': conv_w[i],
              'rw_w0': rw_w0[i], 'rw_w_up': rw_w_up[i], 'rw_a0': rw_a0[i], 'rw_a_up': rw_a_up[i],
              'rw_k_k': rw_k_k[i], 'rw_k_a': rw_k_a[i], 'rw_r_k': rw_r_k[i], 'rw_g_up': rw_g_up[i],
              'rw_ln_w': rw_ln_w[i], 'rw_ln_b': rw_ln_b[i], 'gla_a_up': gla_a_up[i], 'gla_a_b': gla_a_b[i],
              'gla_norm_w': gla_norm_w[i], 'proj_a': proj_a[i], 'proj_b': proj_b[i], 'proj_c': proj_c[i],
              'w_out': w_out[i], 'router': router[i], 'w1': exp_w1[i], 'w3': exp_w3[i], 'w2': exp_w2[i]}
        mod_lat = jax.nn.silu(c) @ w_mod[i] + b_mod[i]
        mod_ctx = jax.nn.silu(c_ctx) @ w_mod[i] + b_mod[i]
        h_lat, h_ctx = _layer(h_lat, h_ctx, mod_lat, mod_ctx, lp, rows,
                              col_major=(i % 2 == 1), need_ctx=(i < DEPTH - 1))
    return _rmsnorm(h_lat, final_norm_w)
```

```python
import functools
import math

import jax
import jax.numpy as jnp
import numpy as np
from jax import lax
from jax.experimental import pallas as pl
from jax.experimental.pallas import tpu as pltpu

F32 = jnp.float32
BF16 = jnp.bfloat16
HI = lax.Precision.HIGHEST

_GRID_W = 64
_NORM_EPS = 1e-6
_N_MOD = 6
_RW_HEADS = 6
_RW_HD = 64
_RW_WIDTH = _RW_HEADS * _RW_HD
_RW_LORA = 64
_RW_GATE_RANK = 128
_RW_GN_EPS = 64e-5
_GLA_HEADS = 4
_GLA_DK = 48
_GLA_DV = 96
_GLA_RANK = 16
_GLA_TAU = 16.0
_CHUNK = 64
_FN_GROUPS = 4
_FN_GD = 64
_FN_WIDTH = _FN_GROUPS * _FN_GD
_LANE = 128
_GLA_PW = _GLA_HEADS * _LANE
_VMEM_LIMIT = 56 * 1024 * 1024


def _cp(sem, vmem=None):
    return pltpu.CompilerParams(dimension_semantics=sem, vmem_limit_bytes=vmem)


def _tile(n, cap, mult=128):
    if n <= cap:
        return n
    best = None
    for t in range(mult, cap + 1, mult):
        if n % t == 0:
            best = t
    assert best is not None, (n, cap)
    return best


def _sigmoid(x):
    return 1.0 / (1.0 + jnp.exp(-x))


def _silu(x):
    return x * _sigmoid(x)


def _softplus(x):
    return jnp.maximum(x, 0.0) + jnp.log(1.0 + jnp.exp(-jnp.abs(x)))


def _dot(a, b, precision=None):
    return jnp.dot(a, b, preferred_element_type=F32, precision=precision)


def _dot_nt(a, b, precision=None):
    return lax.dot_general(a, b, (((1,), (1,)), ((), ())), preferred_element_type=F32, precision=precision)


def _dot_tn(a, b, precision=None):
    return lax.dot_general(a, b, (((0,), (0,)), ((), ())), preferred_element_type=F32, precision=precision)


def _mod_kernel(c_ref, w_ref, b_ref, o_ref):
    o_ref[0] = _dot(_silu(c_ref[...]), w_ref[0], HI) + b_ref[0]


def _mod_call(cc, w_mod, b_mod):
    L, D, N = w_mod.shape
    R = cc.shape[0]
    tn = _tile(N, 1536)
    return pl.pallas_call(
        _mod_kernel,
        grid=(L, N // tn),
        in_specs=[pl.BlockSpec((R, D), lambda l, j: (0, 0)),
                  pl.BlockSpec((1, D, tn), lambda l, j: (l, 0, j)),
                  pl.BlockSpec((1, 1, tn), lambda l, j: (l, 0, j))],
        out_specs=pl.BlockSpec((1, R, tn), lambda l, j: (l, 0, j)),
        out_shape=jax.ShapeDtypeStruct((L, R, N), F32),
        compiler_params=_cp(("arbitrary", "arbitrary"), _VMEM_LIMIT),
        name="adaln_mod",
    )(cc, w_mod, b_mod.reshape(L, 1, N))


def _modnorm(x, nw, sc, sh):
    y = x * lax.rsqrt(jnp.mean(x * x, axis=-1, keepdims=True) + _NORM_EPS)
    return (y * nw) * (1.0 + sc) + sh


def _nm_kernel(h_ref, nw_ref, sc_ref, sh_ref, w_ref, o_ref, u_ref):
    @pl.when(pl.program_id(2) == 0)
    def _():
        u_ref[...] = _modnorm(h_ref[0], nw_ref[...], sc_ref[0], sh_ref[0]).astype(BF16)

    o_ref[0] = _dot(u_ref[...], w_ref[...])


def _norm_matmul(h, nw, sc, sh, w):
    B, T, D = h.shape
    N = w.shape[1]
    tm = _tile(T, 512, 8)
    tn = _tile(N, 1536)
    return pl.pallas_call(
        _nm_kernel,
        grid=(B, T // tm, N // tn),
        in_specs=[pl.BlockSpec((1, tm, D), lambda b, i, j: (b, i, 0)),
                  pl.BlockSpec((1, D), lambda b, i, j: (0, 0)),
                  pl.BlockSpec((1, 1, D), lambda b, i, j: (b, 0, 0)),
                  pl.BlockSpec((1, 1, D), lambda b, i, j: (b, 0, 0)),
                  pl.BlockSpec((D, tn), lambda b, i, j: (0, j))],
        out_specs=pl.BlockSpec((1, tm, tn), lambda b, i, j: (b, i, j)),
        out_shape=jax.ShapeDtypeStruct((B, T, N), F32),
        scratch_shapes=[pltpu.VMEM((tm, D), BF16)],
        compiler_params=_cp(("arbitrary", "arbitrary", "arbitrary"), _VMEM_LIMIT),
        name="norm_in_proj",
    )(h, nw, sc, sh, w)


def _conv_kernel(x_ref, w_ref, o_ref, xp_ref, *, rows, W, T, PAD, CH):
    cw = x_ref.shape[2]
    xp_ref[0:PAD, :] = jnp.zeros((PAD, cw), F32)
    xp_ref[PAD + T:PAD + T + PAD, :] = jnp.zeros((PAD, cw), F32)
    xp_ref[PAD:PAD + T, :] = x_ref[0]
    for c0 in range(0, T, CH):
        col = jnp.bitwise_and(lax.broadcasted_iota(jnp.int32, (CH, cw), 0) + c0, W - 1)
        acc = jnp.zeros((CH, cw), F32)
        for a in range(3):
            if rows == 1 and a != 1:
                continue
            for b in range(3):
                off = (a - 1) * W + (b - 1)
                xs = xp_ref[PAD + c0 + off:PAD + c0 + off + CH, :]
                if b == 0:
                    xs = jnp.where(col >= 1, xs, 0.0)
                elif b == 2:
                    xs = jnp.where(col <= W - 2, xs, 0.0)
                acc = acc + xs * w_ref[a * 3 + b:a * 3 + b + 1, :]
        o_ref[0, c0:c0 + CH, :] = acc


def _conv_call(z, w9, rows, W):
    B, T, C = z.shape
    assert W & (W - 1) == 0 and rows * W == T
    cw = _LANE
    PAD = W + 8 if rows > 1 else 8
    CH = min(T, 128)
    kern = functools.partial(_conv_kernel, rows=rows, W=W, T=T, PAD=PAD, CH=CH)
    return pl.pallas_call(
        kern,
        grid=(B, C // cw),
        in_specs=[pl.BlockSpec((1, T, cw), lambda b, j: (b, 0, j)),
                  pl.BlockSpec((9, cw), lambda b, j: (0, j))],
        out_specs=pl.BlockSpec((1, T, cw), lambda b, j: (b, 0, j)),
        out_shape=jax.ShapeDtypeStruct((B, T, C), F32),
        scratch_shapes=[pltpu.VMEM((T + 2 * PAD, cw), F32)],
        compiler_params=_cp(("arbitrary", "arbitrary"), _VMEM_LIMIT),
        name="short_conv",
    )(z, w9)


def _seq_view(arr, col_major, rows):
    B, T, C = arr.shape
    return arr.reshape(B, rows, (T // rows) * C) if col_major else arr


def _seq_spec(C, ctot, part, col_major, nblk, reverse):
    nper = ctot // C

    def blk(i):
        return nblk - 1 - i if reverse else i

    if col_major:
        return pl.BlockSpec((1, _CHUNK, C), lambda b, i: (b, 0, blk(i) * nper + part))
    return pl.BlockSpec((1, _CHUNK, C), lambda b, i: (b, blk(i), part))


def _full_spec(shape):
    nd = len(shape)
    return pl.BlockSpec(shape, lambda b, i: (0,) * nd)


def _rwkv_kernel(r_ref, k_ref, v_ref, lo_ref, s0_ref, w0_ref, wup_ref, a0_ref, aup_ref, kkw_ref, kaw_ref,
                 rkw_ref, j_ref, o_ref, bon_ref, sout_ref, S, Wd, KK, BB, KM, RR, VT, OT, *, reverse):
    i = pl.program_id(1)
    TB = _CHUNK
    NP = _RW_HEADS // 2

    @pl.when(i == 0)
    def _():
        S[...] = s0_ref[0]

    r = r_ref[0]
    k = k_ref[0]
    v = v_ref[0]
    lo = lo_ref[0]
    jm = j_ref[...]
    w_log = -_softplus(-(w0_ref[...] + _dot(jnp.tanh(lo[:, 0:_LANE]), wup_ref[...], HI))) - 0.5
    a = _sigmoid(a0_ref[...] + _dot(lo[:, _LANE:2 * _LANE], aup_ref[...], HI))
    kk0 = k * kkw_ref[...]
    kk = kk0 / jnp.maximum(jnp.sqrt(_dot(kk0 * kk0, jm, HI)), 1e-12)
    km = k * (1.0 + (a - 1.0) * kaw_ref[...])
    bon_ref[0] = _dot(r * km * rkw_ref[...], jm, HI) * v
    Wd[...] = jnp.exp(-jnp.exp(w_log))
    KK[...] = kk
    BB[...] = kk * a
    KM[...] = km
    RR[...] = r
    for p in range(NP):
        VT[p] = v[:, p * _LANE:(p + 1) * _LANE].T
        OT[p] = jnp.zeros((_LANE, TB), F32)

    m_a = lax.broadcasted_iota(jnp.int32, (_RW_HD, _LANE), 1) < _RW_HD
    lane_t = lax.broadcasted_iota(jnp.int32, (_LANE, TB), 1)

    def seg_sum(x):
        s_a = jnp.sum(jnp.where(m_a, x, 0.0), axis=1, keepdims=True)
        s_b = jnp.sum(jnp.where(m_a, 0.0, x), axis=1, keepdims=True)
        return s_a, s_b

    SUB = 8

    def group(g, carry):
        base = pl.multiple_of(((TB // SUB - 1 - g) if reverse else g) * SUB, SUB)
        tiles = [[ref[pl.ds(base, SUB), p * _LANE:(p + 1) * _LANE] for ref in (KK, Wd, BB, KM, RR)]
                 for p in range(NP)]
        for jj in range(SUB):
            jr = (SUB - 1 - jj) if reverse else jj
            hit = lane_t == base + jr
            for p in range(NP):
                kk_r, w_r, b_r, km_r, r_r = [tl[jr:jr + 1, :] for tl in tiles[p]]
                s_p = S[p]
                sa_a, sa_b = seg_sum(s_p * kk_r)
                sa = jnp.where(m_a, sa_a, sa_b)
                vc = jnp.sum(jnp.where(hit, VT[p], 0.0), axis=1, keepdims=True)
                v2 = jnp.where(m_a, vc[0:_RW_HD], vc[_RW_HD:2 * _RW_HD])
                s_n = s_p * w_r - sa * b_r + v2 * km_r
                S[p] = s_n
                o_a, o_b = seg_sum(s_n * r_r)
                OT[p] = jnp.where(hit, jnp.concatenate([o_a, o_b], axis=0), OT[p])
        return carry

    lax.fori_loop(0, TB // SUB, group, 0)
    for p in range(NP):
        o_ref[0, :, p * _LANE:(p + 1) * _LANE] = OT[p].T

    @pl.when(i == pl.num_programs(1) - 1)
    def _():
        sout_ref[0] = S[...]


def _rwkv_call(rkv, lora, s0, wts, col_major, reverse, rows):
    B, T, _ = rkv.shape
    C = _RW_WIDTH
    nblk = T // _CHUNK
    if col_major:
        assert rows == _CHUNK
    rv = _seq_view(rkv, col_major, rows)
    lv = _seq_view(lora, col_major, rows)
    NP = _RW_HEADS // 2
    seq = functools.partial(_seq_spec, col_major=col_major, nblk=nblk, reverse=reverse)
    out_shape = rv.shape[:2] + (rv.shape[2] // 3,)
    st_spec = pl.BlockSpec((1, NP, _RW_HD, _LANE), lambda b, i: (b, 0, 0, 0))
    o, bon, s_out = pl.pallas_call(
        functools.partial(_rwkv_kernel, reverse=reverse),
        grid=(B, nblk),
        in_specs=[seq(C, 3 * C, 0), seq(C, 3 * C, 1), seq(C, 3 * C, 2), seq(C, C, 0), st_spec,
                  _full_spec((1, C)), _full_spec((2 * _RW_LORA, C)), _full_spec((1, C)),
                  _full_spec((2 * _RW_LORA, C)), _full_spec((1, C)), _full_spec((1, C)), _full_spec((1, C)),
                  _full_spec((C, C))],
        out_specs=[seq(C, C, 0), seq(C, C, 0), st_spec],
        out_shape=[jax.ShapeDtypeStruct(out_shape, F32), jax.ShapeDtypeStruct(out_shape, F32),
                   jax.ShapeDtypeStruct((B, NP, _RW_HD, _LANE), F32)],
        scratch_shapes=[pltpu.VMEM((NP, _RW_HD, _LANE), F32)] + [pltpu.VMEM((_CHUNK, C), F32)] * 5
        + [pltpu.VMEM((NP, _LANE, _CHUNK), F32)] * 2,
        compiler_params=_cp(("arbitrary", "arbitrary"), _VMEM_LIMIT),
        name="rwkv7_scan",
    )(rv, rv, rv, lv, s0, *wts)
    return o.reshape(B, T, C), bon.reshape(B, T, C), s_out


def _gla_kernel(q_ref, k_ref, v_ref, lo_ref, s0_ref, aup_ref, ab_ref, o_ref, sout_ref, ST, *, reverse):
    i = pl.program_id(1)
    TB = _CHUNK

    @pl.when(i == 0)
    def _():
        ST[...] = s0_ref[0]

    q = _silu(q_ref[0]) * (_GLA_DK ** -0.5)
    k = _silu(k_ref[0])
    v = _silu(v_ref[0])
    gad = lo_ref[0][:, 2 * _LANE:3 * _LANE]
    log_a = -_softplus(-(_dot(gad, aup_ref[...], HI) + ab_ref[...])) / _GLA_TAU
    row = lax.broadcasted_iota(jnp.int32, (TB, TB), 0)
    col = lax.broadcasted_iota(jnp.int32, (TB, TB), 1)
    keep = (row <= col) if reverse else (row >= col)
    g_cum = _dot(jnp.where(keep, 1.0, 0.0), log_a, HI)
    g_last = g_cum[0:1] if reverse else g_cum[TB - 1:TB]
    q_in = q * jnp.exp(g_cum)
    k_in = k * jnp.exp(-g_cum)
    k_tail = k * jnp.exp(g_last - g_cum)
    dec = jnp.exp(g_last)
    for h in range(_GLA_HEADS):
        sl = slice(h * _LANE, (h + 1) * _LANE)
        qh = q_in[:, sl].astype(BF16)
        att = jnp.where(keep, _dot_nt(qh, k_in[:, sl].astype(BF16)), 0.0)
        st = ST[h]
        vh = v[:, sl].astype(BF16)
        o_ref[0, :, sl] = _dot(att.astype(BF16), vh) + _dot_nt(qh, st.astype(BF16))
        ST[h] = st * dec[:, sl] + _dot_tn(vh, k_tail[:, sl].astype(BF16))

    @pl.when(i == pl.num_programs(1) - 1)
    def _():
        sout_ref[0] = ST[...]


def _gla_call(qkv, lora, s0, aup, ab, col_major, reverse, rows):
    B, T, _ = qkv.shape
    C = _GLA_PW
    nblk = T // _CHUNK
    if col_major:
        assert rows == _CHUNK
    qv = _seq_view(qkv, col_major, rows)
    lv = _seq_view(lora, col_major, rows)
    seq = functools.partial(_seq_spec, col_major=col_major, nblk=nblk, reverse=reverse)
    out_shape = qv.shape[:2] + (qv.shape[2] // 3,)
    st_spec = pl.BlockSpec((1, _GLA_HEADS, _LANE, _LANE), lambda b, i: (b, 0, 0, 0))
    o, s_out = pl.pallas_call(
        functools.partial(_gla_kernel, reverse=reverse),
        grid=(B, nblk),
        in_specs=[seq(C, 3 * C, 0), seq(C, 3 * C, 1), seq(C, 3 * C, 2), seq(_RW_WIDTH, _RW_WIDTH, 0), st_spec,
                  _full_spec((_LANE, C)), _full_spec((1, C))],
        out_specs=[seq(C, C, 0), st_spec],
        out_shape=[jax.ShapeDtypeStruct(out_shape, F32),
                   jax.ShapeDtypeStruct((B, _GLA_HEADS, _LANE, _LANE), F32)],
        scratch_shapes=[pltpu.VMEM((_GLA_HEADS, _LANE, _LANE), F32)],
        compiler_params=_cp(("arbitrary", "arbitrary"), _VMEM_LIMIT),
        name="gla_chunked",
    )(qv, qv, qv, lv, s0, aup, ab)
    return o.reshape(B, T, C), s_out


def _fourier_kernel(x_ref, cg_ref, sg_ref, w_ref, o_ref, xcs_ref):
    T = x_ref.shape[1]

    @pl.when(pl.program_id(1) == 0)
    def _():
        x = x_ref[0]
        xcs_ref[0:T, :] = _dot(x, cg_ref[...], HI).astype(BF16)
        xcs_ref[T:2 * T, :] = _dot(x, sg_ref[...], HI).astype(BF16)

    o_ref[0] = _dot(w_ref[...], xcs_ref[...])


def _fourier_call(zpost, cg, sg, wt):
    B, T, _ = zpost.shape
    C = _FN_WIDTH
    tm = _tile(T, 512, 8)
    return pl.pallas_call(
        _fourier_kernel,
        grid=(B, T // tm),
        in_specs=[pl.BlockSpec((1, T, C), lambda b, i: (b, 0, 0)),
                  _full_spec((C, C)), _full_spec((C, C)),
                  pl.BlockSpec((tm, 2 * T), lambda b, i: (i, 0))],
        out_specs=pl.BlockSpec((1, tm, C), lambda b, i: (b, i, 0)),
        out_shape=jax.ShapeDtypeStruct((B, T, C), F32),
        scratch_shapes=[pltpu.VMEM((2 * T, C), BF16)],
        compiler_params=_cp(("arbitrary", "arbitrary"), _VMEM_LIMIT),
        name="fnet_dft",
    )(zpost, cg, sg, wt)


def _merge_kernel(rwf_ref, rwb_ref, bnf_ref, bnb_ref, glf_ref, glb_ref, zp_ref, zg_ref, fn_ref, h_ref, g1_ref,
                  jr_ref, lnw_ref, lnb_ref, gup_ref, jg_ref, gnw_ref, pa_ref, pb_ref, pc_ref, wo_ref, o_ref):
    D = h_ref.shape[2]
    zp = zp_ref[0]
    og = zp[:, _FN_WIDTH:_FN_WIDTH + _GLA_PW]
    gd = zp[:, _FN_WIDTH + _GLA_PW:_FN_WIDTH + _GLA_PW + _RW_GATE_RANK]
    o = rwf_ref[0] + rwb_ref[0]
    jr = jr_ref[...]
    mu = _dot(o, jr, HI) * (1.0 / _RW_HD)
    xc = o - mu
    var = _dot(xc * xc, jr, HI) * (1.0 / _RW_HD)
    y = xc * lax.rsqrt(var + _RW_GN_EPS)
    y = y * lnw_ref[...] + lnb_ref[...] + (bnf_ref[0] + bnb_ref[0])
    rw_y = y * _dot(_sigmoid(gd), gup_ref[...], HI)
    g = glf_ref[0] + glb_ref[0]
    ms = _dot(g * g, jg_ref[...], HI) * (1.0 / _GLA_DV)
    gla_y = g * lax.rsqrt(ms + _NORM_EPS) * gnw_ref[...] * _silu(og)
    gates = _sigmoid(zg_ref[0])
    m = (gates[:, 0:D] * _dot(rw_y.astype(BF16), pa_ref[...])
         + gates[:, D:2 * D] * _dot(gla_y.astype(BF16), pb_ref[...])
         + gates[:, 2 * D:3 * D] * _dot(fn_ref[0].astype(BF16), pc_ref[...]))
    o_ref[0] = h_ref[0] + g1_ref[0] * _dot(m.astype(BF16), wo_ref[...])


def _merge_call(rwf, rwb, bnf, bnb, glf, glb, zpost, zgates, fn, h, g1, wts):
    B, T, D = h.shape
    tm = _tile(T, 256, 8)

    def row(c):
        return pl.BlockSpec((1, tm, c), lambda b, i: (b, i, 0))

    acts = [rwf, rwb, bnf, bnb, glf, glb, zpost, zgates, fn, h]
    return pl.pallas_call(
        _merge_kernel,
        grid=(B, T // tm),
        in_specs=[row(a.shape[2]) for a in acts] + [pl.BlockSpec((1, 1, D), lambda b, i: (b, 0, 0))]
        + [_full_spec(w.shape) for w in wts],
        out_specs=row(D),
        out_shape=jax.ShapeDtypeStruct((B, T, D), F32),
        compiler_params=_cp(("arbitrary", "arbitrary"), _VMEM_LIMIT),
        name="branch_merge",
    )(*acts, g1, *wts)


def _router_kernel(h_ref, nw_ref, sc_ref, sh_ref, rt_ref, v_ref, aff_ref):
    u = _modnorm(h_ref[0], nw_ref[...], sc_ref[0], sh_ref[0])
    v_ref[0] = u.astype(BF16)
    logits = _dot_nt(rt_ref[...], u, HI)
    e = jnp.exp(logits - jnp.max(logits, axis=0, keepdims=True))
    aff_ref[0] = e / jnp.sum(e, axis=0, keepdims=True)


def _router_call(h, nw, sc, sh, router_t):
    B, T, D = h.shape
    E = router_t.shape[0]
    tm = _tile(T, 512, 128)
    return pl.pallas_call(
        _router_kernel,
        grid=(B, T // tm),
        in_specs=[pl.BlockSpec((1, tm, D), lambda b, i: (b, i, 0)), _full_spec((1, D)),
                  pl.BlockSpec((1, 1, D), lambda b, i: (b, 0, 0)), pl.BlockSpec((1, 1, D), lambda b, i: (b, 0, 0)),
                  _full_spec((E, D))],
        out_specs=[pl.BlockSpec((1, tm, D), lambda b, i: (b, i, 0)), pl.BlockSpec((1, E, tm), lambda b, i: (b, 0, i))],
        out_shape=[jax.ShapeDtypeStruct((B, T, D), BF16), jax.ShapeDtypeStruct((B, E, T), F32)],
        compiler_params=_cp(("arbitrary", "arbitrary"), _VMEM_LIMIT),
        name="router_softmax",
    )(h, nw, sc, sh, router_t)


def _topc_kernel(aff_ref, pos_ref, *, cap):
    E, T = aff_ref.shape[1], aff_ref.shape[2]
    x = pltpu.bitcast(aff_ref[0], jnp.int32)

    def body(it, thr):
        cand = thr | lax.shift_left(jnp.int32(1), 30 - it)
        cnt = jnp.sum(jnp.where(x >= cand, 1, 0), axis=1, keepdims=True)
        return jnp.where(cnt >= cap, cand, thr)

    thr = lax.fori_loop(0, 31, body, jnp.zeros((E, 1), jnp.int32))
    need = (cap - jnp.sum(jnp.where(x > thr, 1, 0), axis=1, keepdims=True)).astype(F32)
    blk = _LANE
    upper = jnp.where(lax.broadcasted_iota(jnp.int32, (blk, blk), 0) < lax.broadcasted_iota(jnp.int32, (blk, blk), 1),
                      1.0, 0.0).astype(BF16)
    off_eq = jnp.zeros((E, 1), F32)
    off_sel = jnp.zeros((E, 1), F32)
    for c in range(T // blk):
        sl = slice(c * blk, (c + 1) * blk)
        xc = pltpu.bitcast(aff_ref[0, :, sl], jnp.int32)
        eq_c = jnp.where(xc == thr, 1.0, 0.0)
        rank_eq = _dot(eq_c.astype(BF16), upper) + off_eq
        take = jnp.where(rank_eq < need, eq_c, 0.0)
        sel = jnp.where(xc > thr, 1.0, take)
        rank = _dot(sel.astype(BF16), upper) + off_sel
        pos_ref[0, :, sl] = jnp.where(sel > 0.0, rank.astype(jnp.int32), -1)
        off_eq = off_eq + jnp.sum(eq_c, axis=1, keepdims=True)
        off_sel = off_sel + jnp.sum(sel, axis=1, keepdims=True)


def _topc_call(aff, cap):
    B, E, T = aff.shape
    return pl.pallas_call(
        functools.partial(_topc_kernel, cap=cap),
        grid=(B,),
        in_specs=[pl.BlockSpec((1, E, T), lambda b: (b, 0, 0))],
        out_specs=pl.BlockSpec((1, E, T), lambda b: (b, 0, 0)),
        out_shape=jax.ShapeDtypeStruct((B, E, T), jnp.int32),
        compiler_params=_cp(("arbitrary",), _VMEM_LIMIT),
        name="expert_choice_topc",
    )(aff)


def _ffn_kernel(v_ref, pos_ref, aff_ref, w1_ref, w3_ref, w2_ref, ys_ref, xs_ref, acc_ref, gate_ref, *, cap, tc):
    e = pl.program_id(1)
    f = pl.program_id(2)
    T = v_ref.shape[1]

    @pl.when(f == 0)
    def _():
        slot = lax.broadcasted_iota(jnp.int32, (cap, tc), 0)
        xs = jnp.zeros(xs_ref.shape, F32)
        gate = jnp.zeros((cap, 1), F32)
        for c in range(T // tc):
            sl = slice(c * tc, (c + 1) * tc)
            hit = slot == pos_ref[0, e, :, sl]
            xs = xs + _dot(jnp.where(hit, 1.0, 0.0).astype(BF16), v_ref[0, sl, :])
            gate = gate + jnp.sum(jnp.where(hit, aff_ref[0, e, :, sl], 0.0), axis=1, keepdims=True)
        xs_ref[...] = xs.astype(BF16)
        gate_ref[...] = gate
        acc_ref[...] = jnp.zeros(acc_ref.shape, F32)

    x = xs_ref[...]
    h1 = _dot(x, w1_ref[0])
    hid = _silu(h1) * _dot(x, w3_ref[0])
    acc_ref[...] += _dot(hid.astype(BF16), w2_ref[0])

    @pl.when(f == pl.num_programs(2) - 1)
    def _():
        ys_ref[0, 0] = (acc_ref[...] * gate_ref[...]).astype(BF16)


def _ffn_call(v, pos, aff, w1, w3, w2, cap):
    B, T, D = v.shape
    E, _, F = w1.shape
    fc = _tile(F, 512)
    tc = _tile(T, 1024)
    return pl.pallas_call(
        functools.partial(_ffn_kernel, cap=cap, tc=tc),
        grid=(B, E, F // fc),
        in_specs=[pl.BlockSpec((1, T, D), lambda b, e, f: (b, 0, 0)),
                  pl.BlockSpec((1, E, 1, T), lambda b, e, f: (b, 0, 0, 0)),
                  pl.BlockSpec((1, E, 1, T), lambda b, e, f: (b, 0, 0, 0)),
                  pl.BlockSpec((1, D, fc), lambda b, e, f: (e, 0, f)),
                  pl.BlockSpec((1, D, fc), lambda b, e, f: (e, 0, f)),
                  pl.BlockSpec((1, fc, D), lambda b, e, f: (e, f, 0))],
        out_specs=pl.BlockSpec((1, 1, cap, D), lambda b, e, f: (b, e, 0, 0)),
        out_shape=jax.ShapeDtypeStruct((B, E, cap, D), BF16),
        scratch_shapes=[pltpu.VMEM((cap, D), BF16), pltpu.VMEM((cap, D), F32), pltpu.VMEM((cap, 1), F32)],
        compiler_params=_cp(("arbitrary", "arbitrary", "arbitrary"), _VMEM_LIMIT),
        name="expert_ffn",
    )(v, pos, aff, w1, w3, w2)


def _scatter_kernel(pos_ref, ys_ref, h_ref, g2_ref, o_ref, acc_ref, *, cap):
    e = pl.program_id(2)
    tm = h_ref.shape[1]

    @pl.when(e == 0)
    def _():
        acc_ref[...] = jnp.zeros(acc_ref.shape, F32)

    hit = lax.broadcasted_iota(jnp.int32, (cap, tm), 0) == pos_ref[0, e]
    acc_ref[...] += _dot_tn(jnp.where(hit, 1.0, 0.0).astype(BF16), ys_ref[0, 0])

    @pl.when(e == pl.num_programs(2) - 1)
    def _():
        o_ref[0] = h_ref[0] + g2_ref[0] * acc_ref[...]


def _scatter_call(pos, ys, h, g2, cap):
    B, T, D = h.shape
    E = pos.shape[1]
    tm = _tile(T, 1024)
    return pl.pallas_call(
        functools.partial(_scatter_kernel, cap=cap),
        grid=(B, T // tm, E),
        in_specs=[pl.BlockSpec((1, E, 1, tm), lambda b, i, e: (b, 0, 0, i)),
                  pl.BlockSpec((1, 1, cap, D), lambda b, i, e: (b, e, 0, 0)),
                  pl.BlockSpec((1, tm, D), lambda b, i, e: (b, i, 0)),
                  pl.BlockSpec((1, 1, D), lambda b, i, e: (b, 0, 0))],
        out_specs=pl.BlockSpec((1, tm, D), lambda b, i, e: (b, i, 0)),
        out_shape=jax.ShapeDtypeStruct((B, T, D), F32),
        scratch_shapes=[pltpu.VMEM((tm, D), F32)],
        compiler_params=_cp(("arbitrary", "arbitrary", "arbitrary"), _VMEM_LIMIT),
        name="expert_scatter",
    )(pos, ys, h, g2)


def _final_kernel(h_ref, w_ref, o_ref):
    x = h_ref[0]
    o_ref[0] = x * lax.rsqrt(jnp.mean(x * x, axis=-1, keepdims=True) + _NORM_EPS) * w_ref[...]


def _final_call(h, w):
    B, T, D = h.shape
    tm = _tile(T, 1024, 8)
    return pl.pallas_call(
        _final_kernel,
        grid=(B, T // tm),
        in_specs=[pl.BlockSpec((1, tm, D), lambda b, i: (b, i, 0)), _full_spec((1, D))],
        out_specs=pl.BlockSpec((1, tm, D), lambda b, i: (b, i, 0)),
        out_shape=jax.ShapeDtypeStruct((B, T, D), F32),
        compiler_params=_cp(("arbitrary", "arbitrary"), _VMEM_LIMIT),
        name="final_norm",
    )(h, w)


def _pad_heads(x, nh, d, axis=-1):
    axis = axis % x.ndim
    shp = x.shape
    x = x.reshape(shp[:axis] + (nh, d) + shp[axis + 1:])
    pad = [(0, 0)] * x.ndim
    pad[axis + 1] = (0, _LANE - d)
    x = jnp.pad(x, pad)
    return x.reshape(shp[:axis] + (nh * _LANE,) + shp[axis + 1:])


def _pad_to(x, n, axis=-1):
    axis = axis % x.ndim
    pad = [(0, 0)] * x.ndim
    pad[axis] = (0, n - x.shape[axis])
    return jnp.pad(x, pad)


def _block_ones(n, blk):
    i = np.arange(n) // blk
    return jnp.asarray((i[:, None] == i[None, :]).astype(np.float32))


def _dft_consts(T):
    t = jnp.arange(T, dtype=jnp.int32)
    ang = (2.0 * np.pi / T) * ((t[:, None] * t[None, :]) % T).astype(F32)
    wt = jnp.concatenate([jnp.cos(ang), -jnp.sin(ang)], axis=1) * (1.0 / math.sqrt(T))
    g = np.arange(_FN_GD)
    ang_g = 2.0 * np.pi * ((g[:, None] * g[None, :]) % _FN_GD) / _FN_GD
    eye = np.eye(_FN_GROUPS)
    cg = np.kron(eye, np.cos(ang_g)) / math.sqrt(_FN_GD)
    sg = np.kron(eye, np.sin(ang_g)) / math.sqrt(_FN_GD)
    return jnp.asarray(wt, BF16), jnp.asarray(cg, F32), jnp.asarray(sg, F32)


def _layer_weights(i, w_in, conv_w, rw_w0, rw_w_up, rw_a0, rw_a_up, rw_k_k, rw_k_a, rw_r_k, rw_g_up, rw_ln_w,
                   rw_ln_b, gla_a_up, gla_a_b, gla_norm_w, proj_a, proj_b, proj_c, w_out):
    D = w_in.shape[1]
    rw, kw, vw = _RW_WIDTH, _GLA_HEADS * _GLA_DK, _GLA_HEADS * _GLA_DV
    o_gq = 3 * rw
    o_gk = o_gq + kw
    o_gv = o_gk + kw
    o_wd = o_gv + vw
    o_ad = o_wd + 2 * _RW_LORA
    o_ga = o_ad + 2 * _RW_LORA
    o_gd = o_ga + 2 * _GLA_RANK
    o_og = o_gd + _RW_GATE_RANK
    o_fn = o_og + vw
    o_gt = o_fn + _FN_WIDTH
    wi = w_in[i]
    cw = conv_w[i].reshape(9, -1)

    def gla_cols(x):
        return jnp.concatenate([_pad_heads(x[..., o_gq:o_gk], _GLA_HEADS, _GLA_DK),
                                _pad_heads(x[..., o_gk:o_gv], _GLA_HEADS, _GLA_DK),
                                _pad_heads(x[..., o_gv:o_wd], _GLA_HEADS, _GLA_DV)], axis=-1)

    w = {}
    w['in_rw'] = wi[:, 0:o_gq].astype(BF16)
    w['in_gla'] = gla_cols(wi).astype(BF16)
    w['in_lora'] = jnp.concatenate([wi[:, o_wd:o_ga], _pad_to(wi[:, o_ga:o_gd], _LANE)], axis=-1).astype(BF16)
    w['in_post'] = jnp.concatenate([wi[:, o_fn:o_gt], _pad_heads(wi[:, o_og:o_fn], _GLA_HEADS, _GLA_DV),
                                    wi[:, o_gd:o_og]], axis=-1).astype(BF16)
    w['in_gates'] = wi[:, o_gt:].astype(BF16)
    w['conv_rw'] = cw[:, 0:o_gq]
    w['conv_gla'] = gla_cols(cw)
    zl = jnp.zeros((_RW_LORA, rw), F32)
    w['rwkv'] = []
    w['gla'] = []
    jr = _block_ones(rw, _RW_HD)
    for d in range(2):
        wup = jnp.concatenate([rw_w_up[i, d], zl] if d == 0 else [zl, rw_w_up[i, d]], axis=0)
        aup = jnp.concatenate([rw_a_up[i, d], zl] if d == 0 else [zl, rw_a_up[i, d]], axis=0)
        w['rwkv'].append((rw_w0[i, d][None], wup, rw_a0[i, d][None], aup, rw_k_k[i][None], rw_k_a[i][None],
                          rw_r_k[i].reshape(1, rw), jr))
        ga = _pad_heads(gla_a_up[i, d], _GLA_HEADS, _GLA_DK)
        ga = jnp.pad(ga, ((d * _GLA_RANK, _LANE - (d + 1) * _GLA_RANK), (0, 0)))
        w['gla'].append((ga, _pad_heads(gla_a_b[i, d][None], _GLA_HEADS, _GLA_DK)))
    w['merge'] = (jr, rw_ln_w[i][None], rw_ln_b[i][None], rw_g_up[i], _block_ones(_GLA_PW, _LANE),
                  _pad_heads(jnp.tile(gla_norm_w[i], _GLA_HEADS)[None], _GLA_HEADS, _GLA_DV),
                  proj_a[i].astype(BF16), _pad_heads(proj_b[i], _GLA_HEADS, _GLA_DV, axis=0).astype(BF16),
                  proj_c[i].astype(BF16), w_out[i].astype(BF16))
    return w


def _moe(h, nw, sc, sh, g2, router_t, w1, w3, w2):
    B, T, D = h.shape
    E = router_t.shape[0]
    cap = 2 * T // E
    v, aff = _router_call(h, nw, sc, sh, router_t)
    pos = _topc_call(aff, cap).reshape(B, E, 1, T)
    ys = _ffn_call(v, pos, aff.reshape(B, E, 1, T), w1, w3, w2, cap)
    return _scatter_call(pos, ys, h, g2, cap)


def kernel(x, c, ctx, c_ctx, w_mod, b_mod, norm1_w, norm2_w, w_in, conv_w, rw_w0, rw_w_up, rw_a0, rw_a_up, rw_k_k, rw_k_a, rw_r_k, rw_g_up, rw_ln_w, rw_ln_b, gla_a_up, gla_a_b, gla_norm_w, proj_a, proj_b, proj_c, w_out, router, exp_w1, exp_w3, exp_w2, final_norm_w):
    B, S, D = x.shape
    TC = ctx.shape[1]
    L = w_mod.shape[0]
    W = _GRID_W
    rows = S // W
    assert S % _CHUNK == 0 and TC % _CHUNK == 0

    cc = _pad_to(jnp.concatenate([c, c_ctx[None]], axis=0), 16, axis=0)
    mods = _mod_call(cc, w_mod, b_mod)
    dft = {S: _dft_consts(S), TC: _dft_consts(TC)}
    s0_rw = jnp.zeros((B, _RW_HEADS // 2, _RW_HD, _LANE), F32)
    s0_gla = jnp.zeros((B, _GLA_HEADS, _LANE, _LANE), F32)

    h_lat, h_ctx = x, ctx
    for i in range(L):
        col_major = i % 2 == 1
        need_ctx = i < L - 1
        lw = _layer_weights(i, w_in, conv_w, rw_w0, rw_w_up, rw_a0, rw_a_up, rw_k_k, rw_k_a, rw_r_k, rw_g_up,
                            rw_ln_w, rw_ln_b, gla_a_up, gla_a_b, gla_norm_w, proj_a, proj_b, proj_c, w_out)
        m_lat = mods[i, :B].reshape(B, 1, _N_MOD, D)
        m_ctx = jnp.broadcast_to(mods[i, B:B + 1].reshape(1, 1, _N_MOD, D), (B, 1, _N_MOD, D))
        n1 = norm1_w[i][None]
        n2 = norm2_w[i][None]

        def in_proj(h, m, img_rows, img_w):
            z = {k: _norm_matmul(h, n1, m[:, :, 1], m[:, :, 0], lw['in_' + k])
                 for k in ('rw', 'gla', 'lora', 'post', 'gates')}
            z['rw'] = _conv_call(z['rw'], lw['conv_rw'], img_rows, img_w)
            z['gla'] = _conv_call(z['gla'], lw['conv_gla'], img_rows, img_w)
            return z

        z_lat = in_proj(h_lat, m_lat, rows, W)
        z_ctx = in_proj(h_ctx, m_ctx, 1, TC)

        rw_lat, rw_ctx, gl_lat, gl_ctx = [], [], [], []
        for d in range(2):
            rev = d == 1
            o_c, b_c, st = _rwkv_call(z_ctx['rw'], z_ctx['lora'], s0_rw, lw['rwkv'][d], False, rev, 1)
            o_l, b_l, _ = _rwkv_call(z_lat['rw'], z_lat['lora'], st, lw['rwkv'][d], col_major, rev, rows)
            rw_ctx.append((o_c, b_c))
            rw_lat.append((o_l, b_l))
            g_c, st = _gla_call(z_ctx['gla'], z_ctx['lora'], s0_gla, *lw['gla'][d], False, rev, 1)
            g_l, _ = _gla_call(z_lat['gla'], z_lat['lora'], st, *lw['gla'][d], col_major, rev, rows)
            gl_ctx.append(g_c)
            gl_lat.append(g_l)

        def mix(h, m, z, rwo, glo, T):
            wt, cg, sg = dft[T]
            fn = _fourier_call(z['post'], cg, sg, wt)
            return _merge_call(rwo[0][0], rwo[1][0], rwo[0][1], rwo[1][1], glo[0], glo[1], z['post'], z['gates'],
                               fn, h, m[:, :, 2], lw['merge'])

        router_t = router[i].T
        w1 = exp_w1[i].astype(BF16)
        w3 = exp_w3[i].astype(BF16)
        w2 = exp_w2[i].astype(BF16)
        h_lat = mix(h_lat, m_lat, z_lat, rw_lat, gl_lat, S)
        h_lat = _moe(h_lat, n2, m_lat[:, :, 4], m_lat[:, :, 3], m_lat[:, :, 5], router_t, w1, w3, w2)
        if need_ctx:
            h_ctx = mix(h_ctx, m_ctx, z_ctx, rw_ctx, gl_ctx, TC)
            h_ctx = _moe(h_ctx, n2, m_ctx[:, :, 4], m_ctx[:, :, 3], m_ctx[:, :, 5], router_t, w1, w3, w2)
    return _final_call(h_lat, final_norm_w[None])
```

```python
import functools
import math

import jax
import jax.numpy as jnp
import numpy as np
from jax import lax
from jax.experimental import pallas as pl
from jax.experimental.pallas import tpu as pltpu

F32 = jnp.float32
BF16 = jnp.bfloat16
HI = lax.Precision.HIGHEST

_GRID_W = 64
_NORM_EPS = 1e-6
_N_MOD = 6
_RW_HEADS = 6
_RW_HD = 64
_RW_WIDTH = _RW_HEADS * _RW_HD
_RW_LORA = 64
_RW_GATE_RANK = 128
_RW_GN_EPS = 64e-5
_GLA_HEADS = 4
_GLA_DK = 48
_GLA_DV = 96
_GLA_RANK = 16
_GLA_TAU = 16.0
_CHUNK = 64
_FN_GROUPS = 4
_FN_GD = 64
_FN_WIDTH = _FN_GROUPS * _FN_GD
_LANE = 128
_GLA_PW = _GLA_HEADS * _LANE
_VMEM_LIMIT = 56 * 1024 * 1024


def _cp(sem, vmem=None):
    return pltpu.CompilerParams(dimension_semantics=sem, vmem_limit_bytes=vmem)


def _tile(n, cap, mult=128):
    if n <= cap:
        return n
    best = None
    for t in range(mult, cap + 1, mult):
        if n % t == 0:
            best = t
    assert best is not None, (n, cap)
    return best


def _sigmoid(x):
    return 1.0 / (1.0 + jnp.exp(-x))


def _silu(x):
    return x * _sigmoid(x)


def _softplus(x):
    return jnp.maximum(x, 0.0) + jnp.log(1.0 + jnp.exp(-jnp.abs(x)))


def _dot(a, b, precision=None):
    return jnp.dot(a, b, preferred_element_type=F32, precision=precision)


def _dot_nt(a, b, precision=None):
    return lax.dot_general(a, b, (((1,), (1,)), ((), ())), preferred_element_type=F32, precision=precision)


def _dot_tn(a, b, precision=None):
    return lax.dot_general(a, b, (((0,), (0,)), ((), ())), preferred_element_type=F32, precision=precision)


def _mod_kernel(c_ref, w_ref, b_ref, o_ref):
    o_ref[0] = _dot(_silu(c_ref[...]), w_ref[0], HI) + b_ref[0]


def _mod_call(cc, w_mod, b_mod):
    L, D, N = w_mod.shape
    R = cc.shape[0]
    tn = _tile(N, 1536)
    return pl.pallas_call(
        _mod_kernel,
        grid=(L, N // tn),
        in_specs=[pl.BlockSpec((R, D), lambda l, j: (0, 0)),
                  pl.BlockSpec((1, D, tn), lambda l, j: (l, 0, j)),
                  pl.BlockSpec((1, 1, tn), lambda l, j: (l, 0, j))],
        out_specs=pl.BlockSpec((1, R, tn), lambda l, j: (l, 0, j)),
        out_shape=jax.ShapeDtypeStruct((L, R, N), F32),
        compiler_params=_cp(("arbitrary", "arbitrary"), _VMEM_LIMIT),
        name="adaln_mod",
    )(cc, w_mod, b_mod.reshape(L, 1, N))


def _modnorm(x, nw, sc, sh):
    y = x * lax.rsqrt(jnp.mean(x * x, axis=-1, keepdims=True) + _NORM_EPS)
    return (y * nw) * (1.0 + sc) + sh


def _nm_kernel(h_ref, nw_ref, sc_ref, sh_ref, w_ref, o_ref, u_ref):
    @pl.when(pl.program_id(2) == 0)
    def _():
        u_ref[...] = _modnorm(h_ref[0], nw_ref[...], sc_ref[0], sh_ref[0]).astype(BF16)

    o_ref[0] = _dot(u_ref[...], w_ref[...])


def _norm_matmul(h, nw, sc, sh, w):
    B, T, D = h.shape
    N = w.shape[1]
    tm = _tile(T, 512, 8)
    tn = _tile(N, 1536)
    return pl.pallas_call(
        _nm_kernel,
        grid=(B, T // tm, N // tn),
        in_specs=[pl.BlockSpec((1, tm, D), lambda b, i, j: (b, i, 0)),
                  pl.BlockSpec((1, D), lambda b, i, j: (0, 0)),
                  pl.BlockSpec((1, 1, D), lambda b, i, j: (b, 0, 0)),
                  pl.BlockSpec((1, 1, D), lambda b, i, j: (b, 0, 0)),
                  pl.BlockSpec((D, tn), lambda b, i, j: (0, j))],
        out_specs=pl.BlockSpec((1, tm, tn), lambda b, i, j: (b, i, j)),
        out_shape=jax.ShapeDtypeStruct((B, T, N), F32),
        scratch_shapes=[pltpu.VMEM((tm, D), BF16)],
        compiler_params=_cp(("arbitrary", "arbitrary", "arbitrary"), _VMEM_LIMIT),
        name="norm_in_proj",
    )(h, nw, sc, sh, w)


def _conv_kernel(x_ref, w_ref, o_ref, xp_ref, *, rows, W, T, PAD, CH):
    cw = x_ref.shape[2]
    xp_ref[0:PAD, :] = jnp.zeros((PAD, cw), F32)
    xp_ref[PAD + T:PAD + T + PAD, :] = jnp.zeros((PAD, cw), F32)
    xp_ref[PAD:PAD + T, :] = x_ref[0]
    for c0 in range(0, T, CH):
        col = jnp.bitwise_and(lax.broadcasted_iota(jnp.int32, (CH, cw), 0) + c0, W - 1)
        acc = jnp.zeros((CH, cw), F32)
        for a in range(3):
            if rows == 1 and a != 1:
                continue
            for b in range(3):
                off = (a - 1) * W + (b - 1)
                xs = xp_ref[PAD + c0 + off:PAD + c0 + off + CH, :]
                if b == 0:
                    xs = jnp.where(col >= 1, xs, 0.0)
                elif b == 2:
                    xs = jnp.where(col <= W - 2, xs, 0.0)
                acc = acc + xs * w_ref[a * 3 + b:a * 3 + b + 1, :]
        o_ref[0, c0:c0 + CH, :] = acc


def _conv_call(z, w9, rows, W):
    B, T, C = z.shape
    assert W & (W - 1) == 0 and rows * W == T
    cw = _LANE
    PAD = W + 8 if rows > 1 else 8
    CH = min(T, 128)
    kern = functools.partial(_conv_kernel, rows=rows, W=W, T=T, PAD=PAD, CH=CH)
    return pl.pallas_call(
        kern,
        grid=(B, C // cw),
        in_specs=[pl.BlockSpec((1, T, cw), lambda b, j: (b, 0, j)),
                  pl.BlockSpec((9, cw), lambda b, j: (0, j))],
        out_specs=pl.BlockSpec((1, T, cw), lambda b, j: (b, 0, j)),
        out_shape=jax.ShapeDtypeStruct((B, T, C), F32),
        scratch_shapes=[pltpu.VMEM((T + 2 * PAD, cw), F32)],
        compiler_params=_cp(("arbitrary", "arbitrary"), _VMEM_LIMIT),
        name="short_conv",
    )(z, w9)


def _seq_view(arr, col_major, rows):
    B, T, C = arr.shape
    return arr.reshape(B, rows, (T // rows) * C) if col_major else arr


def _seq_spec(C, ctot, part, col_major, nblk, reverse):
    nper = ctot // C

    def blk(i):
        return nblk - 1 - i if reverse else i

    if col_major:
        return pl.BlockSpec((1, _CHUNK, C), lambda b, i: (b, 0, blk(i) * nper + part))
    return pl.BlockSpec((1, _CHUNK, C), lambda b, i: (b, blk(i), part))


def _full_spec(shape):
    nd = len(shape)
    return pl.BlockSpec(shape, lambda b, i: (0,) * nd)


def _split(x):
    hi = x.astype(BF16)
    return hi, (x - hi.astype(F32)).astype(BF16)


def _dot3(a, b, nt=False):
    f = _dot_nt if nt else _dot
    ah, al = _split(a)
    bh, bl = _split(b)
    return f(ah, bh) + (f(ah, bl) + f(al, bh))


def _dot_sel(a, sel):
    sel = sel.astype(BF16)
    a1 = a.astype(BF16)
    r1 = a - a1.astype(F32)
    a2 = r1.astype(BF16)
    a3 = (r1 - a2.astype(F32)).astype(BF16)
    return _dot(a1, sel) + (_dot(a2, sel) + _dot(a3, sel))


def _sel_dot(sel, b):
    sel = sel.astype(BF16)
    b1 = b.astype(BF16)
    r1 = b - b1.astype(F32)
    b2 = r1.astype(BF16)
    b3 = (r1 - b2.astype(F32)).astype(BF16)
    return _dot(sel, b1) + (_dot(sel, b2) + _dot(sel, b3))


def _rwkv_kernel(r_ref, k_ref, v_ref, lo_ref, s0_ref, w0_ref, wup_ref, a0_ref, aup_ref, kkw_ref, kaw_ref,
                 rkw_ref, j_ref, o_ref, bon_ref, sout_ref, ZT, *, reverse):
    i = pl.program_id(1)
    TB = _CHUNK
    NP = _RW_HEADS // 2

    @pl.when(i == 0)
    def _():
        ZT[...] = s0_ref[0]

    r = r_ref[0]
    k = k_ref[0]
    v = v_ref[0]
    lo = lo_ref[0]
    jm = j_ref[...]
    w_log = -_softplus(-(w0_ref[...] + _dot3(jnp.tanh(lo[:, 0:_LANE]), wup_ref[...]))) - 0.5
    a = _sigmoid(a0_ref[...] + _dot3(lo[:, _LANE:2 * _LANE], aup_ref[...]))
    kk0 = k * kkw_ref[...]
    kk = kk0 / jnp.maximum(jnp.sqrt(_dot_sel(kk0 * kk0, jm)), 1e-12)
    km = k * (1.0 + (a - 1.0) * kaw_ref[...])
    bon_ref[0] = _dot_sel(r * km * rkw_ref[...], jm) * v

    row = lax.broadcasted_iota(jnp.int32, (TB, TB), 0)
    col = lax.broadcasted_iota(jnp.int32, (TB, TB), 1)
    lw = -jnp.exp(w_log)
    g = _sel_dot(jnp.where((row <= col) if reverse else (row >= col), 1.0, 0.0), lw)
    g_end = g[0:1] if reverse else g[TB - 1:TB]
    pm = kk * jnp.exp(g - lw)
    qm = (kk * a) * jnp.exp(-g)
    khm = km * jnp.exp(-g)
    rhm = r * jnp.exp(g)
    qgm = (kk * a) * jnp.exp(g_end - g)
    kgm = km * jnp.exp(g_end - g)
    gam = jnp.exp(g_end)

    t_i = lax.broadcasted_iota(jnp.int32, (TB, _LANE), 0)
    lane = lax.broadcasted_iota(jnp.int32, (TB, _LANE), 1)
    s_i = jnp.bitwise_and(lane, _RW_HD - 1)
    m_a = lane < _RW_HD
    strict = (s_i > t_i) if reverse else (s_i < t_i)
    incl = (s_i >= t_i) if reverse else (s_i <= t_i)
    eye = jnp.where(s_i == t_i, 1.0, 0.0)
    same = ((lax.broadcasted_iota(jnp.int32, (_LANE, _LANE), 0) < _RW_HD)
            == (lax.broadcasted_iota(jnp.int32, (_LANE, _LANE), 1) < _RW_HD))

    def bd(x):
        return jnp.concatenate([jnp.where(m_a, x, 0.0), jnp.where(m_a, 0.0, x)], axis=0)

    pairs = range(NP)
    sls = [slice(p * _LANE, (p + 1) * _LANE) for p in pairs]
    lo_h, hi_h = slice(0, _LANE), slice(_LANE, 2 * _LANE)
    gram = [_dot3(jnp.concatenate([pm[:, sl], rhm[:, sl]], axis=0),
                  jnp.concatenate([bd(qm[:, sl]), bd(khm[:, sl])], axis=0), nt=True) for sl in sls]
    l_pq = [jnp.where(strict, x[0:TB, lo_h], 0.0) for x in gram]
    l_pk = [jnp.where(strict, x[0:TB, hi_h], 0.0) for x in gram]
    m_rq = [jnp.where(incl, x[TB:2 * TB, lo_h], 0.0) for x in gram]
    m_rk = [jnp.where(incl, x[TB:2 * TB, hi_h], 0.0) for x in gram]
    wm = [_dot(jnp.concatenate([l_pk[p], m_rk[p]], axis=0).astype(BF16), bd(v[:, sls[p]]).astype(BF16))
          for p in pairs]
    kv = [_dot_tn(v[:, sl].astype(BF16), kgm[:, sl].astype(BF16)) for sl in sls]
    s_m = [eye - x for x in l_pq]
    m_m = [_dot3(x, bd(x)) for x in l_pq]
    nlev = TB.bit_length() - 1
    for lev in range(1, nlev):
        if lev < nlev - 1:
            xs = [_dot3(m_m[p], jnp.concatenate([bd(s_m[p]), bd(m_m[p])], axis=1)) for p in pairs]
            s_m = [s_m[p] + xs[p][:, lo_h] for p in pairs]
            m_m = [x[:, hi_h] for x in xs]
        else:
            s_m = [s_m[p] + _dot3(m_m[p], bd(s_m[p])) for p in pairs]
    ta = [_dot3(s_m[p], jnp.concatenate([bd(pm[:, sls[p]]), bd(wm[p][0:TB])], axis=1)) for p in pairs]
    zt = [ZT[p] for p in pairs]
    az = [_dot_nt(jnp.concatenate([ta[p][:, lo_h], rhm[:, sls[p]]], axis=0).astype(BF16), zt[p].astype(BF16))
          for p in pairs]
    u = [az[p][0:TB] + ta[p][:, hi_h] for p in pairs]
    mu = [_dot(m_rq[p].astype(BF16), bd(u[p]).astype(BF16)) for p in pairs]
    qu = [_dot_tn(u[p].astype(BF16), qgm[:, sls[p]].astype(BF16)) for p in pairs]
    for p in pairs:
        o_ref[0, :, sls[p]] = az[p][TB:2 * TB] - mu[p] + wm[p][TB:2 * TB]
        ZT[p] = zt[p] * gam[:, sls[p]] + jnp.where(same, kv[p] - qu[p], 0.0)

    @pl.when(i == pl.num_programs(1) - 1)
    def _():
        sout_ref[0] = ZT[...]


def _rwkv_call(rkv, lora, s0, wts, col_major, reverse, rows):
    B, T, _ = rkv.shape
    C = _RW_WIDTH
    nblk = T // _CHUNK
    if col_major:
        assert rows == _CHUNK
    rv = _seq_view(rkv, col_major, rows)
    lv = _seq_view(lora, col_major, rows)
    NP = _RW_HEADS // 2
    seq = functools.partial(_seq_spec, col_major=col_major, nblk=nblk, reverse=reverse)
    out_shape = rv.shape[:2] + (rv.shape[2] // 3,)
    st_spec = pl.BlockSpec((1, NP, _LANE, _LANE), lambda b, i: (b, 0, 0, 0))
    o, bon, s_out = pl.pallas_call(
        functools.partial(_rwkv_kernel, reverse=reverse),
        grid=(B, nblk),
        in_specs=[seq(C, 3 * C, 0), seq(C, 3 * C, 1), seq(C, 3 * C, 2), seq(C, C, 0), st_spec,
                  _full_spec((1, C)), _full_spec((2 * _RW_LORA, C)), _full_spec((1, C)),
                  _full_spec((2 * _RW_LORA, C)), _full_spec((1, C)), _full_spec((1, C)), _full_spec((1, C)),
                  _full_spec((C, C))],
        out_specs=[seq(C, C, 0), seq(C, C, 0), st_spec],
        out_shape=[jax.ShapeDtypeStruct(out_shape, F32), jax.ShapeDtypeStruct(out_shape, F32),
                   jax.ShapeDtypeStruct((B, NP, _LANE, _LANE), F32)],
        scratch_shapes=[pltpu.VMEM((NP, _LANE, _LANE), F32)],
        compiler_params=_cp(("arbitrary", "arbitrary"), _VMEM_LIMIT),
        name="rwkv7_scan",
    )(rv, rv, rv, lv, s0, *wts)
    return o.reshape(B, T, C), bon.reshape(B, T, C), s_out


def _gla_kernel(q_ref, k_ref, v_ref, lo_ref, s0_ref, aup_ref, ab_ref, o_ref, sout_ref, ST, *, reverse):
    i = pl.program_id(1)
    TB = _CHUNK

    @pl.when(i == 0)
    def _():
        ST[...] = s0_ref[0]

    q = _silu(q_ref[0]) * (_GLA_DK ** -0.5)
    k = _silu(k_ref[0])
    v = _silu(v_ref[0])
    gad = lo_ref[0][:, 2 * _LANE:3 * _LANE]
    log_a = -_softplus(-(_dot3(gad, aup_ref[...]) + ab_ref[...])) / _GLA_TAU
    row = lax.broadcasted_iota(jnp.int32, (TB, TB), 0)
    col = lax.broadcasted_iota(jnp.int32, (TB, TB), 1)
    keep = (row <= col) if reverse else (row >= col)
    g_cum = _sel_dot(jnp.where(keep, 1.0, 0.0), log_a)
    g_last = g_cum[0:1] if reverse else g_cum[TB - 1:TB]
    q_in = q * jnp.exp(g_cum)
    k_in = k * jnp.exp(-g_cum)
    k_tail = k * jnp.exp(g_last - g_cum)
    dec = jnp.exp(g_last)
    heads = range(_GLA_HEADS)
    sls = [slice(h * _LANE, (h + 1) * _LANE) for h in heads]
    qh = [q_in[:, sl].astype(BF16) for sl in sls]
    vh = [v[:, sl].astype(BF16) for sl in sls]
    st = [ST[h] for h in heads]
    att = [_dot_nt(qh[h], k_in[:, sls[h]].astype(BF16)) for h in heads]
    o_inter = [_dot_nt(qh[h], st[h].astype(BF16)) for h in heads]
    kv = [_dot_tn(vh[h], k_tail[:, sls[h]].astype(BF16)) for h in heads]
    o_intra = [_dot(jnp.where(keep, att[h], 0.0).astype(BF16), vh[h]) for h in heads]
    for h in heads:
        o_ref[0, :, sls[h]] = o_intra[h] + o_inter[h]
        ST[h] = st[h] * dec[:, sls[h]] + kv[h]

    @pl.when(i == pl.num_programs(1) - 1)
    def _():
        sout_ref[0] = ST[...]


def _gla_call(qkv, lora, s0, aup, ab, col_major, reverse, rows):
    B, T, _ = qkv.shape
    C = _GLA_PW
    nblk = T // _CHUNK
    if col_major:
        assert rows == _CHUNK
    qv = _seq_view(qkv, col_major, rows)
    lv = _seq_view(lora, col_major, rows)
    seq = functools.partial(_seq_spec, col_major=col_major, nblk=nblk, reverse=reverse)
    out_shape = qv.shape[:2] + (qv.shape[2] // 3,)
    st_spec = pl.BlockSpec((1, _GLA_HEADS, _LANE, _LANE), lambda b, i: (b, 0, 0, 0))
    o, s_out = pl.pallas_call(
        functools.partial(_gla_kernel, reverse=reverse),
        grid=(B, nblk),
        in_specs=[seq(C, 3 * C, 0), seq(C, 3 * C, 1), seq(C, 3 * C, 2), seq(_RW_WIDTH, _RW_WIDTH, 0), st_spec,
                  _full_spec((_LANE, C)), _full_spec((1, C))],
        out_specs=[seq(C, C, 0), st_spec],
        out_shape=[jax.ShapeDtypeStruct(out_shape, F32),
                   jax.ShapeDtypeStruct((B, _GLA_HEADS, _LANE, _LANE), F32)],
        scratch_shapes=[pltpu.VMEM((_GLA_HEADS, _LANE, _LANE), F32)],
        compiler_params=_cp(("arbitrary", "arbitrary"), _VMEM_LIMIT),
        name="gla_chunked",
    )(qv, qv, qv, lv, s0, aup, ab)
    return o.reshape(B, T, C), s_out


def _fourier_kernel(x_ref, cg_ref, sg_ref, w_ref, o_ref, xcs_ref):
    T = x_ref.shape[1]

    @pl.when(pl.program_id(1) == 0)
    def _():
        x = x_ref[0]
        xcs_ref[0:T, :] = _dot(x, cg_ref[...], HI).astype(BF16)
        xcs_ref[T:2 * T, :] = _dot(x, sg_ref[...], HI).astype(BF16)

    o_ref[0] = _dot(w_ref[...], xcs_ref[...])


def _fourier_call(zpost, cg, sg, wt):
    B, T, _ = zpost.shape
    C = _FN_WIDTH
    tm = _tile(T, 512, 8)
    return pl.pallas_call(
        _fourier_kernel,
        grid=(B, T // tm),
        in_specs=[pl.BlockSpec((1, T, C), lambda b, i: (b, 0, 0)),
                  _full_spec((C, C)), _full_spec((C, C)),
                  pl.BlockSpec((tm, 2 * T), lambda b, i: (i, 0))],
        out_specs=pl.BlockSpec((1, tm, C), lambda b, i: (b, i, 0)),
        out_shape=jax.ShapeDtypeStruct((B, T, C), F32),
        scratch_shapes=[pltpu.VMEM((2 * T, C), BF16)],
        compiler_params=_cp(("arbitrary", "arbitrary"), _VMEM_LIMIT),
        name="fnet_dft",
    )(zpost, cg, sg, wt)


def _merge_kernel(rwf_ref, rwb_ref, bnf_ref, bnb_ref, glf_ref, glb_ref, zp_ref, zg_ref, fn_ref, h_ref, g1_ref,
                  jr_ref, lnw_ref, lnb_ref, gup_ref, jg_ref, gnw_ref, pa_ref, pb_ref, pc_ref, wo_ref, o_ref):
    D = h_ref.shape[2]
    zp = zp_ref[0]
    og = zp[:, _FN_WIDTH:_FN_WIDTH + _GLA_PW]
    gd = zp[:, _FN_WIDTH + _GLA_PW:_FN_WIDTH + _GLA_PW + _RW_GATE_RANK]
    o = rwf_ref[0] + rwb_ref[0]
    jr = jr_ref[...]
    mu = _dot_sel(o, jr) * (1.0 / _RW_HD)
    xc = o - mu
    var = _dot_sel(xc * xc, jr) * (1.0 / _RW_HD)
    y = xc * lax.rsqrt(var + _RW_GN_EPS)
    y = y * lnw_ref[...] + lnb_ref[...] + (bnf_ref[0] + bnb_ref[0])
    rw_y = y * _dot3(_sigmoid(gd), gup_ref[...])
    g = glf_ref[0] + glb_ref[0]
    ms = _dot_sel(g * g, jg_ref[...]) * (1.0 / _GLA_DV)
    gla_y = g * lax.rsqrt(ms + _NORM_EPS) * gnw_ref[...] * _silu(og)
    gates = _sigmoid(zg_ref[0])
    m = (gates[:, 0:D] * _dot(rw_y.astype(BF16), pa_ref[...])
         + gates[:, D:2 * D] * _dot(gla_y.astype(BF16), pb_ref[...])
         + gates[:, 2 * D:3 * D] * _dot(fn_ref[0].astype(BF16), pc_ref[...]))
    o_ref[0] = h_ref[0] + g1_ref[0] * _dot(m.astype(BF16), wo_ref[...])


def _merge_call(rwf, rwb, bnf, bnb, glf, glb, zpost, zgates, fn, h, g1, wts):
    B, T, D = h.shape
    tm = _tile(T, 256, 8)

    def row(c):
        return pl.BlockSpec((1, tm, c), lambda b, i: (b, i, 0))

    acts = [rwf, rwb, bnf, bnb, glf, glb, zpost, zgates, fn, h]
    return pl.pallas_call(
        _merge_kernel,
        grid=(B, T // tm),
        in_specs=[row(a.shape[2]) for a in acts] + [pl.BlockSpec((1, 1, D), lambda b, i: (b, 0, 0))]
        + [_full_spec(w.shape) for w in wts],
        out_specs=row(D),
        out_shape=jax.ShapeDtypeStruct((B, T, D), F32),
        compiler_params=_cp(("arbitrary", "arbitrary"), _VMEM_LIMIT),
        name="branch_merge",
    )(*acts, g1, *wts)


def _router_kernel(h_ref, nw_ref, sc_ref, sh_ref, rt_ref, v_ref, aff_ref):
    u = _modnorm(h_ref[0], nw_ref[...], sc_ref[0], sh_ref[0])
    v_ref[0] = u.astype(BF16)
    logits = _dot_nt(rt_ref[...], u, HI)
    e = jnp.exp(logits - jnp.max(logits, axis=0, keepdims=True))
    aff_ref[0] = e / jnp.sum(e, axis=0, keepdims=True)


def _router_call(h, nw, sc, sh, router_t):
    B, T, D = h.shape
    E = router_t.shape[0]
    tm = _tile(T, 512, 128)
    return pl.pallas_call(
        _router_kernel,
        grid=(B, T // tm),
        in_specs=[pl.BlockSpec((1, tm, D), lambda b, i: (b, i, 0)), _full_spec((1, D)),
                  pl.BlockSpec((1, 1, D), lambda b, i: (b, 0, 0)), pl.BlockSpec((1, 1, D), lambda b, i: (b, 0, 0)),
                  _full_spec((E, D))],
        out_specs=[pl.BlockSpec((1, tm, D), lambda b, i: (b, i, 0)), pl.BlockSpec((1, E, tm), lambda b, i: (b, 0, i))],
        out_shape=[jax.ShapeDtypeStruct((B, T, D), BF16), jax.ShapeDtypeStruct((B, E, T), F32)],
        compiler_params=_cp(("arbitrary", "arbitrary"), _VMEM_LIMIT),
        name="router_softmax",
    )(h, nw, sc, sh, router_t)


def _topc_kernel(aff_ref, pos_ref, *, cap):
    E, T = aff_ref.shape[1], aff_ref.shape[2]
    x = pltpu.bitcast(aff_ref[0], jnp.int32)

    def body(it, thr):
        cand = thr | lax.shift_left(jnp.int32(1), 30 - it)
        cnt = jnp.sum(jnp.where(x >= cand, 1, 0), axis=1, keepdims=True)
        return jnp.where(cnt >= cap, cand, thr)

    thr = lax.fori_loop(0, 31, body, jnp.zeros((E, 1), jnp.int32))
    need = (cap - jnp.sum(jnp.where(x > thr, 1, 0), axis=1, keepdims=True)).astype(F32)
    blk = _LANE
    upper = jnp.where(lax.broadcasted_iota(jnp.int32, (blk, blk), 0) < lax.broadcasted_iota(jnp.int32, (blk, blk), 1),
                      1.0, 0.0).astype(BF16)
    off_eq = jnp.zeros((E, 1), F32)
    off_sel = jnp.zeros((E, 1), F32)
    for c in range(T // blk):
        sl = slice(c * blk, (c + 1) * blk)
        xc = pltpu.bitcast(aff_ref[0, :, sl], jnp.int32)
        eq_c = jnp.where(xc == thr, 1.0, 0.0)
        rank_eq = _dot(eq_c.astype(BF16), upper) + off_eq
        take = jnp.where(rank_eq < need, eq_c, 0.0)
        sel = jnp.where(xc > thr, 1.0, take)
        rank = _dot(sel.astype(BF16), upper) + off_sel
        pos_ref[0, :, sl] = jnp.where(sel > 0.0, rank.astype(jnp.int32), -1)
        off_eq = off_eq + jnp.sum(eq_c, axis=1, keepdims=True)
        off_sel = off_sel + jnp.sum(sel, axis=1, keepdims=True)


def _topc_call(aff, cap):
    B, E, T = aff.shape
    return pl.pallas_call(
        functools.partial(_topc_kernel, cap=cap),
        grid=(B,),
        in_specs=[pl.BlockSpec((1, E, T), lambda b: (b, 0, 0))],
        out_specs=pl.BlockSpec((1, E, T), lambda b: (b, 0, 0)),
        out_shape=jax.ShapeDtypeStruct((B, E, T), jnp.int32),
        compiler_params=_cp(("arbitrary",), _VMEM_LIMIT),
        name="expert_choice_topc",
    )(aff)


def _ffn_kernel(v_ref, pos_ref, aff_ref, w1_ref, w3_ref, w2_ref, ys_ref, xs_ref, acc_ref, gate_ref, *, cap, tc):
    e = pl.program_id(1)
    f = pl.program_id(2)
    T = v_ref.shape[1]

    @pl.when(f == 0)
    def _():
        slot = lax.broadcasted_iota(jnp.int32, (cap, tc), 0)
        xs = jnp.zeros(xs_ref.shape, F32)
        gate = jnp.zeros((cap, 1), F32)
        for c in range(T // tc):
            sl = slice(c * tc, (c + 1) * tc)
            hit = slot == pos_ref[0, e, :, sl]
            xs = xs + _dot(jnp.where(hit, 1.0, 0.0).astype(BF16), v_ref[0, sl, :])
            gate = gate + jnp.sum(jnp.where(hit, aff_ref[0, e, :, sl], 0.0), axis=1, keepdims=True)
        xs_ref[...] = xs.astype(BF16)
        gate_ref[...] = gate
        acc_ref[...] = jnp.zeros(acc_ref.shape, F32)

    x = xs_ref[...]
    h1 = _dot(x, w1_ref[0])
    hid = _silu(h1) * _dot(x, w3_ref[0])
    acc_ref[...] += _dot(hid.astype(BF16), w2_ref[0])

    @pl.when(f == pl.num_programs(2) - 1)
    def _():
        ys_ref[0, 0] = (acc_ref[...] * gate_ref[...]).astype(BF16)


def _ffn_call(v, pos, aff, w1, w3, w2, cap):
    B, T, D = v.shape
    E, _, F = w1.shape
    fc = _tile(F, 512)
    tc = _tile(T, 1024)
    return pl.pallas_call(
        functools.partial(_ffn_kernel, cap=cap, tc=tc),
        grid=(B, E, F // fc),
        in_specs=[pl.BlockSpec((1, T, D), lambda b, e, f: (b, 0, 0)),
                  pl.BlockSpec((1, E, 1, T), lambda b, e, f: (b, 0, 0, 0)),
                  pl.BlockSpec((1, E, 1, T), lambda b, e, f: (b, 0, 0, 0)),
                  pl.BlockSpec((1, D, fc), lambda b, e, f: (e, 0, f)),
                  pl.BlockSpec((1, D, fc), lambda b, e, f: (e, 0, f)),
                  pl.BlockSpec((1, fc, D), lambda b, e, f: (e, f, 0))],
        out_specs=pl.BlockSpec((1, 1, cap, D), lambda b, e, f: (b, e, 0, 0)),
        out_shape=jax.ShapeDtypeStruct((B, E, cap, D), BF16),
        scratch_shapes=[pltpu.VMEM((cap, D), BF16), pltpu.VMEM((cap, D), F32), pltpu.VMEM((cap, 1), F32)],
        compiler_params=_cp(("arbitrary", "arbitrary", "arbitrary"), _VMEM_LIMIT),
        name="expert_ffn",
    )(v, pos, aff, w1, w3, w2)


def _scatter_kernel(pos_ref, ys_ref, h_ref, g2_ref, o_ref, acc_ref, *, cap):
    e = pl.program_id(2)
    tm = h_ref.shape[1]

    @pl.when(e == 0)
    def _():
        acc_ref[...] = jnp.zeros(acc_ref.shape, F32)

    hit = lax.broadcasted_iota(jnp.int32, (cap, tm), 0) == pos_ref[0, e]
    acc_ref[...] += _dot_tn(jnp.where(hit, 1.0, 0.0).astype(BF16), ys_ref[0, 0])

    @pl.when(e == pl.num_programs(2) - 1)
    def _():
        o_ref[0] = h_ref[0] + g2_ref[0] * acc_ref[...]


def _scatter_call(pos, ys, h, g2, cap):
    B, T, D = h.shape
    E = pos.shape[1]
    tm = _tile(T, 1024)
    return pl.pallas_call(
        functools.partial(_scatter_kernel, cap=cap),
        grid=(B, T // tm, E),
        in_specs=[pl.BlockSpec((1, E, 1, tm), lambda b, i, e: (b, 0, 0, i)),
                  pl.BlockSpec((1, 1, cap, D), lambda b, i, e: (b, e, 0, 0)),
                  pl.BlockSpec((1, tm, D), lambda b, i, e: (b, i, 0)),
                  pl.BlockSpec((1, 1, D), lambda b, i, e: (b, 0, 0))],
        out_specs=pl.BlockSpec((1, tm, D), lambda b, i, e: (b, i, 0)),
        out_shape=jax.ShapeDtypeStruct((B, T, D), F32),
        scratch_shapes=[pltpu.VMEM((tm, D), F32)],
        compiler_params=_cp(("arbitrary", "arbitrary", "arbitrary"), _VMEM_LIMIT),
        name="expert_scatter",
    )(pos, ys, h, g2)


def _final_kernel(h_ref, w_ref, o_ref):
    x = h_ref[0]
    o_ref[0] = x * lax.rsqrt(jnp.mean(x * x, axis=-1, keepdims=True) + _NORM_EPS) * w_ref[...]


def _final_call(h, w):
    B, T, D = h.shape
    tm = _tile(T, 1024, 8)
    return pl.pallas_call(
        _final_kernel,
        grid=(B, T // tm),
        in_specs=[pl.BlockSpec((1, tm, D), lambda b, i: (b, i, 0)), _full_spec((1, D))],
        out_specs=pl.BlockSpec((1, tm, D), lambda b, i: (b, i, 0)),
        out_shape=jax.ShapeDtypeStruct((B, T, D), F32),
        compiler_params=_cp(("arbitrary", "arbitrary"), _VMEM_LIMIT),
        name="final_norm",
    )(h, w)


def _pad_heads(x, nh, d, axis=-1):
    axis = axis % x.ndim
    shp = x.shape
    x = x.reshape(shp[:axis] + (nh, d) + shp[axis + 1:])
    pad = [(0, 0)] * x.ndim
    pad[axis + 1] = (0, _LANE - d)
    x = jnp.pad(x, pad)
    return x.reshape(shp[:axis] + (nh * _LANE,) + shp[axis + 1:])


def _pad_to(x, n, axis=-1):
    axis = axis % x.ndim
    pad = [(0, 0)] * x.ndim
    pad[axis] = (0, n - x.shape[axis])
    return jnp.pad(x, pad)


def _block_ones(n, blk):
    i = np.arange(n) // blk
    return jnp.asarray((i[:, None] == i[None, :]).astype(np.float32))


def _dft_consts(T):
    t = jnp.arange(T, dtype=jnp.int32)
    ang = (2.0 * np.pi / T) * ((t[:, None] * t[None, :]) % T).astype(F32)
    wt = jnp.concatenate([jnp.cos(ang), -jnp.sin(ang)], axis=1) * (1.0 / math.sqrt(T))
    g = np.arange(_FN_GD)
    ang_g = 2.0 * np.pi * ((g[:, None] * g[None, :]) % _FN_GD) / _FN_GD
    eye = np.eye(_FN_GROUPS)
    cg = np.kron(eye, np.cos(ang_g)) / math.sqrt(_FN_GD)
    sg = np.kron(eye, np.sin(ang_g)) / math.sqrt(_FN_GD)
    return jnp.asarray(wt, BF16), jnp.asarray(cg, F32), jnp.asarray(sg, F32)


def _layer_weights(i, w_in, conv_w, rw_w0, rw_w_up, rw_a0, rw_a_up, rw_k_k, rw_k_a, rw_r_k, rw_g_up, rw_ln_w,
                   rw_ln_b, gla_a_up, gla_a_b, gla_norm_w, proj_a, proj_b, proj_c, w_out):
    D = w_in.shape[1]
    rw, kw, vw = _RW_WIDTH, _GLA_HEADS * _GLA_DK, _GLA_HEADS * _GLA_DV
    o_gq = 3 * rw
    o_gk = o_gq + kw
    o_gv = o_gk + kw
    o_wd = o_gv + vw
    o_ad = o_wd + 2 * _RW_LORA
    o_ga = o_ad + 2 * _RW_LORA
    o_gd = o_ga + 2 * _GLA_RANK
    o_og = o_gd + _RW_GATE_RANK
    o_fn = o_og + vw
    o_gt = o_fn + _FN_WIDTH
    wi = w_in[i]
    cw = conv_w[i].reshape(9, -1)

    def gla_cols(x):
        return jnp.concatenate([_pad_heads(x[..., o_gq:o_gk], _GLA_HEADS, _GLA_DK),
                                _pad_heads(x[..., o_gk:o_gv], _GLA_HEADS, _GLA_DK),
                                _pad_heads(x[..., o_gv:o_wd], _GLA_HEADS, _GLA_DV)], axis=-1)

    w = {}
    w['in_rw'] = wi[:, 0:o_gq].astype(BF16)
    w['in_gla'] = gla_cols(wi).astype(BF16)
    w['in_lora'] = jnp.concatenate([wi[:, o_wd:o_ga], _pad_to(wi[:, o_ga:o_gd], _LANE)], axis=-1).astype(BF16)
    w['in_post'] = jnp.concatenate([wi[:, o_fn:o_gt], _pad_heads(wi[:, o_og:o_fn], _GLA_HEADS, _GLA_DV),
                                    wi[:, o_gd:o_og]], axis=-1).astype(BF16)
    w['in_gates'] = wi[:, o_gt:].astype(BF16)
    w['conv_rw'] = cw[:, 0:o_gq]
    w['conv_gla'] = gla_cols(cw)
    zl = jnp.zeros((_RW_LORA, rw), F32)
    w['rwkv'] = []
    w['gla'] = []
    jr = _block_ones(rw, _RW_HD)
    for d in range(2):
        wup = jnp.concatenate([rw_w_up[i, d], zl] if d == 0 else [zl, rw_w_up[i, d]], axis=0)
        aup = jnp.concatenate([rw_a_up[i, d], zl] if d == 0 else [zl, rw_a_up[i, d]], axis=0)
        w['rwkv'].append((rw_w0[i, d][None], wup, rw_a0[i, d][None], aup, rw_k_k[i][None], rw_k_a[i][None],
                          rw_r_k[i].reshape(1, rw), jr))
        ga = _pad_heads(gla_a_up[i, d], _GLA_HEADS, _GLA_DK)
        ga = jnp.pad(ga, ((d * _GLA_RANK, _LANE - (d + 1) * _GLA_RANK), (0, 0)))
        w['gla'].append((ga, _pad_heads(gla_a_b[i, d][None], _GLA_HEADS, _GLA_DK)))
    w['merge'] = (jr, rw_ln_w[i][None], rw_ln_b[i][None], rw_g_up[i], _block_ones(_GLA_PW, _LANE),
                  _pad_heads(jnp.tile(gla_norm_w[i], _GLA_HEADS)[None], _GLA_HEADS, _GLA_DV),
                  proj_a[i].astype(BF16), _pad_heads(proj_b[i], _GLA_HEADS, _GLA_DV, axis=0).astype(BF16),
                  proj_c[i].astype(BF16), w_out[i].astype(BF16))
    return w


def _moe(h, nw, sc, sh, g2, router_t, w1, w3, w2):
    B, T, D = h.shape
    E = router_t.shape[0]
    cap = 2 * T // E
    v, aff = _router_call(h, nw, sc, sh, router_t)
    pos = _topc_call(aff, cap).reshape(B, E, 1, T)
    ys = _ffn_call(v, pos, aff.reshape(B, E, 1, T), w1, w3, w2, cap)
    return _scatter_call(pos, ys, h, g2, cap)


def kernel(x, c, ctx, c_ctx, w_mod, b_mod, norm1_w, norm2_w, w_in, conv_w, rw_w0, rw_w_up, rw_a0, rw_a_up, rw_k_k, rw_k_a, rw_r_k, rw_g_up, rw_ln_w, rw_ln_b, gla_a_up, gla_a_b, gla_norm_w, proj_a, proj_b, proj_c, w_out, router, exp_w1, exp_w3, exp_w2, final_norm_w):
    B, S, D = x.shape
    TC = ctx.shape[1]
    L = w_mod.shape[0]
    W = _GRID_W
    rows = S // W
    assert S % _CHUNK == 0 and TC % _CHUNK == 0

    cc = _pad_to(jnp.concatenate([c, c_ctx[None]], axis=0), 16, axis=0)
    mods = _mod_call(cc, w_mod, b_mod)
    dft = {S: _dft_consts(S), TC: _dft_consts(TC)}
    s0_rw = jnp.zeros((B, _RW_HEADS // 2, _LANE, _LANE), F32)
    s0_gla = jnp.zeros((B, _GLA_HEADS, _LANE, _LANE), F32)

    h_lat, h_ctx = x, ctx
    for i in range(L):
        col_major = i % 2 == 1
        need_ctx = i < L - 1
        lw = _layer_weights(i, w_in, conv_w, rw_w0, rw_w_up, rw_a0, rw_a_up, rw_k_k, rw_k_a, rw_r_k, rw_g_up,
                            rw_ln_w, rw_ln_b, gla_a_up, gla_a_b, gla_norm_w, proj_a, proj_b, proj_c, w_out)
        m_lat = mods[i, :B].reshape(B, 1, _N_MOD, D)
        m_ctx = jnp.broadcast_to(mods[i, B:B + 1].reshape(1, 1, _N_MOD, D), (B, 1, _N_MOD, D))
        n1 = norm1_w[i][None]
        n2 = norm2_w[i][None]

        def in_proj(h, m, img_rows, img_w):
            z = {k: _norm_matmul(h, n1, m[:, :, 1], m[:, :, 0], lw['in_' + k])
                 for k in ('rw', 'gla', 'lora', 'post', 'gates')}
            z['rw'] = _conv_call(z['rw'], lw['conv_rw'], img_rows, img_w)
            z['gla'] = _conv_call(z['gla'], lw['conv_gla'], img_rows, img_w)
            return z

        z_lat = in_proj(h_lat, m_lat, rows, W)
        z_ctx = in_proj(h_ctx, m_ctx, 1, TC)

        rw_lat, rw_ctx, gl_lat, gl_ctx = [], [], [], []
        for d in range(2):
            rev = d == 1
            o_c, b_c, st = _rwkv_call(z_ctx['rw'], z_ctx['lora'], s0_rw, lw['rwkv'][d], False, rev, 1)
            o_l, b_l, _ = _rwkv_call(z_lat['rw'], z_lat['lora'], st, lw['rwkv'][d], col_major, rev, rows)
            rw_ctx.append((o_c, b_c))
            rw_lat.append((o_l, b_l))
            g_c, st = _gla_call(z_ctx['gla'], z_ctx['lora'], s0_gla, *lw['gla'][d], False, rev, 1)
            g_l, _ = _gla_call(z_lat['gla'], z_lat['lora'], st, *lw['gla'][d], col_major, rev, rows)
            gl_ctx.append(g_c)
            gl_lat.append(g_l)

        def mix(h, m, z, rwo, glo, T):
            wt, cg, sg = dft[T]
            fn = _fourier_call(z['post'], cg, sg, wt)
            return _merge_call(rwo[0][0], rwo[1][0], rwo[0][1], rwo[1][1], glo[0], glo[1], z['post'], z['gates'],
                               fn, h, m[:, :, 2], lw['merge'])

        router_t = router[i].T
        w1 = exp_w1[i].astype(BF16)
        w3 = exp_w3[i].astype(BF16)
        w2 = exp_w2[i].astype(BF16)
        h_lat = mix(h_lat, m_lat, z_lat, rw_lat, gl_lat, S)
        h_lat = _moe(h_lat, n2, m_lat[:, :, 4], m_lat[:, :, 3], m_lat[:, :, 5], router_t, w1, w3, w2)
        if need_ctx:
            h_ctx = mix(h_ctx, m_ctx, z_ctx, rw_ctx, gl_ctx, TC)
            h_ctx = _moe(h_ctx, n2, m_ctx[:, :, 4], m_ctx[:, :, 3], m_ctx[:, :, 5], router_t, w1, w3, w2)
    return _final_call(h_lat, final_norm_w[None])
```

```python
import functools
import math

import jax
import jax.numpy as jnp
import numpy as np
from jax import lax
from jax.experimental import pallas as pl
from jax.experimental.pallas import tpu as pltpu

F32 = jnp.float32
BF16 = jnp.bfloat16
HI = lax.Precision.HIGHEST

_GRID_W = 64
_NORM_EPS = 1e-6
_N_MOD = 6
_RW_HEADS = 6
_RW_HD = 64
_RW_WIDTH = _RW_HEADS * _RW_HD
_RW_LORA = 64
_RW_GATE_RANK = 128
_RW_GN_EPS = 64e-5
_GLA_HEADS = 4
_GLA_DK = 48
_GLA_DV = 96
_GLA_RANK = 16
_GLA_TAU = 16.0
_CHUNK = 64
_FN_GROUPS = 4
_FN_GD = 64
_FN_WIDTH = _FN_GROUPS * _FN_GD
_LANE = 128
_GLA_PW = _GLA_HEADS * _LANE
_VMEM_LIMIT = 56 * 1024 * 1024


def _cp(sem, vmem=None):
    return pltpu.CompilerParams(dimension_semantics=sem, vmem_limit_bytes=vmem)


def _tile(n, cap, mult=128):
    if n <= cap:
        return n
    best = None
    for t in range(mult, cap + 1, mult):
        if n % t == 0:
            best = t
    assert best is not None, (n, cap)
    return best


def _sigmoid(x):
    return 1.0 / (1.0 + jnp.exp(-x))


def _silu(x):
    return x * _sigmoid(x)


def _softplus(x):
    return jnp.maximum(x, 0.0) + jnp.log(1.0 + jnp.exp(-jnp.abs(x)))


def _dot(a, b, precision=None):
    return jnp.dot(a, b, preferred_element_type=F32, precision=precision)


def _dot_nt(a, b, precision=None):
    return lax.dot_general(a, b, (((1,), (1,)), ((), ())), preferred_element_type=F32, precision=precision)


def _dot_tn(a, b, precision=None):
    return lax.dot_general(a, b, (((0,), (0,)), ((), ())), preferred_element_type=F32, precision=precision)


def _mod_kernel(c_ref, w_ref, b_ref, o_ref):
    o_ref[0] = _dot(_silu(c_ref[...]), w_ref[0], HI) + b_ref[0]


def _mod_call(cc, w_mod, b_mod):
    L, D, N = w_mod.shape
    R = cc.shape[0]
    tn = _tile(N, 1536)
    return pl.pallas_call(
        _mod_kernel,
        grid=(L, N // tn),
        in_specs=[pl.BlockSpec((R, D), lambda l, j: (0, 0)),
                  pl.BlockSpec((1, D, tn), lambda l, j: (l, 0, j)),
                  pl.BlockSpec((1, 1, tn), lambda l, j: (l, 0, j))],
        out_specs=pl.BlockSpec((1, R, tn), lambda l, j: (l, 0, j)),
        out_shape=jax.ShapeDtypeStruct((L, R, N), F32),
        compiler_params=_cp(("arbitrary", "arbitrary"), _VMEM_LIMIT),
        name="adaln_mod",
    )(cc, w_mod, b_mod.reshape(L, 1, N))


def _modnorm(x, nw, sc, sh):
    y = x * lax.rsqrt(jnp.mean(x * x, axis=-1, keepdims=True) + _NORM_EPS)
    return (y * nw) * (1.0 + sc) + sh


def _nm_kernel(h_ref, nw_ref, sc_ref, sh_ref, w_ref, o_ref, u_ref):
    @pl.when(pl.program_id(2) == 0)
    def _():
        u_ref[...] = _modnorm(h_ref[0], nw_ref[...], sc_ref[0], sh_ref[0]).astype(BF16)

    o_ref[0] = _dot(u_ref[...], w_ref[...])


def _norm_matmul(h, nw, sc, sh, w):
    B, T, D = h.shape
    N = w.shape[1]
    tm = _tile(T, 512, 8)
    tn = _tile(N, 1536)
    return pl.pallas_call(
        _nm_kernel,
        grid=(B, T // tm, N // tn),
        in_specs=[pl.BlockSpec((1, tm, D), lambda b, i, j: (b, i, 0)),
                  pl.BlockSpec((1, D), lambda b, i, j: (0, 0)),
                  pl.BlockSpec((1, 1, D), lambda b, i, j: (b, 0, 0)),
                  pl.BlockSpec((1, 1, D), lambda b, i, j: (b, 0, 0)),
                  pl.BlockSpec((D, tn), lambda b, i, j: (0, j))],
        out_specs=pl.BlockSpec((1, tm, tn), lambda b, i, j: (b, i, j)),
        out_shape=jax.ShapeDtypeStruct((B, T, N), F32),
        scratch_shapes=[pltpu.VMEM((tm, D), BF16)],
        compiler_params=_cp(("arbitrary", "arbitrary", "arbitrary"), _VMEM_LIMIT),
        name="norm_in_proj",
    )(h, nw, sc, sh, w)


def _conv_kernel(x_ref, w_ref, o_ref, xp_ref, *, rows, W, T, PAD, CH):
    cw = x_ref.shape[2]
    xp_ref[0:PAD, :] = jnp.zeros((PAD, cw), F32)
    xp_ref[PAD + T:PAD + T + PAD, :] = jnp.zeros((PAD, cw), F32)
    xp_ref[PAD:PAD + T, :] = x_ref[0]
    for c0 in range(0, T, CH):
        col = jnp.bitwise_and(lax.broadcasted_iota(jnp.int32, (CH, cw), 0) + c0, W - 1)
        acc = jnp.zeros((CH, cw), F32)
        for a in range(3):
            if rows == 1 and a != 1:
                continue
            for b in range(3):
                off = (a - 1) * W + (b - 1)
                xs = xp_ref[PAD + c0 + off:PAD + c0 + off + CH, :]
                if b == 0:
                    xs = jnp.where(col >= 1, xs, 0.0)
                elif b == 2:
                    xs = jnp.where(col <= W - 2, xs, 0.0)
                acc = acc + xs * w_ref[a * 3 + b:a * 3 + b + 1, :]
        o_ref[0, c0:c0 + CH, :] = acc


def _conv_call(z, w9, rows, W):
    B, T, C = z.shape
    assert W & (W - 1) == 0 and rows * W == T
    cw = _LANE
    PAD = W + 8 if rows > 1 else 8
    CH = min(T, 128)
    kern = functools.partial(_conv_kernel, rows=rows, W=W, T=T, PAD=PAD, CH=CH)
    return pl.pallas_call(
        kern,
        grid=(B, C // cw),
        in_specs=[pl.BlockSpec((1, T, cw), lambda b, j: (b, 0, j)),
                  pl.BlockSpec((9, cw), lambda b, j: (0, j))],
        out_specs=pl.BlockSpec((1, T, cw), lambda b, j: (b, 0, j)),
        out_shape=jax.ShapeDtypeStruct((B, T, C), F32),
        scratch_shapes=[pltpu.VMEM((T + 2 * PAD, cw), F32)],
        compiler_params=_cp(("arbitrary", "arbitrary"), _VMEM_LIMIT),
        name="short_conv",
    )(z, w9)


def _seq_view(arr, col_major, rows):
    B, T, C = arr.shape
    return arr.reshape(B, rows, (T // rows) * C) if col_major else arr


def _seq_spec(C, ctot, part, col_major, nblk, reverse, bb):
    nper = ctot // C

    def blk(i):
        return nblk - 1 - i if reverse else i

    if col_major:
        return pl.BlockSpec((bb, _CHUNK, C), lambda b, i: (b, 0, blk(i) * nper + part))
    return pl.BlockSpec((bb, _CHUNK, C), lambda b, i: (b, blk(i), part))


def _scan_bb(B):
    return 4 if B % 4 == 0 else (2 if B % 2 == 0 else 1)


def _chunk_tri(n, tb, reverse):
    row = lax.broadcasted_iota(jnp.int32, (n, n), 0)
    col = lax.broadcasted_iota(jnp.int32, (n, n), 1)
    sh = tb.bit_length() - 1
    same = lax.shift_right_logical(row, sh) == lax.shift_right_logical(col, sh)
    order = (row <= col) if reverse else (row >= col)
    return jnp.where(jnp.logical_and(same, order), 1.0, 0.0)


def _full_spec(shape):
    nd = len(shape)
    return pl.BlockSpec(shape, lambda b, i: (0,) * nd)


def _split(x):
    hi = x.astype(BF16)
    return hi, (x - hi.astype(F32)).astype(BF16)


def _dot3(a, b, nt=False):
    f = _dot_nt if nt else _dot
    ah, al = _split(a)
    bh, bl = _split(b)
    return f(ah, bh) + (f(ah, bl) + f(al, bh))


def _dot_sel(a, sel):
    sel = sel.astype(BF16)
    a1 = a.astype(BF16)
    r1 = a - a1.astype(F32)
    a2 = r1.astype(BF16)
    a3 = (r1 - a2.astype(F32)).astype(BF16)
    return _dot(a1, sel) + (_dot(a2, sel) + _dot(a3, sel))


def _sel_dot(sel, b):
    sel = sel.astype(BF16)
    b1 = b.astype(BF16)
    r1 = b - b1.astype(F32)
    b2 = r1.astype(BF16)
    b3 = (r1 - b2.astype(F32)).astype(BF16)
    return _dot(sel, b1) + (_dot(sel, b2) + _dot(sel, b3))


def _rwkv_kernel(r_ref, k_ref, v_ref, lo_ref, s0_ref, w0_ref, wup_ref, a0_ref, aup_ref, kkw_ref, kaw_ref,
                 rkw_ref, j_ref, o_ref, bon_ref, sout_ref, ZT, *, reverse):
    i = pl.program_id(1)
    TB = _CHUNK
    NP = _RW_HEADS // 2
    BB = r_ref.shape[0]
    N = BB * TB

    @pl.when(i == 0)
    def _():
        ZT[...] = s0_ref[...]

    r = r_ref[...].reshape(N, _RW_WIDTH)
    k = k_ref[...].reshape(N, _RW_WIDTH)
    v = v_ref[...].reshape(N, _RW_WIDTH)
    lo = lo_ref[...].reshape(N, _RW_WIDTH)
    jm = j_ref[...]
    w_log = -_softplus(-(w0_ref[...] + _dot3(jnp.tanh(lo[:, 0:_LANE]), wup_ref[...]))) - 0.5
    a = _sigmoid(a0_ref[...] + _dot3(lo[:, _LANE:2 * _LANE], aup_ref[...]))
    kk0 = k * kkw_ref[...]
    kk = kk0 / jnp.maximum(jnp.sqrt(_dot_sel(kk0 * kk0, jm)), 1e-12)
    km = k * (1.0 + (a - 1.0) * kaw_ref[...])
    bon_ref[...] = (_dot_sel(r * km * rkw_ref[...], jm) * v).reshape(BB, TB, _RW_WIDTH)

    lw = -jnp.exp(w_log)
    g = _sel_dot(_chunk_tri(N, TB, reverse), lw)
    e0 = 0 if reverse else TB - 1
    g_end = jnp.concatenate([jnp.broadcast_to(g[bb * TB + e0:bb * TB + e0 + 1], (TB, _RW_WIDTH))
                             for bb in range(BB)], axis=0)
    pm = kk * jnp.exp(g - lw)
    qm = (kk * a) * jnp.exp(-g)
    khm = km * jnp.exp(-g)
    rhm = r * jnp.exp(g)
    qgm = (kk * a) * jnp.exp(g_end - g)
    kgm = km * jnp.exp(g_end - g)
    gam = jnp.exp(g_end)

    t_i = lax.broadcasted_iota(jnp.int32, (TB, _LANE), 0)
    lane = lax.broadcasted_iota(jnp.int32, (TB, _LANE), 1)
    s_i = jnp.bitwise_and(lane, _RW_HD - 1)
    m_a = lane < _RW_HD
    strict = (s_i > t_i) if reverse else (s_i < t_i)
    incl = (s_i >= t_i) if reverse else (s_i <= t_i)
    eye = jnp.where(s_i == t_i, 1.0, 0.0)
    same = ((lax.broadcasted_iota(jnp.int32, (_LANE, _LANE), 0) < _RW_HD)
            == (lax.broadcasted_iota(jnp.int32, (_LANE, _LANE), 1) < _RW_HD))

    def bd(x):
        return jnp.concatenate([jnp.where(m_a, x, 0.0), jnp.where(m_a, 0.0, x)], axis=0)

    units = [(bb, p) for bb in range(BB) for p in range(NP)]
    nu = range(len(units))

    def cut(x, un):
        bb, p = un
        return x[bb * TB:(bb + 1) * TB, p * _LANE:(p + 1) * _LANE]

    lo_h, hi_h = slice(0, _LANE), slice(_LANE, 2 * _LANE)
    gram = [_dot3(jnp.concatenate([cut(pm, un), cut(rhm, un)], axis=0),
                  jnp.concatenate([bd(cut(qm, un)), bd(cut(khm, un))], axis=0), nt=True) for un in units]
    l_pq = [jnp.where(strict, x[0:TB, lo_h], 0.0) for x in gram]
    l_pk = [jnp.where(strict, x[0:TB, hi_h], 0.0) for x in gram]
    m_rq = [jnp.where(incl, x[TB:2 * TB, lo_h], 0.0) for x in gram]
    m_rk = [jnp.where(incl, x[TB:2 * TB, hi_h], 0.0) for x in gram]
    wm = [_dot(jnp.concatenate([l_pk[n], m_rk[n]], axis=0).astype(BF16), bd(cut(v, units[n])).astype(BF16))
          for n in nu]
    kv = [_dot_tn(cut(v, un).astype(BF16), cut(kgm, un).astype(BF16)) for un in units]
    s_m = [eye - x for x in l_pq]
    m_m = [_dot3(x, bd(x)) for x in l_pq]
    nlev = TB.bit_length() - 1
    for lev in range(1, nlev):
        if lev < nlev - 1:
            xs = [_dot3(m_m[n], jnp.concatenate([bd(s_m[n]), bd(m_m[n])], axis=1)) for n in nu]
            s_m = [s_m[n] + xs[n][:, lo_h] for n in nu]
            m_m = [x[:, hi_h] for x in xs]
        else:
            s_m = [s_m[n] + _dot3(m_m[n], bd(s_m[n])) for n in nu]
    ta = [_dot3(s_m[n], jnp.concatenate([bd(cut(pm, units[n])), bd(wm[n][0:TB])], axis=1)) for n in nu]
    zt = [ZT[bb, p] for bb, p in units]
    az = [_dot_nt(jnp.concatenate([ta[n][:, lo_h], cut(rhm, units[n])], axis=0).astype(BF16), zt[n].astype(BF16))
          for n in nu]
    u = [az[n][0:TB] + ta[n][:, hi_h] for n in nu]
    mu = [_dot(m_rq[n].astype(BF16), bd(u[n]).astype(BF16)) for n in nu]
    qu = [_dot_tn(u[n].astype(BF16), cut(qgm, units[n]).astype(BF16)) for n in nu]
    for n in nu:
        bb, p = units[n]
        o_ref[bb, :, p * _LANE:(p + 1) * _LANE] = az[n][TB:2 * TB] - mu[n] + wm[n][TB:2 * TB]
        ZT[bb, p] = zt[n] * cut(gam, units[n])[0:1] + jnp.where(same, kv[n] - qu[n], 0.0)

    @pl.when(i == pl.num_programs(1) - 1)
    def _():
        sout_ref[...] = ZT[...]


def _rwkv_call(rkv, lora, s0, wts, col_major, reverse, rows):
    B, T, _ = rkv.shape
    C = _RW_WIDTH
    nblk = T // _CHUNK
    if col_major:
        assert rows == _CHUNK
    rv = _seq_view(rkv, col_major, rows)
    lv = _seq_view(lora, col_major, rows)
    NP = _RW_HEADS // 2
    bb = _scan_bb(B)
    seq = functools.partial(_seq_spec, col_major=col_major, nblk=nblk, reverse=reverse, bb=bb)
    out_shape = rv.shape[:2] + (rv.shape[2] // 3,)
    st_spec = pl.BlockSpec((bb, NP, _LANE, _LANE), lambda b, i: (b, 0, 0, 0))
    o, bon, s_out = pl.pallas_call(
        functools.partial(_rwkv_kernel, reverse=reverse),
        grid=(B // bb, nblk),
        in_specs=[seq(C, 3 * C, 0), seq(C, 3 * C, 1), seq(C, 3 * C, 2), seq(C, C, 0), st_spec,
                  _full_spec((1, C)), _full_spec((2 * _RW_LORA, C)), _full_spec((1, C)),
                  _full_spec((2 * _RW_LORA, C)), _full_spec((1, C)), _full_spec((1, C)), _full_spec((1, C)),
                  _full_spec((C, C))],
        out_specs=[seq(C, C, 0), seq(C, C, 0), st_spec],
        out_shape=[jax.ShapeDtypeStruct(out_shape, F32), jax.ShapeDtypeStruct(out_shape, F32),
                   jax.ShapeDtypeStruct((B, NP, _LANE, _LANE), F32)],
        scratch_shapes=[pltpu.VMEM((bb, NP, _LANE, _LANE), F32)],
        compiler_params=_cp(("arbitrary", "arbitrary"), _VMEM_LIMIT),
        name="rwkv7_scan",
    )(rv, rv, rv, lv, s0, *wts)
    return o.reshape(B, T, C), bon.reshape(B, T, C), s_out


def _gla_kernel(q_ref, k_ref, v_ref, lo_ref, s0_ref, aup_ref, ab_ref, o_ref, sout_ref, ST, *, reverse):
    i = pl.program_id(1)
    TB = _CHUNK

    @pl.when(i == 0)
    def _():
        ST[...] = s0_ref[...]

    BB = q_ref.shape[0]
    N = BB * TB
    q = _silu(q_ref[...].reshape(N, _GLA_PW)) * (_GLA_DK ** -0.5)
    k = _silu(k_ref[...].reshape(N, _GLA_PW))
    v = _silu(v_ref[...].reshape(N, _GLA_PW))
    gad = lo_ref[...].reshape(N, _RW_WIDTH)[:, 2 * _LANE:3 * _LANE]
    log_a = -_softplus(-(_dot3(gad, aup_ref[...]) + ab_ref[...])) / _GLA_TAU
    row = lax.broadcasted_iota(jnp.int32, (TB, TB), 0)
    col = lax.broadcasted_iota(jnp.int32, (TB, TB), 1)
    keep = (row <= col) if reverse else (row >= col)
    g_cum = _sel_dot(_chunk_tri(N, TB, reverse), log_a)
    e0 = 0 if reverse else TB - 1
    g_last = jnp.concatenate([jnp.broadcast_to(g_cum[bb * TB + e0:bb * TB + e0 + 1], (TB, _GLA_PW))
                              for bb in range(BB)], axis=0)
    q_in = q * jnp.exp(g_cum)
    k_in = k * jnp.exp(-g_cum)
    k_tail = k * jnp.exp(g_last - g_cum)
    dec = jnp.exp(g_last)
    units = [(bb, h) for bb in range(BB) for h in range(_GLA_HEADS)]
    nu = range(len(units))

    def cut(x, un):
        bb, h = un
        return x[bb * TB:(bb + 1) * TB, h * _LANE:(h + 1) * _LANE]

    qh = [cut(q_in, un).astype(BF16) for un in units]
    vh = [cut(v, un).astype(BF16) for un in units]
    st = [ST[bb, h] for bb, h in units]
    att = [_dot_nt(qh[n], cut(k_in, units[n]).astype(BF16)) for n in nu]
    o_inter = [_dot_nt(qh[n], st[n].astype(BF16)) for n in nu]
    kv = [_dot_tn(vh[n], cut(k_tail, units[n]).astype(BF16)) for n in nu]
    o_intra = [_dot(jnp.where(keep, att[n], 0.0).astype(BF16), vh[n]) for n in nu]
    for n in nu:
        bb, h = units[n]
        o_ref[bb, :, h * _LANE:(h + 1) * _LANE] = o_intra[n] + o_inter[n]
        ST[bb, h] = st[n] * cut(dec, units[n])[0:1] + kv[n]

    @pl.when(i == pl.num_programs(1) - 1)
    def _():
        sout_ref[...] = ST[...]


def _gla_call(qkv, lora, s0, aup, ab, col_major, reverse, rows):
    B, T, _ = qkv.shape
    C = _GLA_PW
    nblk = T // _CHUNK
    if col_major:
        assert rows == _CHUNK
    qv = _seq_view(qkv, col_major, rows)
    lv = _seq_view(lora, col_major, rows)
    bb = _scan_bb(B)
    seq = functools.partial(_seq_spec, col_major=col_major, nblk=nblk, reverse=reverse, bb=bb)
    out_shape = qv.shape[:2] + (qv.shape[2] // 3,)
    st_spec = pl.BlockSpec((bb, _GLA_HEADS, _LANE, _LANE), lambda b, i: (b, 0, 0, 0))
    o, s_out = pl.pallas_call(
        functools.partial(_gla_kernel, reverse=reverse),
        grid=(B // bb, nblk),
        in_specs=[seq(C, 3 * C, 0), seq(C, 3 * C, 1), seq(C, 3 * C, 2), seq(_RW_WIDTH, _RW_WIDTH, 0), st_spec,
                  _full_spec((_LANE, C)), _full_spec((1, C))],
        out_specs=[seq(C, C, 0), st_spec],
        out_shape=[jax.ShapeDtypeStruct(out_shape, F32),
                   jax.ShapeDtypeStruct((B, _GLA_HEADS, _LANE, _LANE), F32)],
        scratch_shapes=[pltpu.VMEM((bb, _GLA_HEADS, _LANE, _LANE), F32)],
        compiler_params=_cp(("arbitrary", "arbitrary"), _VMEM_LIMIT),
        name="gla_chunked",
    )(qv, qv, qv, lv, s0, aup, ab)
    return o.reshape(B, T, C), s_out


def _fourier_kernel(x_ref, cg_ref, sg_ref, w_ref, o_ref, xcs_ref):
    T = x_ref.shape[1]

    @pl.when(pl.program_id(1) == 0)
    def _():
        x = x_ref[0]
        xcs_ref[0:T, :] = _dot3(x, cg_ref[...]).astype(BF16)
        xcs_ref[T:2 * T, :] = _dot3(x, sg_ref[...]).astype(BF16)

    o_ref[0] = _dot(w_ref[...], xcs_ref[...])


def _fourier_call(zpost, cg, sg, wt):
    B, T, _ = zpost.shape
    C = _FN_WIDTH
    tm = _tile(T, 512, 8)
    return pl.pallas_call(
        _fourier_kernel,
        grid=(B, T // tm),
        in_specs=[pl.BlockSpec((1, T, C), lambda b, i: (b, 0, 0)),
                  _full_spec((C, C)), _full_spec((C, C)),
                  pl.BlockSpec((tm, 2 * T), lambda b, i: (i, 0))],
        out_specs=pl.BlockSpec((1, tm, C), lambda b, i: (b, i, 0)),
        out_shape=jax.ShapeDtypeStruct((B, T, C), F32),
        scratch_shapes=[pltpu.VMEM((2 * T, C), BF16)],
        compiler_params=_cp(("arbitrary", "arbitrary"), _VMEM_LIMIT),
        name="fnet_dft",
    )(zpost, cg, sg, wt)


def _merge_kernel(rwf_ref, rwb_ref, bnf_ref, bnb_ref, glf_ref, glb_ref, zp_ref, zg_ref, fn_ref, h_ref, g1_ref,
                  jr_ref, lnw_ref, lnb_ref, gup_ref, jg_ref, gnw_ref, pa_ref, pb_ref, pc_ref, wo_ref, o_ref):
    D = h_ref.shape[2]
    zp = zp_ref[0]
    og = zp[:, _FN_WIDTH:_FN_WIDTH + _GLA_PW]
    gd = zp[:, _FN_WIDTH + _GLA_PW:_FN_WIDTH + _GLA_PW + _RW_GATE_RANK]
    o = rwf_ref[0] + rwb_ref[0]
    jr = jr_ref[...]
    mu = _dot_sel(o, jr) * (1.0 / _RW_HD)
    xc = o - mu
    var = _dot_sel(xc * xc, jr) * (1.0 / _RW_HD)
    y = xc * lax.rsqrt(var + _RW_GN_EPS)
    y = y * lnw_ref[...] + lnb_ref[...] + (bnf_ref[0] + bnb_ref[0])
    rw_y = y * _dot3(_sigmoid(gd), gup_ref[...])
    g = glf_ref[0] + glb_ref[0]
    ms = _dot_sel(g * g, jg_ref[...]) * (1.0 / _GLA_DV)
    gla_y = g * lax.rsqrt(ms + _NORM_EPS) * gnw_ref[...] * _silu(og)
    gates = _sigmoid(zg_ref[0])
    m = (gates[:, 0:D] * _dot(rw_y.astype(BF16), pa_ref[...])
         + gates[:, D:2 * D] * _dot(gla_y.astype(BF16), pb_ref[...])
         + gates[:, 2 * D:3 * D] * _dot(fn_ref[0].astype(BF16), pc_ref[...]))
    o_ref[0] = h_ref[0] + g1_ref[0] * _dot(m.astype(BF16), wo_ref[...])


def _merge_call(rwf, rwb, bnf, bnb, glf, glb, zpost, zgates, fn, h, g1, wts):
    B, T, D = h.shape
    tm = _tile(T, 256, 8)

    def row(c):
        return pl.BlockSpec((1, tm, c), lambda b, i: (b, i, 0))

    acts = [rwf, rwb, bnf, bnb, glf, glb, zpost, zgates, fn, h]
    return pl.pallas_call(
        _merge_kernel,
        grid=(B, T // tm),
        in_specs=[row(a.shape[2]) for a in acts] + [pl.BlockSpec((1, 1, D), lambda b, i: (b, 0, 0))]
        + [_full_spec(w.shape) for w in wts],
        out_specs=row(D),
        out_shape=jax.ShapeDtypeStruct((B, T, D), F32),
        compiler_params=_cp(("arbitrary", "arbitrary"), _VMEM_LIMIT),
        name="branch_merge",
    )(*acts, g1, *wts)


def _router_kernel(h_ref, nw_ref, sc_ref, sh_ref, rt_ref, v_ref, aff_ref):
    u = _modnorm(h_ref[0], nw_ref[...], sc_ref[0], sh_ref[0])
    v_ref[0] = u.astype(BF16)
    logits = _dot_nt(rt_ref[...], u, HI)
    e = jnp.exp(logits - jnp.max(logits, axis=0, keepdims=True))
    aff_ref[0] = e / jnp.sum(e, axis=0, keepdims=True)


def _router_call(h, nw, sc, sh, router_t):
    B, T, D = h.shape
    E = router_t.shape[0]
    tm = _tile(T, 512, 128)
    return pl.pallas_call(
        _router_kernel,
        grid=(B, T // tm),
        in_specs=[pl.BlockSpec((1, tm, D), lambda b, i: (b, i, 0)), _full_spec((1, D)),
                  pl.BlockSpec((1, 1, D), lambda b, i: (b, 0, 0)), pl.BlockSpec((1, 1, D), lambda b, i: (b, 0, 0)),
                  _full_spec((E, D))],
        out_specs=[pl.BlockSpec((1, tm, D), lambda b, i: (b, i, 0)), pl.BlockSpec((1, E, tm), lambda b, i: (b, 0, i))],
        out_shape=[jax.ShapeDtypeStruct((B, T, D), BF16), jax.ShapeDtypeStruct((B, E, T), F32)],
        compiler_params=_cp(("arbitrary", "arbitrary"), _VMEM_LIMIT),
        name="router_softmax",
    )(h, nw, sc, sh, router_t)


def _topc_kernel(aff_ref, pos_ref, *, cap):
    E, T = aff_ref.shape[1], aff_ref.shape[2]
    x = pltpu.bitcast(aff_ref[0], jnp.int32)

    def body(it, thr):
        cand = thr | lax.shift_left(jnp.int32(1), 30 - it)
        cnt = jnp.sum(jnp.where(x >= cand, 1, 0), axis=1, keepdims=True)
        return jnp.where(cnt >= cap, cand, thr)

    thr = lax.fori_loop(0, 31, body, jnp.zeros((E, 1), jnp.int32))
    need = (cap - jnp.sum(jnp.where(x > thr, 1, 0), axis=1, keepdims=True)).astype(F32)
    blk = _LANE
    upper = jnp.where(lax.broadcasted_iota(jnp.int32, (blk, blk), 0) < lax.broadcasted_iota(jnp.int32, (blk, blk), 1),
                      1.0, 0.0).astype(BF16)
    off_eq = jnp.zeros((E, 1), F32)
    off_sel = jnp.zeros((E, 1), F32)
    for c in range(T // blk):
        sl = slice(c * blk, (c + 1) * blk)
        xc = pltpu.bitcast(aff_ref[0, :, sl], jnp.int32)
        eq_c = jnp.where(xc == thr, 1.0, 0.0)
        rank_eq = _dot(eq_c.astype(BF16), upper) + off_eq
        take = jnp.where(rank_eq < need, eq_c, 0.0)
        sel = jnp.where(xc > thr, 1.0, take)
        rank = _dot(sel.astype(BF16), upper) + off_sel
        pos_ref[0, :, sl] = jnp.where(sel > 0.0, rank.astype(jnp.int32), -1)
        off_eq = off_eq + jnp.sum(eq_c, axis=1, keepdims=True)
        off_sel = off_sel + jnp.sum(sel, axis=1, keepdims=True)


def _topc_call(aff, cap):
    B, E, T = aff.shape
    return pl.pallas_call(
        functools.partial(_topc_kernel, cap=cap),
        grid=(B,),
        in_specs=[pl.BlockSpec((1, E, T), lambda b: (b, 0, 0))],
        out_specs=pl.BlockSpec((1, E, T), lambda b: (b, 0, 0)),
        out_shape=jax.ShapeDtypeStruct((B, E, T), jnp.int32),
        compiler_params=_cp(("arbitrary",), _VMEM_LIMIT),
        name="expert_choice_topc",
    )(aff)


def _ffn_kernel(v_ref, pos_ref, aff_ref, w1_ref, w3_ref, w2_ref, ys_ref, xs_ref, acc_ref, gate_ref, *, cap, tc,
                expert_axis):
    e = pl.program_id(expert_axis)
    f = pl.program_id(2)
    T = v_ref.shape[1]

    @pl.when(f == 0)
    def _():
        slot = lax.broadcasted_iota(jnp.int32, (cap, tc), 0)
        xs = jnp.zeros(xs_ref.shape, F32)
        gate = jnp.zeros((cap, 1), F32)
        for c in range(T // tc):
            sl = slice(c * tc, (c + 1) * tc)
            hit = slot == pos_ref[0, e, :, sl]
            xs = xs + _dot(jnp.where(hit, 1.0, 0.0).astype(BF16), v_ref[0, sl, :])
            gate = gate + jnp.sum(jnp.where(hit, aff_ref[0, e, :, sl], 0.0), axis=1, keepdims=True)
        xs_ref[...] = xs.astype(BF16)
        gate_ref[...] = gate
        acc_ref[...] = jnp.zeros(acc_ref.shape, F32)

    x = xs_ref[...]
    h1 = _dot(x, w1_ref[0])
    hid = _silu(h1) * _dot(x, w3_ref[0])
    acc_ref[...] += _dot(hid.astype(BF16), w2_ref[0])

    @pl.when(f == pl.num_programs(2) - 1)
    def _():
        ys_ref[0, 0] = (acc_ref[...] * gate_ref[...]).astype(BF16)


def _ffn_call(v, pos, aff, w1, w3, w2, cap):
    B, T, D = v.shape
    E, _, F = w1.shape
    tc = _tile(T, 1024)
    expert_major = T <= 512
    fc = F if expert_major else _tile(F, 512)
    if expert_major:
        grid = (E, B, 1)

        def ix(f):
            return lambda e, b, j: f(b, e, j)
    else:
        grid = (B, E, F // fc)

        def ix(f):
            return f
    return pl.pallas_call(
        functools.partial(_ffn_kernel, cap=cap, tc=tc, expert_axis=0 if expert_major else 1),
        grid=grid,
        in_specs=[pl.BlockSpec((1, T, D), ix(lambda b, e, f: (b, 0, 0))),
                  pl.BlockSpec((1, E, 1, T), ix(lambda b, e, f: (b, 0, 0, 0))),
                  pl.BlockSpec((1, E, 1, T), ix(lambda b, e, f: (b, 0, 0, 0))),
                  pl.BlockSpec((1, D, fc), ix(lambda b, e, f: (e, 0, f))),
                  pl.BlockSpec((1, D, fc), ix(lambda b, e, f: (e, 0, f))),
                  pl.BlockSpec((1, fc, D), ix(lambda b, e, f: (e, f, 0)))],
        out_specs=pl.BlockSpec((1, 1, cap, D), ix(lambda b, e, f: (b, e, 0, 0))),
        out_shape=jax.ShapeDtypeStruct((B, E, cap, D), BF16),
        scratch_shapes=[pltpu.VMEM((cap, D), BF16), pltpu.VMEM((cap, D), F32), pltpu.VMEM((cap, 1), F32)],
        compiler_params=_cp(("arbitrary", "arbitrary", "arbitrary"), _VMEM_LIMIT),
        name="expert_ffn",
    )(v, pos, aff, w1, w3, w2)


def _scatter_kernel(pos_ref, ys_ref, h_ref, g2_ref, o_ref, acc_ref, *, cap):
    e = pl.program_id(2)
    tm = h_ref.shape[1]

    @pl.when(e == 0)
    def _():
        acc_ref[...] = jnp.zeros(acc_ref.shape, F32)

    hit = lax.broadcasted_iota(jnp.int32, (cap, tm), 0) == pos_ref[0, e]
    acc_ref[...] += _dot_tn(jnp.where(hit, 1.0, 0.0).astype(BF16), ys_ref[0, 0])

    @pl.when(e == pl.num_programs(2) - 1)
    def _():
        o_ref[0] = h_ref[0] + g2_ref[0] * acc_ref[...]


def _scatter_call(pos, ys, h, g2, cap):
    B, T, D = h.shape
    E = pos.shape[1]
    tm = _tile(T, 1024)
    return pl.pallas_call(
        functools.partial(_scatter_kernel, cap=cap),
        grid=(B, T // tm, E),
        in_specs=[pl.BlockSpec((1, E, 1, tm), lambda b, i, e: (b, 0, 0, i)),
                  pl.BlockSpec((1, 1, cap, D), lambda b, i, e: (b, e, 0, 0)),
                  pl.BlockSpec((1, tm, D), lambda b, i, e: (b, i, 0)),
                  pl.BlockSpec((1, 1, D), lambda b, i, e: (b, 0, 0))],
        out_specs=pl.BlockSpec((1, tm, D), lambda b, i, e: (b, i, 0)),
        out_shape=jax.ShapeDtypeStruct((B, T, D), F32),
        scratch_shapes=[pltpu.VMEM((tm, D), F32)],
        compiler_params=_cp(("arbitrary", "arbitrary", "arbitrary"), _VMEM_LIMIT),
        name="expert_scatter",
    )(pos, ys, h, g2)


def _final_kernel(h_ref, w_ref, o_ref):
    x = h_ref[0]
    o_ref[0] = x * lax.rsqrt(jnp.mean(x * x, axis=-1, keepdims=True) + _NORM_EPS) * w_ref[...]


def _final_call(h, w):
    B, T, D = h.shape
    tm = _tile(T, 1024, 8)
    return pl.pallas_call(
        _final_kernel,
        grid=(B, T // tm),
        in_specs=[pl.BlockSpec((1, tm, D), lambda b, i: (b, i, 0)), _full_spec((1, D))],
        out_specs=pl.BlockSpec((1, tm, D), lambda b, i: (b, i, 0)),
        out_shape=jax.ShapeDtypeStruct((B, T, D), F32),
        compiler_params=_cp(("arbitrary", "arbitrary"), _VMEM_LIMIT),
        name="final_norm",
    )(h, w)


def _pad_heads(x, nh, d, axis=-1):
    axis = axis % x.ndim
    shp = x.shape
    x = x.reshape(shp[:axis] + (nh, d) + shp[axis + 1:])
    pad = [(0, 0)] * x.ndim
    pad[axis + 1] = (0, _LANE - d)
    x = jnp.pad(x, pad)
    return x.reshape(shp[:axis] + (nh * _LANE,) + shp[axis + 1:])


def _pad_to(x, n, axis=-1):
    axis = axis % x.ndim
    pad = [(0, 0)] * x.ndim
    pad[axis] = (0, n - x.shape[axis])
    return jnp.pad(x, pad)


def _block_ones(n, blk):
    i = np.arange(n) // blk
    return jnp.asarray((i[:, None] == i[None, :]).astype(np.float32))


def _dft_consts(T):
    s = 1 << ((T.bit_length() - 1) // 2)
    i = np.arange(T)[:, None]
    a1 = 2.0 * np.pi * ((i * s * np.arange(T // s)[None, :]) % T) / T
    a2 = 2.0 * np.pi * ((i * np.arange(s)[None, :]) % T) / T
    c1, s1, c2, s2 = (jnp.asarray(f(a) / math.sqrt(math.sqrt(T)), F32)
                      for a, f in ((a1, np.cos), (a1, np.sin), (a2, np.cos), (a2, np.sin)))
    cos = (c1[:, :, None] * c2[:, None, :] - s1[:, :, None] * s2[:, None, :]).reshape(T, T)
    sin = (s1[:, :, None] * c2[:, None, :] + c1[:, :, None] * s2[:, None, :]).reshape(T, T)
    wt = jnp.concatenate([cos, -sin], axis=1)
    g = np.arange(_FN_GD)
    ang_g = 2.0 * np.pi * ((g[:, None] * g[None, :]) % _FN_GD) / _FN_GD
    eye = np.eye(_FN_GROUPS)
    cg = np.kron(eye, np.cos(ang_g)) / math.sqrt(_FN_GD)
    sg = np.kron(eye, np.sin(ang_g)) / math.sqrt(_FN_GD)
    return jnp.asarray(wt, BF16), jnp.asarray(cg, F32), jnp.asarray(sg, F32)


def _layer_weights(i, w_in, conv_w, rw_w0, rw_w_up, rw_a0, rw_a_up, rw_k_k, rw_k_a, rw_r_k, rw_g_up, rw_ln_w,
                   rw_ln_b, gla_a_up, gla_a_b, gla_norm_w, proj_a, proj_b, proj_c, w_out):
    D = w_in.shape[1]
    rw, kw, vw = _RW_WIDTH, _GLA_HEADS * _GLA_DK, _GLA_HEADS * _GLA_DV
    o_gq = 3 * rw
    o_gk = o_gq + kw
    o_gv = o_gk + kw
    o_wd = o_gv + vw
    o_ad = o_wd + 2 * _RW_LORA
    o_ga = o_ad + 2 * _RW_LORA
    o_gd = o_ga + 2 * _GLA_RANK
    o_og = o_gd + _RW_GATE_RANK
    o_fn = o_og + vw
    o_gt = o_fn + _FN_WIDTH
    wi = w_in[i]
    cw = conv_w[i].reshape(9, -1)

    def gla_cols(x):
        return jnp.concatenate([_pad_heads(x[..., o_gq:o_gk], _GLA_HEADS, _GLA_DK),
                                _pad_heads(x[..., o_gk:o_gv], _GLA_HEADS, _GLA_DK),
                                _pad_heads(x[..., o_gv:o_wd], _GLA_HEADS, _GLA_DV)], axis=-1)

    w = {}
    w['in_rw'] = wi[:, 0:o_gq].astype(BF16)
    w['in_gla'] = gla_cols(wi).astype(BF16)
    w['in_lora'] = jnp.concatenate([wi[:, o_wd:o_ga], _pad_to(wi[:, o_ga:o_gd], _LANE)], axis=-1).astype(BF16)
    w['in_post'] = jnp.concatenate([wi[:, o_fn:o_gt], _pad_heads(wi[:, o_og:o_fn], _GLA_HEADS, _GLA_DV),
                                    wi[:, o_gd:o_og]], axis=-1).astype(BF16)
    w['in_gates'] = wi[:, o_gt:].astype(BF16)
    w['conv_rw'] = cw[:, 0:o_gq]
    w['conv_gla'] = gla_cols(cw)
    zl = jnp.zeros((_RW_LORA, rw), F32)
    w['rwkv'] = []
    w['gla'] = []
    jr = _block_ones(rw, _RW_HD)
    for d in range(2):
        wup = jnp.concatenate([rw_w_up[i, d], zl] if d == 0 else [zl, rw_w_up[i, d]], axis=0)
        aup = jnp.concatenate([rw_a_up[i, d], zl] if d == 0 else [zl, rw_a_up[i, d]], axis=0)
        w['rwkv'].append((rw_w0[i, d][None], wup, rw_a0[i, d][None], aup, rw_k_k[i][None], rw_k_a[i][None],
                          rw_r_k[i].reshape(1, rw), jr))
        ga = _pad_heads(gla_a_up[i, d], _GLA_HEADS, _GLA_DK)
        ga = jnp.pad(ga, ((d * _GLA_RANK, _LANE - (d + 1) * _GLA_RANK), (0, 0)))
        w['gla'].append((ga, _pad_heads(gla_a_b[i, d][None], _GLA_HEADS, _GLA_DK)))
    w['merge'] = (jr, rw_ln_w[i][None], rw_ln_b[i][None], rw_g_up[i], _block_ones(_GLA_PW, _LANE),
                  _pad_heads(jnp.tile(gla_norm_w[i], _GLA_HEADS)[None], _GLA_HEADS, _GLA_DV),
                  proj_a[i].astype(BF16), _pad_heads(proj_b[i], _GLA_HEADS, _GLA_DV, axis=0).astype(BF16),
                  proj_c[i].astype(BF16), w_out[i].astype(BF16))
    return w


def _moe(h, nw, sc, sh, g2, router_t, w1, w3, w2):
    B, T, D = h.shape
    E = router_t.shape[0]
    cap = 2 * T // E
    v, aff = _router_call(h, nw, sc, sh, router_t)
    pos = _topc_call(aff, cap).reshape(B, E, 1, T)
    ys = _ffn_call(v, pos, aff.reshape(B, E, 1, T), w1, w3, w2, cap)
    return _scatter_call(pos, ys, h, g2, cap)


def kernel(x, c, ctx, c_ctx, w_mod, b_mod, norm1_w, norm2_w, w_in, conv_w, rw_w0, rw_w_up, rw_a0, rw_a_up, rw_k_k, rw_k_a, rw_r_k, rw_g_up, rw_ln_w, rw_ln_b, gla_a_up, gla_a_b, gla_norm_w, proj_a, proj_b, proj_c, w_out, router, exp_w1, exp_w3, exp_w2, final_norm_w):
    B, S, D = x.shape
    TC = ctx.shape[1]
    L = w_mod.shape[0]
    W = _GRID_W
    rows = S // W
    assert S % _CHUNK == 0 and TC % _CHUNK == 0

    cc = _pad_to(jnp.concatenate([c, c_ctx[None]], axis=0), 16, axis=0)
    mods = _mod_call(cc, w_mod, b_mod)
    dft = {S: _dft_consts(S), TC: _dft_consts(TC)}
    s0_rw = jnp.zeros((B, _RW_HEADS // 2, _LANE, _LANE), F32)
    s0_gla = jnp.zeros((B, _GLA_HEADS, _LANE, _LANE), F32)

    h_lat, h_ctx = x, ctx
    for i in range(L):
        col_major = i % 2 == 1
        need_ctx = i < L - 1
        lw = _layer_weights(i, w_in, conv_w, rw_w0, rw_w_up, rw_a0, rw_a_up, rw_k_k, rw_k_a, rw_r_k, rw_g_up,
                            rw_ln_w, rw_ln_b, gla_a_up, gla_a_b, gla_norm_w, proj_a, proj_b, proj_c, w_out)
        m_lat = mods[i, :B].reshape(B, 1, _N_MOD, D)
        m_ctx = jnp.broadcast_to(mods[i, B:B + 1].reshape(1, 1, _N_MOD, D), (B, 1, _N_MOD, D))
        n1 = norm1_w[i][None]
        n2 = norm2_w[i][None]

        def in_proj(h, m, img_rows, img_w):
            z = {k: _norm_matmul(h, n1, m[:, :, 1], m[:, :, 0], lw['in_' + k])
                 for k in ('rw', 'gla', 'lora', 'post', 'gates')}
            z['rw'] = _conv_call(z['rw'], lw['conv_rw'], img_rows, img_w)
            z['gla'] = _conv_call(z['gla'], lw['conv_gla'], img_rows, img_w)
            return z

        z_lat = in_proj(h_lat, m_lat, rows, W)
        z_ctx = in_proj(h_ctx, m_ctx, 1, TC)

        rw_lat, rw_ctx, gl_lat, gl_ctx = [], [], [], []
        for d in range(2):
            rev = d == 1
            o_c, b_c, st = _rwkv_call(z_ctx['rw'], z_ctx['lora'], s0_rw, lw['rwkv'][d], False, rev, 1)
            o_l, b_l, _ = _rwkv_call(z_lat['rw'], z_lat['lora'], st, lw['rwkv'][d], col_major, rev, rows)
            rw_ctx.append((o_c, b_c))
            rw_lat.append((o_l, b_l))
            g_c, st = _gla_call(z_ctx['gla'], z_ctx['lora'], s0_gla, *lw['gla'][d], False, rev, 1)
            g_l, _ = _gla_call(z_lat['gla'], z_lat['lora'], st, *lw['gla'][d], col_major, rev, rows)
            gl_ctx.append(g_c)
            gl_lat.append(g_l)

        def mix(h, m, z, rwo, glo, T):
            wt, cg, sg = dft[T]
            fn = _fourier_call(z['post'], cg, sg, wt)
            return _merge_call(rwo[0][0], rwo[1][0], rwo[0][1], rwo[1][1], glo[0], glo[1], z['post'], z['gates'],
                               fn, h, m[:, :, 2], lw['merge'])

        router_t = router[i].T
        w1 = exp_w1[i].astype(BF16)
        w3 = exp_w3[i].astype(BF16)
        w2 = exp_w2[i].astype(BF16)
        h_lat = mix(h_lat, m_lat, z_lat, rw_lat, gl_lat, S)
        h_lat = _moe(h_lat, n2, m_lat[:, :, 4], m_lat[:, :, 3], m_lat[:, :, 5], router_t, w1, w3, w2)
        if need_ctx:
            h_ctx = mix(h_ctx, m_ctx, z_ctx, rw_ctx, gl_ctx, TC)
            h_ctx = _moe(h_ctx, n2, m_ctx[:, :, 4], m_ctx[:, :, 3], m_ctx[:, :, 5], router_t, w1, w3, w2)
    return _final_call(h_lat, final_norm_w[None])
```

```python
import functools
import math

import jax
import jax.numpy as jnp
import numpy as np
from jax import lax
from jax.experimental import pallas as pl
from jax.experimental.pallas import tpu as pltpu

F32 = jnp.float32
BF16 = jnp.bfloat16
HI = lax.Precision.HIGHEST

_GRID_W = 64
_NORM_EPS = 1e-6
_N_MOD = 6
_RW_HEADS = 6
_RW_HD = 64
_RW_WIDTH = _RW_HEADS * _RW_HD
_RW_LORA = 64
_RW_GATE_RANK = 128
_RW_GN_EPS = 64e-5
_GLA_HEADS = 4
_GLA_DK = 48
_GLA_DV = 96
_GLA_RANK = 16
_GLA_TAU = 16.0
_CHUNK = 64
_FN_GROUPS = 4
_FN_GD = 64
_FN_WIDTH = _FN_GROUPS * _FN_GD
_LANE = 128
_GLA_PW = _GLA_HEADS * _LANE
_VMEM_LIMIT = 56 * 1024 * 1024


def _cp(sem, vmem=None):
    return pltpu.CompilerParams(dimension_semantics=sem, vmem_limit_bytes=vmem)


def _tile(n, cap, mult=128):
    if n <= cap:
        return n
    best = None
    for t in range(mult, cap + 1, mult):
        if n % t == 0:
            best = t
    assert best is not None, (n, cap)
    return best


def _sigmoid(x):
    return 1.0 / (1.0 + jnp.exp(-x))


def _silu(x):
    return x * _sigmoid(x)


def _softplus(x):
    return jnp.maximum(x, 0.0) + jnp.log(1.0 + jnp.exp(-jnp.abs(x)))


def _dot(a, b, precision=None):
    return jnp.dot(a, b, preferred_element_type=F32, precision=precision)


def _dot_nt(a, b, precision=None):
    return lax.dot_general(a, b, (((1,), (1,)), ((), ())), preferred_element_type=F32, precision=precision)


def _dot_tn(a, b, precision=None):
    return lax.dot_general(a, b, (((0,), (0,)), ((), ())), preferred_element_type=F32, precision=precision)


def _mod_kernel(c_ref, w_ref, b_ref, o_ref):
    o_ref[0] = _dot(_silu(c_ref[...]), w_ref[0], HI) + b_ref[0]


def _mod_call(cc, w_mod, b_mod):
    L, D, N = w_mod.shape
    R = cc.shape[0]
    tn = _tile(N, 1536)
    return pl.pallas_call(
        _mod_kernel,
        grid=(L, N // tn),
        in_specs=[pl.BlockSpec((R, D), lambda l, j: (0, 0)),
                  pl.BlockSpec((1, D, tn), lambda l, j: (l, 0, j)),
                  pl.BlockSpec((1, 1, tn), lambda l, j: (l, 0, j))],
        out_specs=pl.BlockSpec((1, R, tn), lambda l, j: (l, 0, j)),
        out_shape=jax.ShapeDtypeStruct((L, R, N), F32),
        compiler_params=_cp(("arbitrary", "arbitrary"), _VMEM_LIMIT),
        name="adaln_mod",
    )(cc, w_mod, b_mod.reshape(L, 1, N))


def _modnorm(x, nw, sc, sh):
    y = x * lax.rsqrt(jnp.mean(x * x, axis=-1, keepdims=True) + _NORM_EPS)
    return (y * nw) * (1.0 + sc) + sh


def _nm_kernel(h_ref, nw_ref, sc_ref, sh_ref, w_ref, o_ref, u_ref):
    @pl.when(pl.program_id(2) == 0)
    def _():
        u_ref[...] = _modnorm(h_ref[0], nw_ref[...], sc_ref[0], sh_ref[0]).astype(BF16)

    o_ref[0] = _dot(u_ref[...], w_ref[...])


def _norm_matmul(h, nw, sc, sh, w):
    B, T, D = h.shape
    N = w.shape[1]
    tm = _tile(T, 1024, 8)
    tn = _tile(N, 1536)
    return pl.pallas_call(
        _nm_kernel,
        grid=(B, T // tm, N // tn),
        in_specs=[pl.BlockSpec((1, tm, D), lambda b, i, j: (b, i, 0)),
                  pl.BlockSpec((1, D), lambda b, i, j: (0, 0)),
                  pl.BlockSpec((1, 1, D), lambda b, i, j: (b, 0, 0)),
                  pl.BlockSpec((1, 1, D), lambda b, i, j: (b, 0, 0)),
                  pl.BlockSpec((D, tn), lambda b, i, j: (0, j))],
        out_specs=pl.BlockSpec((1, tm, tn), lambda b, i, j: (b, i, j)),
        out_shape=jax.ShapeDtypeStruct((B, T, N), F32),
        scratch_shapes=[pltpu.VMEM((tm, D), BF16)],
        compiler_params=_cp(("arbitrary", "arbitrary", "arbitrary"), _VMEM_LIMIT),
        name="norm_in_proj",
    )(h, nw, sc, sh, w)


def _conv_kernel(x_ref, w_ref, o_ref, xp_ref, *, rows, W, T, PAD, CH):
    cw = x_ref.shape[2]
    xp_ref[0:PAD, :] = jnp.zeros((PAD, cw), F32)
    xp_ref[PAD + T:PAD + T + PAD, :] = jnp.zeros((PAD, cw), F32)
    xp_ref[PAD:PAD + T, :] = x_ref[0]
    for c0 in range(0, T, CH):
        col = jnp.bitwise_and(lax.broadcasted_iota(jnp.int32, (CH, cw), 0) + c0, W - 1)
        acc = jnp.zeros((CH, cw), F32)
        for a in range(3):
            if rows == 1 and a != 1:
                continue
            for b in range(3):
                off = (a - 1) * W + (b - 1)
                xs = xp_ref[PAD + c0 + off:PAD + c0 + off + CH, :]
                if b == 0:
                    xs = jnp.where(col >= 1, xs, 0.0)
                elif b == 2:
                    xs = jnp.where(col <= W - 2, xs, 0.0)
                acc = acc + xs * w_ref[a * 3 + b:a * 3 + b + 1, :]
        o_ref[0, c0:c0 + CH, :] = acc


def _conv_call(z, w9, rows, W):
    B, T, C = z.shape
    assert W & (W - 1) == 0 and rows * W == T
    cw = _LANE
    PAD = W + 8 if rows > 1 else 8
    CH = min(T, 128)
    kern = functools.partial(_conv_kernel, rows=rows, W=W, T=T, PAD=PAD, CH=CH)
    return pl.pallas_call(
        kern,
        grid=(B, C // cw),
        in_specs=[pl.BlockSpec((1, T, cw), lambda b, j: (b, 0, j)),
                  pl.BlockSpec((9, cw), lambda b, j: (0, j))],
        out_specs=pl.BlockSpec((1, T, cw), lambda b, j: (b, 0, j)),
        out_shape=jax.ShapeDtypeStruct((B, T, C), F32),
        scratch_shapes=[pltpu.VMEM((T + 2 * PAD, cw), F32)],
        compiler_params=_cp(("arbitrary", "arbitrary"), _VMEM_LIMIT),
        name="short_conv",
    )(z, w9)


def _seq_view(arr, col_major, rows):
    B, T, C = arr.shape
    return arr.reshape(B, rows, (T // rows) * C) if col_major else arr


def _seq_spec(C, ctot, part, col_major, nblk, reverse, bb):
    nper = ctot // C

    def blk(i):
        return nblk - 1 - i if reverse else i

    if col_major:
        return pl.BlockSpec((bb, _CHUNK, C), lambda b, i: (b, 0, blk(i) * nper + part))
    return pl.BlockSpec((bb, _CHUNK, C), lambda b, i: (b, blk(i), part))


def _scan_bb(B):
    return 4 if B % 4 == 0 else (2 if B % 2 == 0 else 1)


def _chunk_tri(n, tb, reverse):
    row = lax.broadcasted_iota(jnp.int32, (n, n), 0)
    col = lax.broadcasted_iota(jnp.int32, (n, n), 1)
    sh = tb.bit_length() - 1
    same = lax.shift_right_logical(row, sh) == lax.shift_right_logical(col, sh)
    order = (row <= col) if reverse else (row >= col)
    return jnp.where(jnp.logical_and(same, order), 1.0, 0.0)


def _full_spec(shape):
    nd = len(shape)
    return pl.BlockSpec(shape, lambda b, i: (0,) * nd)


def _split(x):
    hi = x.astype(BF16)
    return hi, (x - hi.astype(F32)).astype(BF16)


def _dot3(a, b, nt=False):
    f = _dot_nt if nt else _dot
    ah, al = _split(a)
    bh, bl = _split(b)
    return f(ah, bh) + (f(ah, bl) + f(al, bh))


def _dot_sel(a, sel):
    sel = sel.astype(BF16)
    a1 = a.astype(BF16)
    r1 = a - a1.astype(F32)
    a2 = r1.astype(BF16)
    a3 = (r1 - a2.astype(F32)).astype(BF16)
    return _dot(a1, sel) + (_dot(a2, sel) + _dot(a3, sel))


def _sel_dot(sel, b):
    sel = sel.astype(BF16)
    b1 = b.astype(BF16)
    r1 = b - b1.astype(F32)
    b2 = r1.astype(BF16)
    b3 = (r1 - b2.astype(F32)).astype(BF16)
    return _dot(sel, b1) + (_dot(sel, b2) + _dot(sel, b3))


def _rwkv_kernel(r_ref, k_ref, v_ref, lo_ref, s0_ref, w0_ref, wup_ref, a0_ref, aup_ref, kkw_ref, kaw_ref,
                 rkw_ref, j_ref, o_ref, bon_ref, sout_ref, ZT, *, reverse):
    i = pl.program_id(1)
    TB = _CHUNK
    NP = _RW_HEADS // 2
    BB = r_ref.shape[0]
    N = BB * TB

    @pl.when(i == 0)
    def _():
        ZT[...] = s0_ref[...]

    r = r_ref[...].reshape(N, _RW_WIDTH)
    k = k_ref[...].reshape(N, _RW_WIDTH)
    v = v_ref[...].reshape(N, _RW_WIDTH)
    lo = lo_ref[...].reshape(N, _RW_WIDTH)
    jm = j_ref[...]
    w_log = -_softplus(-(w0_ref[...] + _dot3(jnp.tanh(lo[:, 0:_LANE]), wup_ref[...]))) - 0.5
    a = _sigmoid(a0_ref[...] + _dot3(lo[:, _LANE:2 * _LANE], aup_ref[...]))
    kk0 = k * kkw_ref[...]
    kk = kk0 / jnp.maximum(jnp.sqrt(_dot_sel(kk0 * kk0, jm)), 1e-12)
    km = k * (1.0 + (a - 1.0) * kaw_ref[...])
    bon_ref[...] = (_dot_sel(r * km * rkw_ref[...], jm) * v).reshape(BB, TB, _RW_WIDTH)

    lw = -jnp.exp(w_log)
    g = _sel_dot(_chunk_tri(N, TB, reverse), lw)
    e0 = 0 if reverse else TB - 1
    g_end = jnp.concatenate([jnp.broadcast_to(g[bb * TB + e0:bb * TB + e0 + 1], (TB, _RW_WIDTH))
                             for bb in range(BB)], axis=0)
    pm = kk * jnp.exp(g - lw)
    qm = (kk * a) * jnp.exp(-g)
    khm = km * jnp.exp(-g)
    rhm = r * jnp.exp(g)
    qgm = (kk * a) * jnp.exp(g_end - g)
    kgm = km * jnp.exp(g_end - g)
    gam = jnp.exp(g_end)

    t_i = lax.broadcasted_iota(jnp.int32, (TB, _LANE), 0)
    lane = lax.broadcasted_iota(jnp.int32, (TB, _LANE), 1)
    s_i = jnp.bitwise_and(lane, _RW_HD - 1)
    m_a = lane < _RW_HD
    strict = (s_i > t_i) if reverse else (s_i < t_i)
    incl = (s_i >= t_i) if reverse else (s_i <= t_i)
    eye = jnp.where(s_i == t_i, 1.0, 0.0)
    same = ((lax.broadcasted_iota(jnp.int32, (_LANE, _LANE), 0) < _RW_HD)
            == (lax.broadcasted_iota(jnp.int32, (_LANE, _LANE), 1) < _RW_HD))

    def bd(x):
        return jnp.concatenate([jnp.where(m_a, x, 0.0), jnp.where(m_a, 0.0, x)], axis=0)

    units = [(bb, p) for bb in range(BB) for p in range(NP)]
    nu = range(len(units))

    def cut(x, un):
        bb, p = un
        return x[bb * TB:(bb + 1) * TB, p * _LANE:(p + 1) * _LANE]

    lo_h, hi_h = slice(0, _LANE), slice(_LANE, 2 * _LANE)
    gram = [_dot3(jnp.concatenate([cut(pm, un), cut(rhm, un)], axis=0),
                  jnp.concatenate([bd(cut(qm, un)), bd(cut(khm, un))], axis=0), nt=True) for un in units]
    l_pq = [jnp.where(strict, x[0:TB, lo_h], 0.0) for x in gram]
    l_pk = [jnp.where(strict, x[0:TB, hi_h], 0.0) for x in gram]
    m_rq = [jnp.where(incl, x[TB:2 * TB, lo_h], 0.0) for x in gram]
    m_rk = [jnp.where(incl, x[TB:2 * TB, hi_h], 0.0) for x in gram]
    wm = [_dot(jnp.concatenate([l_pk[n], m_rk[n]], axis=0).astype(BF16), bd(cut(v, units[n])).astype(BF16))
          for n in nu]
    kv = [_dot_tn(cut(v, un).astype(BF16), cut(kgm, un).astype(BF16)) for un in units]
    s_m = [eye - x for x in l_pq]
    m_m = [_dot3(x, bd(x)) for x in l_pq]
    nlev = TB.bit_length() - 1
    for lev in range(1, nlev):
        if lev < nlev - 1:
            xs = [_dot3(m_m[n], jnp.concatenate([bd(s_m[n]), bd(m_m[n])], axis=1)) for n in nu]
            s_m = [s_m[n] + xs[n][:, lo_h] for n in nu]
            m_m = [x[:, hi_h] for x in xs]
        else:
            s_m = [s_m[n] + _dot3(m_m[n], bd(s_m[n])) for n in nu]
    ta = [_dot3(s_m[n], jnp.concatenate([bd(cut(pm, units[n])), bd(wm[n][0:TB])], axis=1)) for n in nu]
    zt = [ZT[bb, p] for bb, p in units]
    az = [_dot_nt(jnp.concatenate([ta[n][:, lo_h], cut(rhm, units[n])], axis=0).astype(BF16), zt[n].astype(BF16))
          for n in nu]
    u = [az[n][0:TB] + ta[n][:, hi_h] for n in nu]
    mu = [_dot(m_rq[n].astype(BF16), bd(u[n]).astype(BF16)) for n in nu]
    qu = [_dot_tn(u[n].astype(BF16), cut(qgm, units[n]).astype(BF16)) for n in nu]
    for n in nu:
        bb, p = units[n]
        o_ref[bb, :, p * _LANE:(p + 1) * _LANE] = az[n][TB:2 * TB] - mu[n] + wm[n][TB:2 * TB]
        ZT[bb, p] = zt[n] * cut(gam, units[n])[0:1] + jnp.where(same, kv[n] - qu[n], 0.0)

    @pl.when(i == pl.num_programs(1) - 1)
    def _():
        sout_ref[...] = ZT[...]


def _rwkv_call(rkv, lora, s0, wts, col_major, reverse, rows):
    B, T, _ = rkv.shape
    C = _RW_WIDTH
    nblk = T // _CHUNK
    if col_major:
        assert rows == _CHUNK
    rv = _seq_view(rkv, col_major, rows)
    lv = _seq_view(lora, col_major, rows)
    NP = _RW_HEADS // 2
    bb = _scan_bb(B)
    seq = functools.partial(_seq_spec, col_major=col_major, nblk=nblk, reverse=reverse, bb=bb)
    out_shape = rv.shape[:2] + (rv.shape[2] // 3,)
    st_spec = pl.BlockSpec((bb, NP, _LANE, _LANE), lambda b, i: (b, 0, 0, 0))
    o, bon, s_out = pl.pallas_call(
        functools.partial(_rwkv_kernel, reverse=reverse),
        grid=(B // bb, nblk),
        in_specs=[seq(C, 3 * C, 0), seq(C, 3 * C, 1), seq(C, 3 * C, 2), seq(C, C, 0), st_spec,
                  _full_spec((1, C)), _full_spec((2 * _RW_LORA, C)), _full_spec((1, C)),
                  _full_spec((2 * _RW_LORA, C)), _full_spec((1, C)), _full_spec((1, C)), _full_spec((1, C)),
                  _full_spec((C, C))],
        out_specs=[seq(C, C, 0), seq(C, C, 0), st_spec],
        out_shape=[jax.ShapeDtypeStruct(out_shape, F32), jax.ShapeDtypeStruct(out_shape, F32),
                   jax.ShapeDtypeStruct((B, NP, _LANE, _LANE), F32)],
        scratch_shapes=[pltpu.VMEM((bb, NP, _LANE, _LANE), F32)],
        compiler_params=_cp(("arbitrary", "arbitrary"), _VMEM_LIMIT),
        name="rwkv7_scan",
    )(rv, rv, rv, lv, s0, *wts)
    return o.reshape(B, T, C), bon.reshape(B, T, C), s_out


def _gla_kernel(q_ref, k_ref, v_ref, lo_ref, s0_ref, aup_ref, ab_ref, o_ref, sout_ref, ST, *, reverse):
    i = pl.program_id(1)
    TB = _CHUNK

    @pl.when(i == 0)
    def _():
        ST[...] = s0_ref[...]

    BB = q_ref.shape[0]
    N = BB * TB
    q = _silu(q_ref[...].reshape(N, _GLA_PW)) * (_GLA_DK ** -0.5)
    k = _silu(k_ref[...].reshape(N, _GLA_PW))
    v = _silu(v_ref[...].reshape(N, _GLA_PW))
    gad = lo_ref[...].reshape(N, _RW_WIDTH)[:, 2 * _LANE:3 * _LANE]
    log_a = -_softplus(-(_dot3(gad, aup_ref[...]) + ab_ref[...])) / _GLA_TAU
    row = lax.broadcasted_iota(jnp.int32, (TB, TB), 0)
    col = lax.broadcasted_iota(jnp.int32, (TB, TB), 1)
    keep = (row <= col) if reverse else (row >= col)
    g_cum = _sel_dot(_chunk_tri(N, TB, reverse), log_a)
    e0 = 0 if reverse else TB - 1
    g_last = jnp.concatenate([jnp.broadcast_to(g_cum[bb * TB + e0:bb * TB + e0 + 1], (TB, _GLA_PW))
                              for bb in range(BB)], axis=0)
    q_in = q * jnp.exp(g_cum)
    k_in = k * jnp.exp(-g_cum)
    k_tail = k * jnp.exp(g_last - g_cum)
    dec = jnp.exp(g_last)
    units = [(bb, h) for bb in range(BB) for h in range(_GLA_HEADS)]
    nu = range(len(units))

    def cut(x, un):
        bb, h = un
        return x[bb * TB:(bb + 1) * TB, h * _LANE:(h + 1) * _LANE]

    qh = [cut(q_in, un).astype(BF16) for un in units]
    vh = [cut(v, un).astype(BF16) for un in units]
    st = [ST[bb, h] for bb, h in units]
    att = [_dot_nt(qh[n], cut(k_in, units[n]).astype(BF16)) for n in nu]
    o_inter = [_dot_nt(qh[n], st[n].astype(BF16)) for n in nu]
    kv = [_dot_tn(vh[n], cut(k_tail, units[n]).astype(BF16)) for n in nu]
    o_intra = [_dot(jnp.where(keep, att[n], 0.0).astype(BF16), vh[n]) for n in nu]
    for n in nu:
        bb, h = units[n]
        o_ref[bb, :, h * _LANE:(h + 1) * _LANE] = o_intra[n] + o_inter[n]
        ST[bb, h] = st[n] * cut(dec, units[n])[0:1] + kv[n]

    @pl.when(i == pl.num_programs(1) - 1)
    def _():
        sout_ref[...] = ST[...]


def _gla_call(qkv, lora, s0, aup, ab, col_major, reverse, rows):
    B, T, _ = qkv.shape
    C = _GLA_PW
    nblk = T // _CHUNK
    if col_major:
        assert rows == _CHUNK
    qv = _seq_view(qkv, col_major, rows)
    lv = _seq_view(lora, col_major, rows)
    bb = _scan_bb(B)
    seq = functools.partial(_seq_spec, col_major=col_major, nblk=nblk, reverse=reverse, bb=bb)
    out_shape = qv.shape[:2] + (qv.shape[2] // 3,)
    st_spec = pl.BlockSpec((bb, _GLA_HEADS, _LANE, _LANE), lambda b, i: (b, 0, 0, 0))
    o, s_out = pl.pallas_call(
        functools.partial(_gla_kernel, reverse=reverse),
        grid=(B // bb, nblk),
        in_specs=[seq(C, 3 * C, 0), seq(C, 3 * C, 1), seq(C, 3 * C, 2), seq(_RW_WIDTH, _RW_WIDTH, 0), st_spec,
                  _full_spec((_LANE, C)), _full_spec((1, C))],
        out_specs=[seq(C, C, 0), st_spec],
        out_shape=[jax.ShapeDtypeStruct(out_shape, F32),
                   jax.ShapeDtypeStruct((B, _GLA_HEADS, _LANE, _LANE), F32)],
        scratch_shapes=[pltpu.VMEM((bb, _GLA_HEADS, _LANE, _LANE), F32)],
        compiler_params=_cp(("arbitrary", "arbitrary"), _VMEM_LIMIT),
        name="gla_chunked",
    )(qv, qv, qv, lv, s0, aup, ab)
    return o.reshape(B, T, C), s_out


def _fourier_kernel(x_ref, cg_ref, sg_ref, w_ref, o_ref, xcs_ref):
    T = x_ref.shape[1]

    @pl.when(pl.program_id(1) == 0)
    def _():
        x = x_ref[0]
        xcs_ref[0:T, :] = _dot3(x, cg_ref[...]).astype(BF16)
        xcs_ref[T:2 * T, :] = _dot3(x, sg_ref[...]).astype(BF16)

    o_ref[0] = _dot(w_ref[...], xcs_ref[...])


def _fourier_call(zpost, cg, sg, wt):
    B, T, _ = zpost.shape
    C = _FN_WIDTH
    tm = _tile(T, 512, 8)
    return pl.pallas_call(
        _fourier_kernel,
        grid=(B, T // tm),
        in_specs=[pl.BlockSpec((1, T, C), lambda b, i: (b, 0, 0)),
                  _full_spec((C, C)), _full_spec((C, C)),
                  pl.BlockSpec((tm, 2 * T), lambda b, i: (i, 0))],
        out_specs=pl.BlockSpec((1, tm, C), lambda b, i: (b, i, 0)),
        out_shape=jax.ShapeDtypeStruct((B, T, C), F32),
        scratch_shapes=[pltpu.VMEM((2 * T, C), BF16)],
        compiler_params=_cp(("arbitrary", "arbitrary"), _VMEM_LIMIT),
        name="fnet_dft",
    )(zpost, cg, sg, wt)


def _merge_kernel(rwf_ref, rwb_ref, bnf_ref, bnb_ref, glf_ref, glb_ref, zp_ref, fn_ref, h_ref, g1_ref, sc_ref, sh_ref,
                  nw_ref, wg_ref, jr_ref, lnw_ref, lnb_ref, gup_ref, jg_ref, gnw_ref, pa_ref, pb_ref, pc_ref, wo_ref,
                  o_ref):
    D = h_ref.shape[2]
    zp = zp_ref[0]
    og = zp[:, _FN_WIDTH:_FN_WIDTH + _GLA_PW]
    gd = zp[:, _FN_WIDTH + _GLA_PW:_FN_WIDTH + _GLA_PW + _RW_GATE_RANK]
    o = rwf_ref[0] + rwb_ref[0]
    jr = jr_ref[...]
    mu = _dot_sel(o, jr) * (1.0 / _RW_HD)
    xc = o - mu
    var = _dot_sel(xc * xc, jr) * (1.0 / _RW_HD)
    y = xc * lax.rsqrt(var + _RW_GN_EPS)
    y = y * lnw_ref[...] + lnb_ref[...] + (bnf_ref[0] + bnb_ref[0])
    rw_y = y * _dot3(_sigmoid(gd), gup_ref[...])
    g = glf_ref[0] + glb_ref[0]
    ms = _dot_sel(g * g, jg_ref[...]) * (1.0 / _GLA_DV)
    gla_y = g * lax.rsqrt(ms + _NORM_EPS) * gnw_ref[...] * _silu(og)
    h = h_ref[0]
    u = _modnorm(h, nw_ref[...], sc_ref[0], sh_ref[0]).astype(BF16)
    gates = _sigmoid(_dot(u, wg_ref[...]))
    m = (gates[:, 0:D] * _dot(rw_y.astype(BF16), pa_ref[...])
         + gates[:, D:2 * D] * _dot(gla_y.astype(BF16), pb_ref[...])
         + gates[:, 2 * D:3 * D] * _dot(fn_ref[0].astype(BF16), pc_ref[...]))
    o_ref[0] = h + g1_ref[0] * _dot(m.astype(BF16), wo_ref[...])


def _merge_call(rwf, rwb, bnf, bnb, glf, glb, zpost, fn, h, g1, sc, sh, wts):
    B, T, D = h.shape
    tm = _tile(T, 256, 8)

    def row(c):
        return pl.BlockSpec((1, tm, c), lambda b, i: (b, i, 0))

    acts = [rwf, rwb, bnf, bnb, glf, glb, zpost, fn, h]
    mods = [g1, sc, sh]
    return pl.pallas_call(
        _merge_kernel,
        grid=(B, T // tm),
        in_specs=[row(a.shape[2]) for a in acts] + [pl.BlockSpec((1, 1, D), lambda b, i: (b, 0, 0))] * len(mods)
        + [_full_spec(w.shape) for w in wts],
        out_specs=row(D),
        out_shape=jax.ShapeDtypeStruct((B, T, D), F32),
        compiler_params=_cp(("arbitrary", "arbitrary"), _VMEM_LIMIT),
        name="branch_merge",
    )(*acts, *mods, *wts)


def _router_kernel(h_ref, nw_ref, sc_ref, sh_ref, rt_ref, v_ref, aff_ref):
    u = _modnorm(h_ref[0], nw_ref[...], sc_ref[0], sh_ref[0])
    v_ref[0] = u.astype(BF16)
    logits = _dot_nt(rt_ref[...], u, HI)
    e = jnp.exp(logits - jnp.max(logits, axis=0, keepdims=True))
    aff_ref[0] = e / jnp.sum(e, axis=0, keepdims=True)


def _router_call(h, nw, sc, sh, router_t):
    B, T, D = h.shape
    E = router_t.shape[0]
    tm = _tile(T, 512, 128)
    return pl.pallas_call(
        _router_kernel,
        grid=(B, T // tm),
        in_specs=[pl.BlockSpec((1, tm, D), lambda b, i: (b, i, 0)), _full_spec((1, D)),
                  pl.BlockSpec((1, 1, D), lambda b, i: (b, 0, 0)), pl.BlockSpec((1, 1, D), lambda b, i: (b, 0, 0)),
                  _full_spec((E, D))],
        out_specs=[pl.BlockSpec((1, tm, D), lambda b, i: (b, i, 0)), pl.BlockSpec((1, E, tm), lambda b, i: (b, 0, i))],
        out_shape=[jax.ShapeDtypeStruct((B, T, D), BF16), jax.ShapeDtypeStruct((B, E, T), F32)],
        compiler_params=_cp(("arbitrary", "arbitrary"), _VMEM_LIMIT),
        name="router_softmax",
    )(h, nw, sc, sh, router_t)


def _topc_kernel(aff_ref, pos_ref, cum_ref, *, cap):
    E, T = aff_ref.shape[1], aff_ref.shape[2]
    x = pltpu.bitcast(aff_ref[0], jnp.int32)

    def body(it, thr):
        cand = thr | lax.shift_left(jnp.int32(1), 30 - it)
        cnt = jnp.sum(jnp.where(x >= cand, 1, 0), axis=1, keepdims=True)
        return jnp.where(cnt >= cap, cand, thr)

    thr = lax.fori_loop(0, 31, body, jnp.zeros((E, 1), jnp.int32))
    need = (cap - jnp.sum(jnp.where(x > thr, 1, 0), axis=1, keepdims=True)).astype(F32)
    blk = _LANE
    upper = jnp.where(lax.broadcasted_iota(jnp.int32, (blk, blk), 0) < lax.broadcasted_iota(jnp.int32, (blk, blk), 1),
                      1.0, 0.0).astype(BF16)
    off_eq = jnp.zeros((E, 1), F32)
    off_sel = jnp.zeros((E, 1), F32)
    for c in range(T // blk):
        sl = slice(c * blk, (c + 1) * blk)
        xc = pltpu.bitcast(aff_ref[0, :, sl], jnp.int32)
        eq_c = jnp.where(xc == thr, 1.0, 0.0)
        rank_eq = _dot(eq_c.astype(BF16), upper) + off_eq
        take = jnp.where(rank_eq < need, eq_c, 0.0)
        sel = jnp.where(xc > thr, 1.0, take)
        rank = _dot(sel.astype(BF16), upper) + off_sel
        pos_ref[0, :, sl] = jnp.where(sel > 0.0, rank.astype(jnp.int32), -1)
        off_eq = off_eq + jnp.sum(eq_c, axis=1, keepdims=True)
        off_sel = off_sel + jnp.sum(sel, axis=1, keepdims=True)
    chosen = jnp.where(pos_ref[0] >= 0, 1.0, 0.0).astype(BF16)
    before = jnp.where(lax.shift_right_logical(lax.broadcasted_iota(jnp.int32, (T, blk), 0), blk.bit_length() - 1)
                       < lax.broadcasted_iota(jnp.int32, (T, blk), 1), 1.0, 0.0).astype(BF16)
    cum_ref[0] = _dot(chosen, before).astype(jnp.int32)


def _topc_call(aff, cap):
    B, E, T = aff.shape
    assert T // _LANE < _LANE
    pos, cum = pl.pallas_call(
        functools.partial(_topc_kernel, cap=cap),
        grid=(B,),
        in_specs=[pl.BlockSpec((1, E, T), lambda b: (b, 0, 0))],
        out_specs=[pl.BlockSpec((1, E, T), lambda b: (b, 0, 0)), pl.BlockSpec((1, E, _LANE), lambda b: (b, 0, 0))],
        out_shape=[jax.ShapeDtypeStruct((B, E, T), jnp.int32), jax.ShapeDtypeStruct((B, E, _LANE), jnp.int32)],
        compiler_params=_cp(("arbitrary",), _VMEM_LIMIT),
        name="expert_choice_topc",
    )(aff)
    return pos, cum[:, :, :T // _LANE + 1].reshape(-1)


def _slot_block(cap):
    return min(cap, _LANE)


def _ffn_kernel(cum_ref, v_ref, pos_ref, aff_ref, w1_ref, w3_ref, w2_ref, ys_ref, xs_ref, acc_ref, gate_ref, *, cap, tc,
                expert_axis):
    e = pl.program_id(expert_axis)
    b = pl.program_id(1 - expert_axis)
    f = pl.program_id(2)
    T = v_ref.shape[1]
    E = pos_ref.shape[1]
    sbz = _slot_block(cap)

    @pl.when(f == 0)
    def _():
        acc_ref[...] = jnp.zeros(acc_ref.shape, F32)
        gate_ref[...] = jnp.zeros(gate_ref.shape, F32)
        base = (b * E + e) * (T // _LANE + 1)
        for c in range(T // tc):
            sl = slice(c * tc, (c + 1) * tc)
            lo = cum_ref[base + c * (tc // _LANE)]
            hi = cum_ref[base + (c + 1) * (tc // _LANE)]
            for sb in range(cap // sbz):
                rows = slice(sb * sbz, (sb + 1) * sbz)

                @pl.when(jnp.logical_and(lo < (sb + 1) * sbz, hi > sb * sbz))
                def _():
                    slot = lax.broadcasted_iota(jnp.int32, (sbz, tc), 0) + sb * sbz
                    hit = slot == pos_ref[0, e, :, sl]
                    acc_ref[rows, :] += _dot(jnp.where(hit, 1.0, 0.0).astype(BF16), v_ref[0, sl, :])
                    gate_ref[rows, :] += jnp.sum(jnp.where(hit, aff_ref[0, e, :, sl], 0.0), axis=1, keepdims=True)
        xs_ref[...] = acc_ref[...].astype(BF16)
        acc_ref[...] = jnp.zeros(acc_ref.shape, F32)

    x = xs_ref[...]
    h1 = _dot(x, w1_ref[0])
    hid = _silu(h1) * _dot(x, w3_ref[0])
    acc_ref[...] += _dot(hid.astype(BF16), w2_ref[0])

    @pl.when(f == pl.num_programs(2) - 1)
    def _():
        ys_ref[0, 0] = (acc_ref[...] * gate_ref[...]).astype(BF16)


def _ffn_call(cum, v, pos, aff, w1, w3, w2, cap):
    B, T, D = v.shape
    E, _, F = w1.shape
    tc = _tile(T, 512)
    expert_major = T <= 512
    fc = F if expert_major else _tile(F, 512)
    if expert_major:
        grid = (E, B, 1)

        def ix(f):
            return lambda e, b, j: f(b, e, j)
    else:
        grid = (B, E, F // fc)

        def ix(f):
            return f
    return pl.pallas_call(
        functools.partial(_ffn_kernel, cap=cap, tc=tc, expert_axis=0 if expert_major else 1),
        grid=grid,
        in_specs=[pl.BlockSpec(memory_space=pltpu.SMEM),
                  pl.BlockSpec((1, T, D), ix(lambda b, e, f: (b, 0, 0))),
                  pl.BlockSpec((1, E, 1, T), ix(lambda b, e, f: (b, 0, 0, 0))),
                  pl.BlockSpec((1, E, 1, T), ix(lambda b, e, f: (b, 0, 0, 0))),
                  pl.BlockSpec((1, D, fc), ix(lambda b, e, f: (e, 0, f))),
                  pl.BlockSpec((1, D, fc), ix(lambda b, e, f: (e, 0, f))),
                  pl.BlockSpec((1, fc, D), ix(lambda b, e, f: (e, f, 0)))],
        out_specs=pl.BlockSpec((1, 1, cap, D), ix(lambda b, e, f: (b, e, 0, 0))),
        out_shape=jax.ShapeDtypeStruct((B, E, cap, D), BF16),
        scratch_shapes=[pltpu.VMEM((cap, D), BF16), pltpu.VMEM((cap, D), F32), pltpu.VMEM((cap, 1), F32)],
        compiler_params=_cp(("arbitrary", "arbitrary", "arbitrary"), _VMEM_LIMIT),
        name="expert_ffn",
    )(cum, v, pos, aff, w1, w3, w2)


def _scatter_kernel(cum_ref, pos_ref, ys_ref, h_ref, g2_ref, o_ref, acc_ref, *, cap, nblk):
    b = pl.program_id(0)
    i = pl.program_id(1)
    e = pl.program_id(2)
    tm = h_ref.shape[1]
    E = pos_ref.shape[1]
    sbz = _slot_block(cap)

    @pl.when(e == 0)
    def _():
        acc_ref[...] = jnp.zeros(acc_ref.shape, F32)

    base = (b * E + e) * (nblk + 1) + i * (tm // _LANE)
    lo = cum_ref[base]
    hi = cum_ref[base + tm // _LANE]
    for sb in range(cap // sbz):
        @pl.when(jnp.logical_and(lo < (sb + 1) * sbz, hi > sb * sbz))
        def _():
            hit = lax.broadcasted_iota(jnp.int32, (sbz, tm), 0) + sb * sbz == pos_ref[0, e]
            acc_ref[...] += _dot_tn(jnp.where(hit, 1.0, 0.0).astype(BF16), ys_ref[0, 0, sb * sbz:(sb + 1) * sbz, :])

    @pl.when(e == pl.num_programs(2) - 1)
    def _():
        o_ref[0] = h_ref[0] + g2_ref[0] * acc_ref[...]


def _scatter_call(cum, pos, ys, h, g2, cap):
    B, T, D = h.shape
    E = pos.shape[1]
    tm = _tile(T, 1024)
    return pl.pallas_call(
        functools.partial(_scatter_kernel, cap=cap, nblk=T // _LANE),
        grid=(B, T // tm, E),
        in_specs=[pl.BlockSpec(memory_space=pltpu.SMEM),
                  pl.BlockSpec((1, E, 1, tm), lambda b, i, e: (b, 0, 0, i)),
                  pl.BlockSpec((1, 1, cap, D), lambda b, i, e: (b, e, 0, 0)),
                  pl.BlockSpec((1, tm, D), lambda b, i, e: (b, i, 0)),
                  pl.BlockSpec((1, 1, D), lambda b, i, e: (b, 0, 0))],
        out_specs=pl.BlockSpec((1, tm, D), lambda b, i, e: (b, i, 0)),
        out_shape=jax.ShapeDtypeStruct((B, T, D), F32),
        scratch_shapes=[pltpu.VMEM((tm, D), F32)],
        compiler_params=_cp(("arbitrary", "arbitrary", "arbitrary"), _VMEM_LIMIT),
        name="expert_scatter",
    )(cum, pos, ys, h, g2)


def _final_kernel(h_ref, w_ref, o_ref):
    x = h_ref[0]
    o_ref[0] = x * lax.rsqrt(jnp.mean(x * x, axis=-1, keepdims=True) + _NORM_EPS) * w_ref[...]


def _final_call(h, w):
    B, T, D = h.shape
    tm = _tile(T, 1024, 8)
    return pl.pallas_call(
        _final_kernel,
        grid=(B, T // tm),
        in_specs=[pl.BlockSpec((1, tm, D), lambda b, i: (b, i, 0)), _full_spec((1, D))],
        out_specs=pl.BlockSpec((1, tm, D), lambda b, i: (b, i, 0)),
        out_shape=jax.ShapeDtypeStruct((B, T, D), F32),
        compiler_params=_cp(("arbitrary", "arbitrary"), _VMEM_LIMIT),
        name="final_norm",
    )(h, w)


def _pad_heads(x, nh, d, axis=-1):
    axis = axis % x.ndim
    shp = x.shape
    x = x.reshape(shp[:axis] + (nh, d) + shp[axis + 1:])
    pad = [(0, 0)] * x.ndim
    pad[axis + 1] = (0, _LANE - d)
    x = jnp.pad(x, pad)
    return x.reshape(shp[:axis] + (nh * _LANE,) + shp[axis + 1:])


def _pad_to(x, n, axis=-1):
    axis = axis % x.ndim
    pad = [(0, 0)] * x.ndim
    pad[axis] = (0, n - x.shape[axis])
    return jnp.pad(x, pad)


def _block_ones(n, blk):
    i = np.arange(n) // blk
    return jnp.asarray((i[:, None] == i[None, :]).astype(np.float32))


def _dft_consts(T):
    s = 1 << ((T.bit_length() - 1) // 2)
    i = np.arange(T)[:, None]
    a1 = 2.0 * np.pi * ((i * s * np.arange(T // s)[None, :]) % T) / T
    a2 = 2.0 * np.pi * ((i * np.arange(s)[None, :]) % T) / T
    c1, s1, c2, s2 = (jnp.asarray(f(a) / math.sqrt(math.sqrt(T)), F32)
                      for a, f in ((a1, np.cos), (a1, np.sin), (a2, np.cos), (a2, np.sin)))
    cos = (c1[:, :, None] * c2[:, None, :] - s1[:, :, None] * s2[:, None, :]).reshape(T, T)
    sin = (s1[:, :, None] * c2[:, None, :] + c1[:, :, None] * s2[:, None, :]).reshape(T, T)
    wt = jnp.concatenate([cos, -sin], axis=1)
    g = np.arange(_FN_GD)
    ang_g = 2.0 * np.pi * ((g[:, None] * g[None, :]) % _FN_GD) / _FN_GD
    eye = np.eye(_FN_GROUPS)
    cg = np.kron(eye, np.cos(ang_g)) / math.sqrt(_FN_GD)
    sg = np.kron(eye, np.sin(ang_g)) / math.sqrt(_FN_GD)
    return jnp.asarray(wt, BF16), jnp.asarray(cg, F32), jnp.asarray(sg, F32)


def _layer_weights(i, w_in, conv_w, rw_w0, rw_w_up, rw_a0, rw_a_up, rw_k_k, rw_k_a, rw_r_k, rw_g_up, rw_ln_w,
                   rw_ln_b, gla_a_up, gla_a_b, gla_norm_w, proj_a, proj_b, proj_c, w_out):
    D = w_in.shape[1]
    rw, kw, vw = _RW_WIDTH, _GLA_HEADS * _GLA_DK, _GLA_HEADS * _GLA_DV
    o_gq = 3 * rw
    o_gk = o_gq + kw
    o_gv = o_gk + kw
    o_wd = o_gv + vw
    o_ad = o_wd + 2 * _RW_LORA
    o_ga = o_ad + 2 * _RW_LORA
    o_gd = o_ga + 2 * _GLA_RANK
    o_og = o_gd + _RW_GATE_RANK
    o_fn = o_og + vw
    o_gt = o_fn + _FN_WIDTH
    wi = w_in[i]
    cw = conv_w[i].reshape(9, -1)

    def gla_cols(x):
        return jnp.concatenate([_pad_heads(x[..., o_gq:o_gk], _GLA_HEADS, _GLA_DK),
                                _pad_heads(x[..., o_gk:o_gv], _GLA_HEADS, _GLA_DK),
                                _pad_heads(x[..., o_gv:o_wd], _GLA_HEADS, _GLA_DV)], axis=-1)

    w = {}
    w['in_rw'] = wi[:, 0:o_gq].astype(BF16)
    w['in_gla'] = gla_cols(wi).astype(BF16)
    w['in_lora'] = jnp.concatenate([wi[:, o_wd:o_ga], _pad_to(wi[:, o_ga:o_gd], _LANE)], axis=-1).astype(BF16)
    w['in_post'] = jnp.concatenate([wi[:, o_fn:o_gt], _pad_heads(wi[:, o_og:o_fn], _GLA_HEADS, _GLA_DV),
                                    wi[:, o_gd:o_og]], axis=-1).astype(BF16)
    w['in_gates'] = wi[:, o_gt:].astype(BF16)
    w['conv_rw'] = cw[:, 0:o_gq]
    w['conv_gla'] = gla_cols(cw)
    zl = jnp.zeros((_RW_LORA, rw), F32)
    w['rwkv'] = []
    w['gla'] = []
    jr = _block_ones(rw, _RW_HD)
    for d in range(2):
        wup = jnp.concatenate([rw_w_up[i, d], zl] if d == 0 else [zl, rw_w_up[i, d]], axis=0)
        aup = jnp.concatenate([rw_a_up[i, d], zl] if d == 0 else [zl, rw_a_up[i, d]], axis=0)
        w['rwkv'].append((rw_w0[i, d][None], wup, rw_a0[i, d][None], aup, rw_k_k[i][None], rw_k_a[i][None],
                          rw_r_k[i].reshape(1, rw), jr))
        ga = _pad_heads(gla_a_up[i, d], _GLA_HEADS, _GLA_DK)
        ga = jnp.pad(ga, ((d * _GLA_RANK, _LANE - (d + 1) * _GLA_RANK), (0, 0)))
        w['gla'].append((ga, _pad_heads(gla_a_b[i, d][None], _GLA_HEADS, _GLA_DK)))
    w['merge'] = (jr, rw_ln_w[i][None], rw_ln_b[i][None], rw_g_up[i], _block_ones(_GLA_PW, _LANE),
                  _pad_heads(jnp.tile(gla_norm_w[i], _GLA_HEADS)[None], _GLA_HEADS, _GLA_DV),
                  proj_a[i].astype(BF16), _pad_heads(proj_b[i], _GLA_HEADS, _GLA_DV, axis=0).astype(BF16),
                  proj_c[i].astype(BF16), w_out[i].astype(BF16))
    return w


def _moe(h, nw, sc, sh, g2, router_t, w1, w3, w2):
    B, T, D = h.shape
    E = router_t.shape[0]
    cap = 2 * T // E
    v, aff = _router_call(h, nw, sc, sh, router_t)
    pos, cum = _topc_call(aff, cap)
    pos = pos.reshape(B, E, 1, T)
    ys = _ffn_call(cum, v, pos, aff.reshape(B, E, 1, T), w1, w3, w2, cap)
    return _scatter_call(cum, pos, ys, h, g2, cap)


def kernel(x, c, ctx, c_ctx, w_mod, b_mod, norm1_w, norm2_w, w_in, conv_w, rw_w0, rw_w_up, rw_a0, rw_a_up, rw_k_k, rw_k_a, rw_r_k, rw_g_up, rw_ln_w, rw_ln_b, gla_a_up, gla_a_b, gla_norm_w, proj_a, proj_b, proj_c, w_out, router, exp_w1, exp_w3, exp_w2, final_norm_w):
    B, S, D = x.shape
    TC = ctx.shape[1]
    L = w_mod.shape[0]
    W = _GRID_W
    rows = S // W
    assert S % _CHUNK == 0 and TC % _CHUNK == 0

    cc = _pad_to(jnp.concatenate([c, c_ctx[None]], axis=0), 16, axis=0)
    mods = _mod_call(cc, w_mod, b_mod)
    dft = {S: _dft_consts(S), TC: _dft_consts(TC)}
    s0_rw = jnp.zeros((B, _RW_HEADS // 2, _LANE, _LANE), F32)
    s0_gla = jnp.zeros((B, _GLA_HEADS, _LANE, _LANE), F32)

    h_lat, h_ctx = x, ctx
    for i in range(L):
        col_major = i % 2 == 1
        need_ctx = i < L - 1
        lw = _layer_weights(i, w_in, conv_w, rw_w0, rw_w_up, rw_a0, rw_a_up, rw_k_k, rw_k_a, rw_r_k, rw_g_up,
                            rw_ln_w, rw_ln_b, gla_a_up, gla_a_b, gla_norm_w, proj_a, proj_b, proj_c, w_out)
        m_lat = mods[i, :B].reshape(B, 1, _N_MOD, D)
        m_ctx = jnp.broadcast_to(mods[i, B:B + 1].reshape(1, 1, _N_MOD, D), (B, 1, _N_MOD, D))
        n1 = norm1_w[i][None]
        n2 = norm2_w[i][None]

        def in_proj(h, m, img_rows, img_w):
            z = {k: _norm_matmul(h, n1, m[:, :, 1], m[:, :, 0], lw['in_' + k])
                 for k in ('rw', 'gla', 'lora', 'post')}
            z['rw'] = _conv_call(z['rw'], lw['conv_rw'], img_rows, img_w)
            z['gla'] = _conv_call(z['gla'], lw['conv_gla'], img_rows, img_w)
            return z

        z_lat = in_proj(h_lat, m_lat, rows, W)
        z_ctx = in_proj(h_ctx, m_ctx, 1, TC)

        rw_lat, rw_ctx, gl_lat, gl_ctx = [], [], [], []
        for d in range(2):
            rev = d == 1
            o_c, b_c, st = _rwkv_call(z_ctx['rw'], z_ctx['lora'], s0_rw, lw['rwkv'][d], False, rev, 1)
            o_l, b_l, _ = _rwkv_call(z_lat['rw'], z_lat['lora'], st, lw['rwkv'][d], col_major, rev, rows)
            rw_ctx.append((o_c, b_c))
            rw_lat.append((o_l, b_l))
            g_c, st = _gla_call(z_ctx['gla'], z_ctx['lora'], s0_gla, *lw['gla'][d], False, rev, 1)
            g_l, _ = _gla_call(z_lat['gla'], z_lat['lora'], st, *lw['gla'][d], col_major, rev, rows)
            gl_ctx.append(g_c)
            gl_lat.append(g_l)

        def mix(h, m, z, rwo, glo, T):
            wt, cg, sg = dft[T]
            fn = _fourier_call(z['post'], cg, sg, wt)
            return _merge_call(rwo[0][0], rwo[1][0], rwo[0][1], rwo[1][1], glo[0], glo[1], z['post'], fn, h,
                               m[:, :, 2], m[:, :, 1], m[:, :, 0], (n1, lw['in_gates']) + lw['merge'])

        router_t = router[i].T
        w1 = exp_w1[i].astype(BF16)
        w3 = exp_w3[i].astype(BF16)
        w2 = exp_w2[i].astype(BF16)
        h_lat = mix(h_lat, m_lat, z_lat, rw_lat, gl_lat, S)
        h_lat = _moe(h_lat, n2, m_lat[:, :, 4], m_lat[:, :, 3], m_lat[:, :, 5], router_t, w1, w3, w2)
        if need_ctx:
            h_ctx = mix(h_ctx, m_ctx, z_ctx, rw_ctx, gl_ctx, TC)
            h_ctx = _moe(h_ctx, n2, m_ctx[:, :, 4], m_ctx[:, :, 3], m_ctx[:, :, 5], router_t, w1, w3, w2)
    return _final_call(h_lat, final_norm_w[None])
```

```python
import functools
import math

import jax
import jax.numpy as jnp
import numpy as np
from jax import lax
from jax.experimental import pallas as pl
from jax.experimental.pallas import tpu as pltpu

F32 = jnp.float32
BF16 = jnp.bfloat16
HI = lax.Precision.HIGHEST

_GRID_W = 64
_NORM_EPS = 1e-6
_N_MOD = 6
_RW_HEADS = 6
_RW_HD = 64
_RW_WIDTH = _RW_HEADS * _RW_HD
_RW_LORA = 64
_RW_GATE_RANK = 128
_RW_GN_EPS = 64e-5
_GLA_HEADS = 4
_GLA_DK = 48
_GLA_DV = 96
_GLA_RANK = 16
_GLA_TAU = 16.0
_CHUNK = 64
_FN_GROUPS = 4
_FN_GD = 64
_FN_WIDTH = _FN_GROUPS * _FN_GD
_LANE = 128
_GLA_PW = _GLA_HEADS * _LANE
_VMEM_LIMIT = 56 * 1024 * 1024


def _cp(sem, vmem=None):
    return pltpu.CompilerParams(dimension_semantics=sem, vmem_limit_bytes=vmem)


def _tile(n, cap, mult=128):
    if n <= cap:
        return n
    best = None
    for t in range(mult, cap + 1, mult):
        if n % t == 0:
            best = t
    assert best is not None, (n, cap)
    return best


def _sigmoid(x):
    return 1.0 / (1.0 + jnp.exp(-x))


def _silu(x):
    return x * _sigmoid(x)


def _softplus(x):
    return jnp.maximum(x, 0.0) + jnp.log(1.0 + jnp.exp(-jnp.abs(x)))


def _dot(a, b, precision=None):
    return jnp.dot(a, b, preferred_element_type=F32, precision=precision)


def _dot_nt(a, b, precision=None):
    return lax.dot_general(a, b, (((1,), (1,)), ((), ())), preferred_element_type=F32, precision=precision)


def _dot_tn(a, b, precision=None):
    return lax.dot_general(a, b, (((0,), (0,)), ((), ())), preferred_element_type=F32, precision=precision)


def _mod_kernel(c_ref, w_ref, b_ref, o_ref):
    o_ref[0] = _dot(_silu(c_ref[...]), w_ref[0], HI) + b_ref[0]


def _mod_call(cc, w_mod, b_mod):
    L, D, N = w_mod.shape
    R = cc.shape[0]
    tn = _tile(N, 1536)
    return pl.pallas_call(
        _mod_kernel,
        grid=(L, N // tn),
        in_specs=[pl.BlockSpec((R, D), lambda l, j: (0, 0)),
                  pl.BlockSpec((1, D, tn), lambda l, j: (l, 0, j)),
                  pl.BlockSpec((1, 1, tn), lambda l, j: (l, 0, j))],
        out_specs=pl.BlockSpec((1, R, tn), lambda l, j: (l, 0, j)),
        out_shape=jax.ShapeDtypeStruct((L, R, N), F32),
        compiler_params=_cp(("arbitrary", "arbitrary"), _VMEM_LIMIT),
        name="adaln_mod",
    )(cc, w_mod, b_mod.reshape(L, 1, N))


def _modnorm(x, nw, sc, sh):
    y = x * lax.rsqrt(jnp.mean(x * x, axis=-1, keepdims=True) + _NORM_EPS)
    return (y * nw) * (1.0 + sc) + sh


def _nm_kernel(h_ref, nw_ref, sc_ref, sh_ref, w_ref, o_ref, u_ref):
    @pl.when(pl.program_id(2) == 0)
    def _():
        u_ref[...] = _modnorm(h_ref[0], nw_ref[...], sc_ref[0], sh_ref[0]).astype(BF16)

    o_ref[0] = _dot(u_ref[...], w_ref[...])


def _norm_matmul(h, nw, sc, sh, w):
    B, T, D = h.shape
    N = w.shape[1]
    tm = _tile(T, 1024, 8)
    tn = _tile(N, 1536)
    return pl.pallas_call(
        _nm_kernel,
        grid=(B, T // tm, N // tn),
        in_specs=[pl.BlockSpec((1, tm, D), lambda b, i, j: (b, i, 0)),
                  pl.BlockSpec((1, D), lambda b, i, j: (0, 0)),
                  pl.BlockSpec((1, 1, D), lambda b, i, j: (b, 0, 0)),
                  pl.BlockSpec((1, 1, D), lambda b, i, j: (b, 0, 0)),
                  pl.BlockSpec((D, tn), lambda b, i, j: (0, j))],
        out_specs=pl.BlockSpec((1, tm, tn), lambda b, i, j: (b, i, j)),
        out_shape=jax.ShapeDtypeStruct((B, T, N), F32),
        scratch_shapes=[pltpu.VMEM((tm, D), BF16)],
        compiler_params=_cp(("arbitrary", "arbitrary", "arbitrary"), _VMEM_LIMIT),
        name="norm_in_proj",
    )(h, nw, sc, sh, w)


def _conv_kernel(x_ref, w_ref, o_ref, xp_ref, *, rows, W, T, PAD, CH):
    cw = x_ref.shape[2]
    xp_ref[0:PAD, :] = jnp.zeros((PAD, cw), F32)
    xp_ref[PAD + T:PAD + T + PAD, :] = jnp.zeros((PAD, cw), F32)
    xp_ref[PAD:PAD + T, :] = x_ref[0]
    for c0 in range(0, T, CH):
        col = jnp.bitwise_and(lax.broadcasted_iota(jnp.int32, (CH, cw), 0) + c0, W - 1)
        acc = jnp.zeros((CH, cw), F32)
        for a in range(3):
            if rows == 1 and a != 1:
                continue
            for b in range(3):
                off = (a - 1) * W + (b - 1)
                xs = xp_ref[PAD + c0 + off:PAD + c0 + off + CH, :]
                if b == 0:
                    xs = jnp.where(col >= 1, xs, 0.0)
                elif b == 2:
                    xs = jnp.where(col <= W - 2, xs, 0.0)
                acc = acc + xs * w_ref[a * 3 + b:a * 3 + b + 1, :]
        o_ref[0, c0:c0 + CH, :] = acc


def _conv_call(z, w9, rows, W):
    B, T, C = z.shape
    assert W & (W - 1) == 0 and rows * W == T
    cw = _LANE
    PAD = W + 8 if rows > 1 else 8
    CH = min(T, 128)
    kern = functools.partial(_conv_kernel, rows=rows, W=W, T=T, PAD=PAD, CH=CH)
    return pl.pallas_call(
        kern,
        grid=(B, C // cw),
        in_specs=[pl.BlockSpec((1, T, cw), lambda b, j: (b, 0, j)),
                  pl.BlockSpec((9, cw), lambda b, j: (0, j))],
        out_specs=pl.BlockSpec((1, T, cw), lambda b, j: (b, 0, j)),
        out_shape=jax.ShapeDtypeStruct((B, T, C), F32),
        scratch_shapes=[pltpu.VMEM((T + 2 * PAD, cw), F32)],
        compiler_params=_cp(("arbitrary", "arbitrary"), _VMEM_LIMIT),
        name="short_conv",
    )(z, w9)


def _seq_view(arr, col_major, rows):
    B, T, C = arr.shape
    return arr.reshape(B, rows, (T // rows) * C) if col_major else arr


def _seq_spec(C, ctot, part, col_major, nblk, reverse, bb):
    nper = ctot // C

    def blk(i):
        return nblk - 1 - i if reverse else i

    if col_major:
        return pl.BlockSpec((bb, _CHUNK, C), lambda b, i: (b, 0, blk(i) * nper + part))
    return pl.BlockSpec((bb, _CHUNK, C), lambda b, i: (b, blk(i), part))


def _scan_bb(B):
    return 4 if B % 4 == 0 else (2 if B % 2 == 0 else 1)


def _chunk_tri(n, tb, reverse):
    row = lax.broadcasted_iota(jnp.int32, (n, n), 0)
    col = lax.broadcasted_iota(jnp.int32, (n, n), 1)
    sh = tb.bit_length() - 1
    same = lax.shift_right_logical(row, sh) == lax.shift_right_logical(col, sh)
    order = (row <= col) if reverse else (row >= col)
    return jnp.where(jnp.logical_and(same, order), 1.0, 0.0)


def _full_spec(shape):
    nd = len(shape)
    return pl.BlockSpec(shape, lambda b, i: (0,) * nd)


def _split(x):
    hi = x.astype(BF16)
    return hi, (x - hi.astype(F32)).astype(BF16)


def _dot3(a, b, nt=False):
    f = _dot_nt if nt else _dot
    ah, al = _split(a)
    bh, bl = _split(b)
    return f(ah, bh) + (f(ah, bl) + f(al, bh))


def _dot_sel(a, sel):
    sel = sel.astype(BF16)
    a1 = a.astype(BF16)
    r1 = a - a1.astype(F32)
    a2 = r1.astype(BF16)
    a3 = (r1 - a2.astype(F32)).astype(BF16)
    return _dot(a1, sel) + (_dot(a2, sel) + _dot(a3, sel))


def _sel_dot(sel, b):
    sel = sel.astype(BF16)
    b1 = b.astype(BF16)
    r1 = b - b1.astype(F32)
    b2 = r1.astype(BF16)
    b3 = (r1 - b2.astype(F32)).astype(BF16)
    return _dot(sel, b1) + (_dot(sel, b2) + _dot(sel, b3))


def _rwkv_kernel(r_ref, k_ref, v_ref, lo_ref, s0_ref, w0_ref, wup_ref, a0_ref, aup_ref, kkw_ref, kaw_ref,
                 rkw_ref, j_ref, o_ref, bon_ref, sout_ref, ZT, *, reverse):
    i = pl.program_id(1)
    TB = _CHUNK
    NP = _RW_HEADS // 2
    BB = r_ref.shape[0]
    N = BB * TB

    @pl.when(i == 0)
    def _():
        ZT[...] = s0_ref[...]

    r = r_ref[...].reshape(N, _RW_WIDTH)
    k = k_ref[...].reshape(N, _RW_WIDTH)
    v = v_ref[...].reshape(N, _RW_WIDTH)
    lo = lo_ref[...].reshape(N, _RW_WIDTH)
    jm = j_ref[...]
    w_log = -_softplus(-(w0_ref[...] + _dot3(jnp.tanh(lo[:, 0:_LANE]), wup_ref[...]))) - 0.5
    a = _sigmoid(a0_ref[...] + _dot3(lo[:, _LANE:2 * _LANE], aup_ref[...]))
    kk0 = k * kkw_ref[...]
    kk = kk0 / jnp.maximum(jnp.sqrt(_dot_sel(kk0 * kk0, jm)), 1e-12)
    km = k * (1.0 + (a - 1.0) * kaw_ref[...])
    bon_ref[...] = (_dot_sel(r * km * rkw_ref[...], jm) * v).reshape(BB, TB, _RW_WIDTH)

    lw = -jnp.exp(w_log)
    g = _sel_dot(_chunk_tri(N, TB, reverse), lw)
    e0 = 0 if reverse else TB - 1
    g_end = jnp.concatenate([jnp.broadcast_to(g[bb * TB + e0:bb * TB + e0 + 1], (TB, _RW_WIDTH))
                             for bb in range(BB)], axis=0)
    pm = kk * jnp.exp(g - lw)
    qm = (kk * a) * jnp.exp(-g)
    khm = km * jnp.exp(-g)
    rhm = r * jnp.exp(g)
    qgm = (kk * a) * jnp.exp(g_end - g)
    kgm = km * jnp.exp(g_end - g)
    gam = jnp.exp(g_end)

    t_i = lax.broadcasted_iota(jnp.int32, (TB, _LANE), 0)
    lane = lax.broadcasted_iota(jnp.int32, (TB, _LANE), 1)
    s_i = jnp.bitwise_and(lane, _RW_HD - 1)
    m_a = lane < _RW_HD
    strict = (s_i > t_i) if reverse else (s_i < t_i)
    incl = (s_i >= t_i) if reverse else (s_i <= t_i)
    eye = jnp.where(s_i == t_i, 1.0, 0.0)
    same = ((lax.broadcasted_iota(jnp.int32, (_LANE, _LANE), 0) < _RW_HD)
            == (lax.broadcasted_iota(jnp.int32, (_LANE, _LANE), 1) < _RW_HD))

    def bd(x):
        return jnp.concatenate([jnp.where(m_a, x, 0.0), jnp.where(m_a, 0.0, x)], axis=0)

    units = [(bb, p) for bb in range(BB) for p in range(NP)]
    nu = range(len(units))

    def cut(x, un):
        bb, p = un
        return x[bb * TB:(bb + 1) * TB, p * _LANE:(p + 1) * _LANE]

    lo_h, hi_h = slice(0, _LANE), slice(_LANE, 2 * _LANE)
    gram = [_dot3(jnp.concatenate([cut(pm, un), cut(rhm, un)], axis=0),
                  jnp.concatenate([bd(cut(qm, un)), bd(cut(khm, un))], axis=0), nt=True) for un in units]
    l_pq = [jnp.where(strict, x[0:TB, lo_h], 0.0) for x in gram]
    l_pk = [jnp.where(strict, x[0:TB, hi_h], 0.0) for x in gram]
    m_rq = [jnp.where(incl, x[TB:2 * TB, lo_h], 0.0) for x in gram]
    m_rk = [jnp.where(incl, x[TB:2 * TB, hi_h], 0.0) for x in gram]
    wm = [_dot(jnp.concatenate([l_pk[n], m_rk[n]], axis=0).astype(BF16), bd(cut(v, units[n])).astype(BF16))
          for n in nu]
    kv = [_dot_tn(cut(v, un).astype(BF16), cut(kgm, un).astype(BF16)) for un in units]
    s_m = [eye - x for x in l_pq]
    m_m = [_dot3(x, bd(x)) for x in l_pq]
    nlev = TB.bit_length() - 1
    for lev in range(1, nlev):
        if lev < nlev - 1:
            xs = [_dot3(m_m[n], jnp.concatenate([bd(s_m[n]), bd(m_m[n])], axis=1)) for n in nu]
            s_m = [s_m[n] + xs[n][:, lo_h] for n in nu]
            m_m = [x[:, hi_h] for x in xs]
        else:
            s_m = [s_m[n] + _dot3(m_m[n], bd(s_m[n])) for n in nu]
    ta = [_dot3(s_m[n], jnp.concatenate([bd(cut(pm, units[n])), bd(wm[n][0:TB])], axis=1)) for n in nu]
    zt = [ZT[bb, p] for bb, p in units]
    az = [_dot_nt(jnp.concatenate([ta[n][:, lo_h], cut(rhm, units[n])], axis=0).astype(BF16), zt[n].astype(BF16))
          for n in nu]
    u = [az[n][0:TB] + ta[n][:, hi_h] for n in nu]
    mu = [_dot(m_rq[n].astype(BF16), bd(u[n]).astype(BF16)) for n in nu]
    qu = [_dot_tn(u[n].astype(BF16), cut(qgm, units[n]).astype(BF16)) for n in nu]
    for n in nu:
        bb, p = units[n]
        o_ref[bb, :, p * _LANE:(p + 1) * _LANE] = az[n][TB:2 * TB] - mu[n] + wm[n][TB:2 * TB]
        ZT[bb, p] = zt[n] * cut(gam, units[n])[0:1] + jnp.where(same, kv[n] - qu[n], 0.0)

    @pl.when(i == pl.num_programs(1) - 1)
    def _():
        sout_ref[...] = ZT[...]


def _rwkv_call(rkv, lora, s0, wts, col_major, reverse, rows):
    B, T, _ = rkv.shape
    C = _RW_WIDTH
    nblk = T // _CHUNK
    if col_major:
        assert rows == _CHUNK
    rv = _seq_view(rkv, col_major, rows)
    lv = _seq_view(lora, col_major, rows)
    NP = _RW_HEADS // 2
    bb = _scan_bb(B)
    seq = functools.partial(_seq_spec, col_major=col_major, nblk=nblk, reverse=reverse, bb=bb)
    out_shape = rv.shape[:2] + (rv.shape[2] // 3,)
    st_spec = pl.BlockSpec((bb, NP, _LANE, _LANE), lambda b, i: (b, 0, 0, 0))
    o, bon, s_out = pl.pallas_call(
        functools.partial(_rwkv_kernel, reverse=reverse),
        grid=(B // bb, nblk),
        in_specs=[seq(C, 3 * C, 0), seq(C, 3 * C, 1), seq(C, 3 * C, 2), seq(C, C, 0), st_spec,
                  _full_spec((1, C)), _full_spec((2 * _RW_LORA, C)), _full_spec((1, C)),
                  _full_spec((2 * _RW_LORA, C)), _full_spec((1, C)), _full_spec((1, C)), _full_spec((1, C)),
                  _full_spec((C, C))],
        out_specs=[seq(C, C, 0), seq(C, C, 0), st_spec],
        out_shape=[jax.ShapeDtypeStruct(out_shape, F32), jax.ShapeDtypeStruct(out_shape, F32),
                   jax.ShapeDtypeStruct((B, NP, _LANE, _LANE), F32)],
        scratch_shapes=[pltpu.VMEM((bb, NP, _LANE, _LANE), F32)],
        compiler_params=_cp(("arbitrary", "arbitrary"), _VMEM_LIMIT),
        name="rwkv7_scan",
    )(rv, rv, rv, lv, s0, *wts)
    return o.reshape(B, T, C), bon.reshape(B, T, C), s_out


def _gla_kernel(q_ref, k_ref, v_ref, lo_ref, s0_ref, aup_ref, ab_ref, o_ref, sout_ref, ST, *, reverse):
    i = pl.program_id(1)
    TB = _CHUNK

    @pl.when(i == 0)
    def _():
        ST[...] = s0_ref[...]

    BB = q_ref.shape[0]
    N = BB * TB
    q = _silu(q_ref[...].reshape(N, _GLA_PW)) * (_GLA_DK ** -0.5)
    k = _silu(k_ref[...].reshape(N, _GLA_PW))
    v = _silu(v_ref[...].reshape(N, _GLA_PW))
    gad = lo_ref[...].reshape(N, _RW_WIDTH)[:, 2 * _LANE:3 * _LANE]
    log_a = -_softplus(-(_dot3(gad, aup_ref[...]) + ab_ref[...])) / _GLA_TAU
    row = lax.broadcasted_iota(jnp.int32, (TB, TB), 0)
    col = lax.broadcasted_iota(jnp.int32, (TB, TB), 1)
    keep = (row <= col) if reverse else (row >= col)
    g_cum = _sel_dot(_chunk_tri(N, TB, reverse), log_a)
    e0 = 0 if reverse else TB - 1
    g_last = jnp.concatenate([jnp.broadcast_to(g_cum[bb * TB + e0:bb * TB + e0 + 1], (TB, _GLA_PW))
                              for bb in range(BB)], axis=0)
    q_in = q * jnp.exp(g_cum)
    k_in = k * jnp.exp(-g_cum)
    k_tail = k * jnp.exp(g_last - g_cum)
    dec = jnp.exp(g_last)
    units = [(bb, h) for bb in range(BB) for h in range(_GLA_HEADS)]
    nu = range(len(units))

    def cut(x, un):
        bb, h = un
        return x[bb * TB:(bb + 1) * TB, h * _LANE:(h + 1) * _LANE]

    qh = [cut(q_in, un).astype(BF16) for un in units]
    vh = [cut(v, un).astype(BF16) for un in units]
    st = [ST[bb, h] for bb, h in units]
    att = [_dot_nt(qh[n], cut(k_in, units[n]).astype(BF16)) for n in nu]
    o_inter = [_dot_nt(qh[n], st[n].astype(BF16)) for n in nu]
    kv = [_dot_tn(vh[n], cut(k_tail, units[n]).astype(BF16)) for n in nu]
    o_intra = [_dot(jnp.where(keep, att[n], 0.0).astype(BF16), vh[n]) for n in nu]
    for n in nu:
        bb, h = units[n]
        o_ref[bb, :, h * _LANE:(h + 1) * _LANE] = o_intra[n] + o_inter[n]
        ST[bb, h] = st[n] * cut(dec, units[n])[0:1] + kv[n]

    @pl.when(i == pl.num_programs(1) - 1)
    def _():
        sout_ref[...] = ST[...]


def _gla_call(qkv, lora, s0, aup, ab, col_major, reverse, rows):
    B, T, _ = qkv.shape
    C = _GLA_PW
    nblk = T // _CHUNK
    if col_major:
        assert rows == _CHUNK
    qv = _seq_view(qkv, col_major, rows)
    lv = _seq_view(lora, col_major, rows)
    bb = _scan_bb(B)
    seq = functools.partial(_seq_spec, col_major=col_major, nblk=nblk, reverse=reverse, bb=bb)
    out_shape = qv.shape[:2] + (qv.shape[2] // 3,)
    st_spec = pl.BlockSpec((bb, _GLA_HEADS, _LANE, _LANE), lambda b, i: (b, 0, 0, 0))
    o, s_out = pl.pallas_call(
        functools.partial(_gla_kernel, reverse=reverse),
        grid=(B // bb, nblk),
        in_specs=[seq(C, 3 * C, 0), seq(C, 3 * C, 1), seq(C, 3 * C, 2), seq(_RW_WIDTH, _RW_WIDTH, 0), st_spec,
                  _full_spec((_LANE, C)), _full_spec((1, C))],
        out_specs=[seq(C, C, 0), st_spec],
        out_shape=[jax.ShapeDtypeStruct(out_shape, F32),
                   jax.ShapeDtypeStruct((B, _GLA_HEADS, _LANE, _LANE), F32)],
        scratch_shapes=[pltpu.VMEM((bb, _GLA_HEADS, _LANE, _LANE), F32)],
        compiler_params=_cp(("arbitrary", "arbitrary"), _VMEM_LIMIT),
        name="gla_chunked",
    )(qv, qv, qv, lv, s0, aup, ab)
    return o.reshape(B, T, C), s_out


def _fourier_kernel(x_ref, cg_ref, sg_ref, w_ref, o_ref, xcs_ref):
    T = x_ref.shape[1]

    @pl.when(pl.program_id(1) == 0)
    def _():
        x = x_ref[0]
        xcs_ref[0:T, :] = _dot3(x, cg_ref[...]).astype(BF16)
        xcs_ref[T:2 * T, :] = _dot3(x, sg_ref[...]).astype(BF16)

    o_ref[0] = _dot(w_ref[...], xcs_ref[...])


def _fourier_call(zpost, cg, sg, wt):
    B, T, _ = zpost.shape
    C = _FN_WIDTH
    tm = _tile(T, 512, 8)
    return pl.pallas_call(
        _fourier_kernel,
        grid=(B, T // tm),
        in_specs=[pl.BlockSpec((1, T, C), lambda b, i: (b, 0, 0)),
                  _full_spec((C, C)), _full_spec((C, C)),
                  pl.BlockSpec((tm, 2 * T), lambda b, i: (i, 0))],
        out_specs=pl.BlockSpec((1, tm, C), lambda b, i: (b, i, 0)),
        out_shape=jax.ShapeDtypeStruct((B, T, C), F32),
        scratch_shapes=[pltpu.VMEM((2 * T, C), BF16)],
        compiler_params=_cp(("arbitrary", "arbitrary"), _VMEM_LIMIT),
        name="fnet_dft",
    )(zpost, cg, sg, wt)


def _merge_kernel(rwf_ref, rwb_ref, bnf_ref, bnb_ref, glf_ref, glb_ref, zp_ref, fn_ref, h_ref, g1_ref, sc_ref, sh_ref,
                  nw_ref, wg_ref, jr_ref, lnw_ref, lnb_ref, gup_ref, jg_ref, gnw_ref, pa_ref, pb_ref, pc_ref, wo_ref,
                  o_ref):
    D = h_ref.shape[2]
    zp = zp_ref[0]
    og = zp[:, _FN_WIDTH:_FN_WIDTH + _GLA_PW]
    gd = zp[:, _FN_WIDTH + _GLA_PW:_FN_WIDTH + _GLA_PW + _RW_GATE_RANK]
    o = rwf_ref[0] + rwb_ref[0]
    jr = jr_ref[...]
    mu = _dot_sel(o, jr) * (1.0 / _RW_HD)
    xc = o - mu
    var = _dot_sel(xc * xc, jr) * (1.0 / _RW_HD)
    y = xc * lax.rsqrt(var + _RW_GN_EPS)
    y = y * lnw_ref[...] + lnb_ref[...] + (bnf_ref[0] + bnb_ref[0])
    rw_y = y * _dot3(_sigmoid(gd), gup_ref[...])
    g = glf_ref[0] + glb_ref[0]
    ms = _dot_sel(g * g, jg_ref[...]) * (1.0 / _GLA_DV)
    gla_y = g * lax.rsqrt(ms + _NORM_EPS) * gnw_ref[...] * _silu(og)
    h = h_ref[0]
    u = _modnorm(h, nw_ref[...], sc_ref[0], sh_ref[0]).astype(BF16)
    gates = _sigmoid(_dot(u, wg_ref[...]))
    m = (gates[:, 0:D] * _dot(rw_y.astype(BF16), pa_ref[...])
         + gates[:, D:2 * D] * _dot(gla_y.astype(BF16), pb_ref[...])
         + gates[:, 2 * D:3 * D] * _dot(fn_ref[0].astype(BF16), pc_ref[...]))
    o_ref[0] = h + g1_ref[0] * _dot(m.astype(BF16), wo_ref[...])


def _merge_call(rwf, rwb, bnf, bnb, glf, glb, zpost, fn, h, g1, sc, sh, wts):
    B, T, D = h.shape
    tm = _tile(T, 512, 8)

    def row(c):
        return pl.BlockSpec((1, tm, c), lambda b, i: (b, i, 0))

    acts = [rwf, rwb, bnf, bnb, glf, glb, zpost, fn, h]
    mods = [g1, sc, sh]
    return pl.pallas_call(
        _merge_kernel,
        grid=(B, T // tm),
        in_specs=[row(a.shape[2]) for a in acts] + [pl.BlockSpec((1, 1, D), lambda b, i: (b, 0, 0))] * len(mods)
        + [_full_spec(w.shape) for w in wts],
        out_specs=row(D),
        out_shape=jax.ShapeDtypeStruct((B, T, D), F32),
        compiler_params=_cp(("arbitrary", "arbitrary"), _VMEM_LIMIT),
        name="branch_merge",
    )(*acts, *mods, *wts)


def _router_kernel(h_ref, nw_ref, sc_ref, sh_ref, rt_ref, v_ref, aff_ref):
    u = _modnorm(h_ref[0], nw_ref[...], sc_ref[0], sh_ref[0])
    v_ref[0] = u.astype(BF16)
    logits = _dot_nt(rt_ref[...], u, HI)
    e = jnp.exp(logits - jnp.max(logits, axis=0, keepdims=True))
    aff_ref[0] = e / jnp.sum(e, axis=0, keepdims=True)


def _router_call(h, nw, sc, sh, router_t):
    B, T, D = h.shape
    E = router_t.shape[0]
    tm = _tile(T, 512, 128)
    return pl.pallas_call(
        _router_kernel,
        grid=(B, T // tm),
        in_specs=[pl.BlockSpec((1, tm, D), lambda b, i: (b, i, 0)), _full_spec((1, D)),
                  pl.BlockSpec((1, 1, D), lambda b, i: (b, 0, 0)), pl.BlockSpec((1, 1, D), lambda b, i: (b, 0, 0)),
                  _full_spec((E, D))],
        out_specs=[pl.BlockSpec((1, tm, D), lambda b, i: (b, i, 0)), pl.BlockSpec((1, E, tm), lambda b, i: (b, 0, i))],
        out_shape=[jax.ShapeDtypeStruct((B, T, D), BF16), jax.ShapeDtypeStruct((B, E, T), F32)],
        compiler_params=_cp(("arbitrary", "arbitrary"), _VMEM_LIMIT),
        name="router_softmax",
    )(h, nw, sc, sh, router_t)


def _topc_kernel(aff_ref, pos_ref, cum_ref, *, cap):
    E, T = aff_ref.shape[1], aff_ref.shape[2]
    x = pltpu.bitcast(aff_ref[0], jnp.int32)

    def body(it, thr):
        cand = thr | lax.shift_left(jnp.int32(1), 30 - it)
        cnt = jnp.sum(jnp.where(x >= cand, 1, 0), axis=1, keepdims=True)
        return jnp.where(cnt >= cap, cand, thr)

    thr = lax.fori_loop(0, 31, body, jnp.zeros((E, 1), jnp.int32))
    need = (cap - jnp.sum(jnp.where(x > thr, 1, 0), axis=1, keepdims=True)).astype(F32)
    blk = _LANE
    upper = jnp.where(lax.broadcasted_iota(jnp.int32, (blk, blk), 0) < lax.broadcasted_iota(jnp.int32, (blk, blk), 1),
                      1.0, 0.0).astype(BF16)
    off_eq = jnp.zeros((E, 1), F32)
    off_sel = jnp.zeros((E, 1), F32)
    for c in range(T // blk):
        sl = slice(c * blk, (c + 1) * blk)
        xc = pltpu.bitcast(aff_ref[0, :, sl], jnp.int32)
        eq_c = jnp.where(xc == thr, 1.0, 0.0)
        rank_eq = _dot(eq_c.astype(BF16), upper) + off_eq
        take = jnp.where(rank_eq < need, eq_c, 0.0)
        sel = jnp.where(xc > thr, 1.0, take)
        rank = _dot(sel.astype(BF16), upper) + off_sel
        pos_ref[0, :, sl] = jnp.where(sel > 0.0, rank.astype(jnp.int32), -1)
        off_eq = off_eq + jnp.sum(eq_c, axis=1, keepdims=True)
        off_sel = off_sel + jnp.sum(sel, axis=1, keepdims=True)
    chosen = jnp.where(pos_ref[0] >= 0, 1.0, 0.0).astype(BF16)
    before = jnp.where(lax.shift_right_logical(lax.broadcasted_iota(jnp.int32, (T, blk), 0), blk.bit_length() - 1)
                       < lax.broadcasted_iota(jnp.int32, (T, blk), 1), 1.0, 0.0).astype(BF16)
    cum_ref[0] = _dot(chosen, before).astype(jnp.int32)


def _topc_call(aff, cap):
    B, E, T = aff.shape
    assert T // _LANE < _LANE
    pos, cum = pl.pallas_call(
        functools.partial(_topc_kernel, cap=cap),
        grid=(B,),
        in_specs=[pl.BlockSpec((1, E, T), lambda b: (b, 0, 0))],
        out_specs=[pl.BlockSpec((1, E, T), lambda b: (b, 0, 0)), pl.BlockSpec((1, E, _LANE), lambda b: (b, 0, 0))],
        out_shape=[jax.ShapeDtypeStruct((B, E, T), jnp.int32), jax.ShapeDtypeStruct((B, E, _LANE), jnp.int32)],
        compiler_params=_cp(("arbitrary",), _VMEM_LIMIT),
        name="expert_choice_topc",
    )(aff)
    return pos, cum[:, :, :T // _LANE + 1].reshape(-1)


def _slot_block(cap):
    return min(cap, _LANE)


def _ffn_kernel(cum_ref, v_ref, pos_ref, aff_ref, w1_ref, w3_ref, w2_ref, ys_ref, xs_ref, acc_ref, gate_ref, *, cap, tc,
                expert_axis):
    e = pl.program_id(expert_axis)
    b = pl.program_id(1 - expert_axis)
    f = pl.program_id(2)
    T = v_ref.shape[1]
    E = pos_ref.shape[1]
    sbz = _slot_block(cap)

    @pl.when(f == 0)
    def _():
        acc_ref[...] = jnp.zeros(acc_ref.shape, F32)
        gate_ref[...] = jnp.zeros(gate_ref.shape, F32)
        base = (b * E + e) * (T // _LANE + 1)
        for c in range(T // tc):
            sl = slice(c * tc, (c + 1) * tc)
            lo = cum_ref[base + c * (tc // _LANE)]
            hi = cum_ref[base + (c + 1) * (tc // _LANE)]
            for sb in range(cap // sbz):
                rows = slice(sb * sbz, (sb + 1) * sbz)

                @pl.when(jnp.logical_and(lo < (sb + 1) * sbz, hi > sb * sbz))
                def _():
                    slot = lax.broadcasted_iota(jnp.int32, (sbz, tc), 0) + sb * sbz
                    hit = slot == pos_ref[0, e, :, sl]
                    acc_ref[rows, :] += _dot(jnp.where(hit, 1.0, 0.0).astype(BF16), v_ref[0, sl, :])
                    gate_ref[rows, :] += jnp.sum(jnp.where(hit, aff_ref[0, e, :, sl], 0.0), axis=1, keepdims=True)
        xs_ref[...] = acc_ref[...].astype(BF16)
        acc_ref[...] = jnp.zeros(acc_ref.shape, F32)

    x = xs_ref[...]
    h1 = _dot(x, w1_ref[0])
    hid = _silu(h1) * _dot(x, w3_ref[0])
    acc_ref[...] += _dot(hid.astype(BF16), w2_ref[0])

    @pl.when(f == pl.num_programs(2) - 1)
    def _():
        ys_ref[0, 0] = (acc_ref[...] * gate_ref[...]).astype(BF16)


def _ffn_call(cum, v, pos, aff, w1, w3, w2, cap):
    B, T, D = v.shape
    E, _, F = w1.shape
    tc = _tile(T, 512)
    expert_major = T <= 512
    fc = F if expert_major else _tile(F, 1024)
    if expert_major:
        grid = (E, B, 1)

        def ix(f):
            return lambda e, b, j: f(b, e, j)
    else:
        grid = (B, E, F // fc)

        def ix(f):
            return f
    return pl.pallas_call(
        functools.partial(_ffn_kernel, cap=cap, tc=tc, expert_axis=0 if expert_major else 1),
        grid=grid,
        in_specs=[pl.BlockSpec(memory_space=pltpu.SMEM),
                  pl.BlockSpec((1, T, D), ix(lambda b, e, f: (b, 0, 0))),
                  pl.BlockSpec((1, E, 1, T), ix(lambda b, e, f: (b, 0, 0, 0))),
                  pl.BlockSpec((1, E, 1, T), ix(lambda b, e, f: (b, 0, 0, 0))),
                  pl.BlockSpec((1, D, fc), ix(lambda b, e, f: (e, 0, f))),
                  pl.BlockSpec((1, D, fc), ix(lambda b, e, f: (e, 0, f))),
                  pl.BlockSpec((1, fc, D), ix(lambda b, e, f: (e, f, 0)))],
        out_specs=pl.BlockSpec((1, 1, cap, D), ix(lambda b, e, f: (b, e, 0, 0))),
        out_shape=jax.ShapeDtypeStruct((B, E, cap, D), BF16),
        scratch_shapes=[pltpu.VMEM((cap, D), BF16), pltpu.VMEM((cap, D), F32), pltpu.VMEM((cap, 1), F32)],
        compiler_params=_cp(("arbitrary", "arbitrary", "arbitrary"), _VMEM_LIMIT),
        name="expert_ffn",
    )(cum, v, pos, aff, w1, w3, w2)


def _scatter_kernel(pos_ref, ys_ref, h_ref, g2_ref, o_ref, *, cap):
    tm = h_ref.shape[1]
    E = ys_ref.shape[1]
    pos_t = pos_ref[0]
    slot = lax.broadcasted_iota(jnp.int32, (tm, cap), 1)
    acc = None
    for e in range(E):
        hit = jnp.where(slot == pos_t[:, e:e + 1], 1.0, 0.0).astype(BF16)
        part = _dot(hit, ys_ref[0, e])
        acc = part if acc is None else acc + part
    o_ref[0] = h_ref[0] + g2_ref[0] * acc


def _scatter_call(pos_t, ys, h, g2, cap):
    B, T, D = h.shape
    E = ys.shape[1]
    tm = _tile(T, 512)
    return pl.pallas_call(
        functools.partial(_scatter_kernel, cap=cap),
        grid=(B, T // tm),
        in_specs=[pl.BlockSpec((1, tm, E), lambda b, i: (b, i, 0)),
                  pl.BlockSpec((1, E, cap, D), lambda b, i: (b, 0, 0, 0)),
                  pl.BlockSpec((1, tm, D), lambda b, i: (b, i, 0)),
                  pl.BlockSpec((1, 1, D), lambda b, i: (b, 0, 0))],
        out_specs=pl.BlockSpec((1, tm, D), lambda b, i: (b, i, 0)),
        out_shape=jax.ShapeDtypeStruct((B, T, D), F32),
        compiler_params=_cp(("arbitrary", "arbitrary"), _VMEM_LIMIT),
        name="expert_scatter",
    )(pos_t, ys, h, g2)


def _final_kernel(h_ref, w_ref, o_ref):
    x = h_ref[0]
    o_ref[0] = x * lax.rsqrt(jnp.mean(x * x, axis=-1, keepdims=True) + _NORM_EPS) * w_ref[...]


def _final_call(h, w):
    B, T, D = h.shape
    tm = _tile(T, 1024, 8)
    return pl.pallas_call(
        _final_kernel,
        grid=(B, T // tm),
        in_specs=[pl.BlockSpec((1, tm, D), lambda b, i: (b, i, 0)), _full_spec((1, D))],
        out_specs=pl.BlockSpec((1, tm, D), lambda b, i: (b, i, 0)),
        out_shape=jax.ShapeDtypeStruct((B, T, D), F32),
        compiler_params=_cp(("arbitrary", "arbitrary"), _VMEM_LIMIT),
        name="final_norm",
    )(h, w)


def _pad_heads(x, nh, d, axis=-1):
    axis = axis % x.ndim
    shp = x.shape
    x = x.reshape(shp[:axis] + (nh, d) + shp[axis + 1:])
    pad = [(0, 0)] * x.ndim
    pad[axis + 1] = (0, _LANE - d)
    x = jnp.pad(x, pad)
    return x.reshape(shp[:axis] + (nh * _LANE,) + shp[axis + 1:])


def _pad_to(x, n, axis=-1):
    axis = axis % x.ndim
    pad = [(0, 0)] * x.ndim
    pad[axis] = (0, n - x.shape[axis])
    return jnp.pad(x, pad)


def _block_ones(n, blk):
    i = np.arange(n) // blk
    return jnp.asarray((i[:, None] == i[None, :]).astype(np.float32))


def _dft_consts(T):
    s = 1 << ((T.bit_length() - 1) // 2)
    i = np.arange(T)[:, None]
    a1 = 2.0 * np.pi * ((i * s * np.arange(T // s)[None, :]) % T) / T
    a2 = 2.0 * np.pi * ((i * np.arange(s)[None, :]) % T) / T
    c1, s1, c2, s2 = (jnp.asarray(f(a) / math.sqrt(math.sqrt(T)), F32)
                      for a, f in ((a1, np.cos), (a1, np.sin), (a2, np.cos), (a2, np.sin)))
    cos = (c1[:, :, None] * c2[:, None, :] - s1[:, :, None] * s2[:, None, :]).reshape(T, T)
    sin = (s1[:, :, None] * c2[:, None, :] + c1[:, :, None] * s2[:, None, :]).reshape(T, T)
    wt = jnp.concatenate([cos, -sin], axis=1)
    g = np.arange(_FN_GD)
    ang_g = 2.0 * np.pi * ((g[:, None] * g[None, :]) % _FN_GD) / _FN_GD
    eye = np.eye(_FN_GROUPS)
    cg = np.kron(eye, np.cos(ang_g)) / math.sqrt(_FN_GD)
    sg = np.kron(eye, np.sin(ang_g)) / math.sqrt(_FN_GD)
    return jnp.asarray(wt, BF16), jnp.asarray(cg, F32), jnp.asarray(sg, F32)


def _layer_weights(i, w_in, conv_w, rw_w0, rw_w_up, rw_a0, rw_a_up, rw_k_k, rw_k_a, rw_r_k, rw_g_up, rw_ln_w,
                   rw_ln_b, gla_a_up, gla_a_b, gla_norm_w, proj_a, proj_b, proj_c, w_out):
    D = w_in.shape[1]
    rw, kw, vw = _RW_WIDTH, _GLA_HEADS * _GLA_DK, _GLA_HEADS * _GLA_DV
    o_gq = 3 * rw
    o_gk = o_gq + kw
    o_gv = o_gk + kw
    o_wd = o_gv + vw
    o_ad = o_wd + 2 * _RW_LORA
    o_ga = o_ad + 2 * _RW_LORA
    o_gd = o_ga + 2 * _GLA_RANK
    o_og = o_gd + _RW_GATE_RANK
    o_fn = o_og + vw
    o_gt = o_fn + _FN_WIDTH
    wi = w_in[i]
    cw = conv_w[i].reshape(9, -1)

    def gla_cols(x):
        return jnp.concatenate([_pad_heads(x[..., o_gq:o_gk], _GLA_HEADS, _GLA_DK),
                                _pad_heads(x[..., o_gk:o_gv], _GLA_HEADS, _GLA_DK),
                                _pad_heads(x[..., o_gv:o_wd], _GLA_HEADS, _GLA_DV)], axis=-1)

    w = {}
    w['in_rw'] = wi[:, 0:o_gq].astype(BF16)
    w['in_gla'] = gla_cols(wi).astype(BF16)
    w['in_lora'] = jnp.concatenate([wi[:, o_wd:o_ga], _pad_to(wi[:, o_ga:o_gd], _LANE)], axis=-1).astype(BF16)
    w['in_post'] = jnp.concatenate([wi[:, o_fn:o_gt], _pad_heads(wi[:, o_og:o_fn], _GLA_HEADS, _GLA_DV),
                                    wi[:, o_gd:o_og]], axis=-1).astype(BF16)
    w['in_gates'] = wi[:, o_gt:].astype(BF16)
    w['conv_rw'] = cw[:, 0:o_gq]
    w['conv_gla'] = gla_cols(cw)
    zl = jnp.zeros((_RW_LORA, rw), F32)
    w['rwkv'] = []
    w['gla'] = []
    jr = _block_ones(rw, _RW_HD)
    for d in range(2):
        wup = jnp.concatenate([rw_w_up[i, d], zl] if d == 0 else [zl, rw_w_up[i, d]], axis=0)
        aup = jnp.concatenate([rw_a_up[i, d], zl] if d == 0 else [zl, rw_a_up[i, d]], axis=0)
        w['rwkv'].append((rw_w0[i, d][None], wup, rw_a0[i, d][None], aup, rw_k_k[i][None], rw_k_a[i][None],
                          rw_r_k[i].reshape(1, rw), jr))
        ga = _pad_heads(gla_a_up[i, d], _GLA_HEADS, _GLA_DK)
        ga = jnp.pad(ga, ((d * _GLA_RANK, _LANE - (d + 1) * _GLA_RANK), (0, 0)))
        w['gla'].append((ga, _pad_heads(gla_a_b[i, d][None], _GLA_HEADS, _GLA_DK)))
    w['merge'] = (jr, rw_ln_w[i][None], rw_ln_b[i][None], rw_g_up[i], _block_ones(_GLA_PW, _LANE),
                  _pad_heads(jnp.tile(gla_norm_w[i], _GLA_HEADS)[None], _GLA_HEADS, _GLA_DV),
                  proj_a[i].astype(BF16), _pad_heads(proj_b[i], _GLA_HEADS, _GLA_DV, axis=0).astype(BF16),
                  proj_c[i].astype(BF16), w_out[i].astype(BF16))
    return w


def _moe(h, nw, sc, sh, g2, router_t, w1, w3, w2):
    B, T, D = h.shape
    E = router_t.shape[0]
    cap = 2 * T // E
    v, aff = _router_call(h, nw, sc, sh, router_t)
    pos, cum = _topc_call(aff, cap)
    ys = _ffn_call(cum, v, pos.reshape(B, E, 1, T), aff.reshape(B, E, 1, T), w1, w3, w2, cap)
    return _scatter_call(jnp.swapaxes(pos, 1, 2), ys, h, g2, cap)


def kernel(x, c, ctx, c_ctx, w_mod, b_mod, norm1_w, norm2_w, w_in, conv_w, rw_w0, rw_w_up, rw_a0, rw_a_up, rw_k_k, rw_k_a, rw_r_k, rw_g_up, rw_ln_w, rw_ln_b, gla_a_up, gla_a_b, gla_norm_w, proj_a, proj_b, proj_c, w_out, router, exp_w1, exp_w3, exp_w2, final_norm_w):
    B, S, D = x.shape
    TC = ctx.shape[1]
    L = w_mod.shape[0]
    W = _GRID_W
    rows = S // W
    assert S % _CHUNK == 0 and TC % _CHUNK == 0

    cc = _pad_to(jnp.concatenate([c, c_ctx[None]], axis=0), 16, axis=0)
    mods = _mod_call(cc, w_mod, b_mod)
    dft = {S: _dft_consts(S), TC: _dft_consts(TC)}
    s0_rw = jnp.zeros((B, _RW_HEADS // 2, _LANE, _LANE), F32)
    s0_gla = jnp.zeros((B, _GLA_HEADS, _LANE, _LANE), F32)

    h_lat, h_ctx = x, ctx
    for i in range(L):
        col_major = i % 2 == 1
        need_ctx = i < L - 1
        lw = _layer_weights(i, w_in, conv_w, rw_w0, rw_w_up, rw_a0, rw_a_up, rw_k_k, rw_k_a, rw_r_k, rw_g_up,
                            rw_ln_w, rw_ln_b, gla_a_up, gla_a_b, gla_norm_w, proj_a, proj_b, proj_c, w_out)
        m_lat = mods[i, :B].reshape(B, 1, _N_MOD, D)
        m_ctx = jnp.broadcast_to(mods[i, B:B + 1].reshape(1, 1, _N_MOD, D), (B, 1, _N_MOD, D))
        n1 = norm1_w[i][None]
        n2 = norm2_w[i][None]

        def in_proj(h, m, img_rows, img_w):
            z = {k: _norm_matmul(h, n1, m[:, :, 1], m[:, :, 0], lw['in_' + k])
                 for k in ('rw', 'gla', 'lora', 'post')}
            z['rw'] = _conv_call(z['rw'], lw['conv_rw'], img_rows, img_w)
            z['gla'] = _conv_call(z['gla'], lw['conv_gla'], img_rows, img_w)
            return z

        z_lat = in_proj(h_lat, m_lat, rows, W)
        z_ctx = in_proj(h_ctx, m_ctx, 1, TC)

        rw_lat, rw_ctx, gl_lat, gl_ctx = [], [], [], []
        for d in range(2):
            rev = d == 1
            o_c, b_c, st = _rwkv_call(z_ctx['rw'], z_ctx['lora'], s0_rw, lw['rwkv'][d], False, rev, 1)
            o_l, b_l, _ = _rwkv_call(z_lat['rw'], z_lat['lora'], st, lw['rwkv'][d], col_major, rev, rows)
            rw_ctx.append((o_c, b_c))
            rw_lat.append((o_l, b_l))
            g_c, st = _gla_call(z_ctx['gla'], z_ctx['lora'], s0_gla, *lw['gla'][d], False, rev, 1)
            g_l, _ = _gla_call(z_lat['gla'], z_lat['lora'], st, *lw['gla'][d], col_major, rev, rows)
            gl_ctx.append(g_c)
            gl_lat.append(g_l)

        def mix(h, m, z, rwo, glo, T):
            wt, cg, sg = dft[T]
            fn = _fourier_call(z['post'], cg, sg, wt)
            return _merge_call(rwo[0][0], rwo[1][0], rwo[0][1], rwo[1][1], glo[0], glo[1], z['post'], fn, h,
                               m[:, :, 2], m[:, :, 1], m[:, :, 0], (n1, lw['in_gates']) + lw['merge'])

        router_t = router[i].T
        w1 = exp_w1[i].astype(BF16)
        w3 = exp_w3[i].astype(BF16)
        w2 = exp_w2[i].astype(BF16)
        h_lat = mix(h_lat, m_lat, z_lat, rw_lat, gl_lat, S)
        h_lat = _moe(h_lat, n2, m_lat[:, :, 4], m_lat[:, :, 3], m_lat[:, :, 5], router_t, w1, w3, w2)
        if need_ctx:
            h_ctx = mix(h_ctx, m_ctx, z_ctx, rw_ctx, gl_ctx, TC)
            h_ctx = _moe(h_ctx, n2, m_ctx[:, :, 4], m_ctx[:, :, 3], m_ctx[:, :, 5], router_t, w1, w3, w2)
    return _final_call(h_lat, final_norm_w[None])
```

```python
import functools
import math

import jax
import jax.numpy as jnp
import numpy as np
from jax import lax
from jax.experimental import pallas as pl
from jax.experimental.pallas import tpu as pltpu

F32 = jnp.float32
BF16 = jnp.bfloat16
HI = lax.Precision.HIGHEST

_GRID_W = 64
_NORM_EPS = 1e-6
_N_MOD = 6
_RW_HEADS = 6
_RW_HD = 64
_RW_WIDTH = _RW_HEADS * _RW_HD
_RW_LORA = 64
_RW_GATE_RANK = 128
_RW_GN_EPS = 64e-5
_GLA_HEADS = 4
_GLA_DK = 48
_GLA_DV = 96
_GLA_RANK = 16
_GLA_TAU = 16.0
_CHUNK = 64
_FN_GROUPS = 4
_FN_GD = 64
_FN_WIDTH = _FN_GROUPS * _FN_GD
_LANE = 128
_GLA_PW = _GLA_HEADS * _LANE
_VMEM_LIMIT = 56 * 1024 * 1024


def _cp(sem, vmem=None):
    return pltpu.CompilerParams(dimension_semantics=sem, vmem_limit_bytes=vmem)


def _tile(n, cap, mult=128):
    if n <= cap:
        return n
    best = None
    for t in range(mult, cap + 1, mult):
        if n % t == 0:
            best = t
    assert best is not None, (n, cap)
    return best


def _sigmoid(x):
    return 1.0 / (1.0 + jnp.exp(-x))


def _silu(x):
    return x * _sigmoid(x)


def _softplus(x):
    return jnp.maximum(x, 0.0) + jnp.log(1.0 + jnp.exp(-jnp.abs(x)))


def _dot(a, b, precision=None):
    return jnp.dot(a, b, preferred_element_type=F32, precision=precision)


def _dot_nt(a, b, precision=None):
    return lax.dot_general(a, b, (((1,), (1,)), ((), ())), preferred_element_type=F32, precision=precision)


def _dot_tn(a, b, precision=None):
    return lax.dot_general(a, b, (((0,), (0,)), ((), ())), preferred_element_type=F32, precision=precision)


def _mod_kernel(c_ref, w_ref, b_ref, o_ref):
    o_ref[0] = _dot(_silu(c_ref[...]), w_ref[0], HI) + b_ref[0]


def _mod_call(cc, w_mod, b_mod):
    L, D, N = w_mod.shape
    R = cc.shape[0]
    tn = _tile(N, 1536)
    return pl.pallas_call(
        _mod_kernel,
        grid=(L, N // tn),
        in_specs=[pl.BlockSpec((R, D), lambda l, j: (0, 0)),
                  pl.BlockSpec((1, D, tn), lambda l, j: (l, 0, j)),
                  pl.BlockSpec((1, 1, tn), lambda l, j: (l, 0, j))],
        out_specs=pl.BlockSpec((1, R, tn), lambda l, j: (l, 0, j)),
        out_shape=jax.ShapeDtypeStruct((L, R, N), F32),
        compiler_params=_cp(("arbitrary", "arbitrary"), _VMEM_LIMIT),
        name="adaln_mod",
    )(cc, w_mod, b_mod.reshape(L, 1, N))


def _modnorm(x, nw, sc, sh):
    y = x * lax.rsqrt(jnp.mean(x * x, axis=-1, keepdims=True) + _NORM_EPS)
    return (y * nw) * (1.0 + sc) + sh


def _nm_kernel(h_ref, nw_ref, sc_ref, sh_ref, w_ref, o_ref, u_ref):
    @pl.when(pl.program_id(2) == 0)
    def _():
        u_ref[...] = _modnorm(h_ref[0], nw_ref[...], sc_ref[0], sh_ref[0]).astype(BF16)

    o_ref[0] = _dot(u_ref[...], w_ref[...])


def _norm_matmul(h, nw, sc, sh, w):
    B, T, D = h.shape
    N = w.shape[1]
    tm = _tile(T, 1024, 8)
    tn = _tile(N, 1536)
    return pl.pallas_call(
        _nm_kernel,
        grid=(B, T // tm, N // tn),
        in_specs=[pl.BlockSpec((1, tm, D), lambda b, i, j: (b, i, 0)),
                  pl.BlockSpec((1, D), lambda b, i, j: (0, 0)),
                  pl.BlockSpec((1, 1, D), lambda b, i, j: (b, 0, 0)),
                  pl.BlockSpec((1, 1, D), lambda b, i, j: (b, 0, 0)),
                  pl.BlockSpec((D, tn), lambda b, i, j: (0, j))],
        out_specs=pl.BlockSpec((1, tm, tn), lambda b, i, j: (b, i, j)),
        out_shape=jax.ShapeDtypeStruct((B, T, N), F32),
        scratch_shapes=[pltpu.VMEM((tm, D), BF16)],
        compiler_params=_cp(("arbitrary", "arbitrary", "arbitrary"), _VMEM_LIMIT),
        name="norm_in_proj",
    )(h, nw, sc, sh, w)


def _conv_kernel(x_ref, w_ref, o_ref, xp_ref, *, rows, W, T, PAD, CH):
    cw = x_ref.shape[2]
    xp_ref[0:PAD, :] = jnp.zeros((PAD, cw), F32)
    xp_ref[PAD + T:PAD + T + PAD, :] = jnp.zeros((PAD, cw), F32)
    xp_ref[PAD:PAD + T, :] = x_ref[0]
    for c0 in range(0, T, CH):
        col = jnp.bitwise_and(lax.broadcasted_iota(jnp.int32, (CH, cw), 0) + c0, W - 1)
        acc = jnp.zeros((CH, cw), F32)
        for a in range(3):
            if rows == 1 and a != 1:
                continue
            for b in range(3):
                off = (a - 1) * W + (b - 1)
                xs = xp_ref[PAD + c0 + off:PAD + c0 + off + CH, :]
                if b == 0:
                    xs = jnp.where(col >= 1, xs, 0.0)
                elif b == 2:
                    xs = jnp.where(col <= W - 2, xs, 0.0)
                acc = acc + xs * w_ref[a * 3 + b:a * 3 + b + 1, :]
        o_ref[0, c0:c0 + CH, :] = acc


def _conv_call(z, w9, rows, W):
    B, T, C = z.shape
    assert W & (W - 1) == 0 and rows * W == T
    cw = _LANE
    PAD = W + 8 if rows > 1 else 8
    CH = min(T, 128)
    kern = functools.partial(_conv_kernel, rows=rows, W=W, T=T, PAD=PAD, CH=CH)
    return pl.pallas_call(
        kern,
        grid=(B, C // cw),
        in_specs=[pl.BlockSpec((1, T, cw), lambda b, j: (b, 0, j)),
                  pl.BlockSpec((9, cw), lambda b, j: (0, j))],
        out_specs=pl.BlockSpec((1, T, cw), lambda b, j: (b, 0, j)),
        out_shape=jax.ShapeDtypeStruct((B, T, C), F32),
        scratch_shapes=[pltpu.VMEM((T + 2 * PAD, cw), F32)],
        compiler_params=_cp(("arbitrary", "arbitrary"), _VMEM_LIMIT),
        name="short_conv",
    )(z, w9)


def _seq_view(arr, col_major, rows):
    B, T, C = arr.shape
    return arr.reshape(B, rows, (T // rows) * C) if col_major else arr


def _seq_spec(C, ctot, part, col_major, nblk, reverse, bb):
    nper = ctot // C

    def blk(i):
        return nblk - 1 - i if reverse else i

    if col_major:
        return pl.BlockSpec((bb, _CHUNK, C), lambda b, i: (b, 0, blk(i) * nper + part))
    return pl.BlockSpec((bb, _CHUNK, C), lambda b, i: (b, blk(i), part))


def _scan_bb(B):
    return 4 if B % 4 == 0 else (2 if B % 2 == 0 else 1)


def _chunk_tri(n, tb, reverse):
    row = lax.broadcasted_iota(jnp.int32, (n, n), 0)
    col = lax.broadcasted_iota(jnp.int32, (n, n), 1)
    sh = tb.bit_length() - 1
    same = lax.shift_right_logical(row, sh) == lax.shift_right_logical(col, sh)
    order = (row <= col) if reverse else (row >= col)
    return jnp.where(jnp.logical_and(same, order), 1.0, 0.0)


def _full_spec(shape):
    nd = len(shape)
    return pl.BlockSpec(shape, lambda b, i: (0,) * nd)


def _split(x):
    hi = x.astype(BF16)
    return hi, (x - hi.astype(F32)).astype(BF16)


def _dot3(a, b, nt=False):
    f = _dot_nt if nt else _dot
    ah, al = _split(a)
    bh, bl = _split(b)
    cross = f(jnp.concatenate([ah, al], axis=1), jnp.concatenate([bl, bh], axis=1 if nt else 0))
    return f(ah, bh) + cross


def _dot_sel(a, sel):
    sel = sel.astype(BF16)
    a1 = a.astype(BF16)
    r1 = a - a1.astype(F32)
    a2 = r1.astype(BF16)
    a3 = (r1 - a2.astype(F32)).astype(BF16)
    return _dot(a1, sel) + (_dot(a2, sel) + _dot(a3, sel))


def _sel_dot(sel, b):
    sel = sel.astype(BF16)
    b1 = b.astype(BF16)
    r1 = b - b1.astype(F32)
    b2 = r1.astype(BF16)
    b3 = (r1 - b2.astype(F32)).astype(BF16)
    return _dot(sel, b1) + (_dot(sel, b2) + _dot(sel, b3))


def _rwkv_kernel(r_ref, k_ref, v_ref, lo_ref, s0_ref, w0_ref, wup_ref, a0_ref, aup_ref, kkw_ref, kaw_ref,
                 rkw_ref, j_ref, o_ref, bon_ref, sout_ref, ZT, *, reverse):
    i = pl.program_id(1)
    TB = _CHUNK
    NP = _RW_HEADS // 2
    BB = r_ref.shape[0]
    N = BB * TB

    @pl.when(i == 0)
    def _():
        ZT[...] = s0_ref[...]

    r = r_ref[...].reshape(N, _RW_WIDTH)
    k = k_ref[...].reshape(N, _RW_WIDTH)
    v = v_ref[...].reshape(N, _RW_WIDTH)
    lo = lo_ref[...].reshape(N, _RW_WIDTH)
    jm = j_ref[...]
    w_log = -_softplus(-(w0_ref[...] + _dot3(jnp.tanh(lo[:, 0:_LANE]), wup_ref[...]))) - 0.5
    a = _sigmoid(a0_ref[...] + _dot3(lo[:, _LANE:2 * _LANE], aup_ref[...]))
    kk0 = k * kkw_ref[...]
    kk = kk0 / jnp.maximum(jnp.sqrt(_dot_sel(kk0 * kk0, jm)), 1e-12)
    km = k * (1.0 + (a - 1.0) * kaw_ref[...])
    bon_ref[...] = (_dot_sel(r * km * rkw_ref[...], jm) * v).reshape(BB, TB, _RW_WIDTH)

    lw = -jnp.exp(w_log)
    g = _sel_dot(_chunk_tri(N, TB, reverse), lw)
    e0 = 0 if reverse else TB - 1
    g_end = jnp.concatenate([jnp.broadcast_to(g[bb * TB + e0:bb * TB + e0 + 1], (TB, _RW_WIDTH))
                             for bb in range(BB)], axis=0)
    pm = kk * jnp.exp(g - lw)
    qm = (kk * a) * jnp.exp(-g)
    khm = km * jnp.exp(-g)
    rhm = r * jnp.exp(g)
    qgm = (kk * a) * jnp.exp(g_end - g)
    kgm = km * jnp.exp(g_end - g)
    gam = jnp.exp(g_end)

    t_i = lax.broadcasted_iota(jnp.int32, (TB, _LANE), 0)
    lane = lax.broadcasted_iota(jnp.int32, (TB, _LANE), 1)
    s_i = jnp.bitwise_and(lane, _RW_HD - 1)
    m_a = lane < _RW_HD
    strict = (s_i > t_i) if reverse else (s_i < t_i)
    incl = (s_i >= t_i) if reverse else (s_i <= t_i)
    eye = jnp.where(s_i == t_i, 1.0, 0.0)
    same = ((lax.broadcasted_iota(jnp.int32, (_LANE, _LANE), 0) < _RW_HD)
            == (lax.broadcasted_iota(jnp.int32, (_LANE, _LANE), 1) < _RW_HD))

    def bd(x):
        return jnp.concatenate([jnp.where(m_a, x, 0.0), jnp.where(m_a, 0.0, x)], axis=0)

    units = [(bb, p) for bb in range(BB) for p in range(NP)]
    nu = range(len(units))

    def cut(x, un):
        bb, p = un
        return x[bb * TB:(bb + 1) * TB, p * _LANE:(p + 1) * _LANE]

    lo_h, hi_h = slice(0, _LANE), slice(_LANE, 2 * _LANE)
    gram = [_dot3(jnp.concatenate([cut(pm, un), cut(rhm, un)], axis=0),
                  jnp.concatenate([bd(cut(qm, un)), bd(cut(khm, un))], axis=0), nt=True) for un in units]
    l_pq = [jnp.where(strict, x[0:TB, lo_h], 0.0) for x in gram]
    l_pk = [jnp.where(strict, x[0:TB, hi_h], 0.0) for x in gram]
    m_rq = [jnp.where(incl, x[TB:2 * TB, lo_h], 0.0) for x in gram]
    m_rk = [jnp.where(incl, x[TB:2 * TB, hi_h], 0.0) for x in gram]
    wm = [_dot(jnp.concatenate([l_pk[n], m_rk[n]], axis=0).astype(BF16), bd(cut(v, units[n])).astype(BF16))
          for n in nu]
    kv = [_dot_tn(cut(v, un).astype(BF16), cut(kgm, un).astype(BF16)) for un in units]
    s_m = [eye - x for x in l_pq]
    m_m = [_dot3(x, bd(x)) for x in l_pq]
    nlev = TB.bit_length() - 1
    for lev in range(1, nlev):
        if lev < nlev - 1:
            xs = [_dot3(m_m[n], jnp.concatenate([bd(s_m[n]), bd(m_m[n])], axis=1)) for n in nu]
            s_m = [s_m[n] + xs[n][:, lo_h] for n in nu]
            m_m = [x[:, hi_h] for x in xs]
        else:
            s_m = [s_m[n] + _dot3(m_m[n], bd(s_m[n])) for n in nu]
    ta = [_dot3(s_m[n], jnp.concatenate([bd(cut(pm, units[n])), bd(wm[n][0:TB])], axis=1)) for n in nu]
    zt = [ZT[bb, p] for bb, p in units]
    az = [_dot_nt(jnp.concatenate([ta[n][:, lo_h], cut(rhm, units[n])], axis=0).astype(BF16), zt[n].astype(BF16))
          for n in nu]
    u = [az[n][0:TB] + ta[n][:, hi_h] for n in nu]
    mu = [_dot(m_rq[n].astype(BF16), bd(u[n]).astype(BF16)) for n in nu]
    qu = [_dot_tn(u[n].astype(BF16), cut(qgm, units[n]).astype(BF16)) for n in nu]
    for n in nu:
        bb, p = units[n]
        o_ref[bb, :, p * _LANE:(p + 1) * _LANE] = az[n][TB:2 * TB] - mu[n] + wm[n][TB:2 * TB]
        ZT[bb, p] = zt[n] * cut(gam, units[n])[0:1] + jnp.where(same, kv[n] - qu[n], 0.0)

    @pl.when(i == pl.num_programs(1) - 1)
    def _():
        sout_ref[...] = ZT[...]


def _rwkv_call(rkv, lora, s0, wts, col_major, reverse, rows):
    B, T, _ = rkv.shape
    C = _RW_WIDTH
    nblk = T // _CHUNK
    if col_major:
        assert rows == _CHUNK
    rv = _seq_view(rkv, col_major, rows)
    lv = _seq_view(lora, col_major, rows)
    NP = _RW_HEADS // 2
    bb = _scan_bb(B)
    seq = functools.partial(_seq_spec, col_major=col_major, nblk=nblk, reverse=reverse, bb=bb)
    out_shape = rv.shape[:2] + (rv.shape[2] // 3,)
    st_spec = pl.BlockSpec((bb, NP, _LANE, _LANE), lambda b, i: (b, 0, 0, 0))
    o, bon, s_out = pl.pallas_call(
        functools.partial(_rwkv_kernel, reverse=reverse),
        grid=(B // bb, nblk),
        in_specs=[seq(C, 3 * C, 0), seq(C, 3 * C, 1), seq(C, 3 * C, 2), seq(C, C, 0), st_spec,
                  _full_spec((1, C)), _full_spec((2 * _RW_LORA, C)), _full_spec((1, C)),
                  _full_spec((2 * _RW_LORA, C)), _full_spec((1, C)), _full_spec((1, C)), _full_spec((1, C)),
                  _full_spec((C, C))],
        out_specs=[seq(C, C, 0), seq(C, C, 0), st_spec],
        out_shape=[jax.ShapeDtypeStruct(out_shape, F32), jax.ShapeDtypeStruct(out_shape, F32),
                   jax.ShapeDtypeStruct((B, NP, _LANE, _LANE), F32)],
        scratch_shapes=[pltpu.VMEM((bb, NP, _LANE, _LANE), F32)],
        compiler_params=_cp(("arbitrary", "arbitrary"), _VMEM_LIMIT),
        name="rwkv7_scan",
    )(rv, rv, rv, lv, s0, *wts)
    return o.reshape(B, T, C), bon.reshape(B, T, C), s_out


def _gla_kernel(q_ref, k_ref, v_ref, lo_ref, s0_ref, aup_ref, ab_ref, o_ref, sout_ref, ST, *, reverse):
    i = pl.program_id(1)
    TB = _CHUNK

    @pl.when(i == 0)
    def _():
        ST[...] = s0_ref[...]

    BB = q_ref.shape[0]
    N = BB * TB
    q = _silu(q_ref[...].reshape(N, _GLA_PW)) * (_GLA_DK ** -0.5)
    k = _silu(k_ref[...].reshape(N, _GLA_PW))
    v = _silu(v_ref[...].reshape(N, _GLA_PW))
    gad = lo_ref[...].reshape(N, _RW_WIDTH)[:, 2 * _LANE:3 * _LANE]
    log_a = -_softplus(-(_dot3(gad, aup_ref[...]) + ab_ref[...])) / _GLA_TAU
    row = lax.broadcasted_iota(jnp.int32, (TB, TB), 0)
    col = lax.broadcasted_iota(jnp.int32, (TB, TB), 1)
    keep = (row <= col) if reverse else (row >= col)
    g_cum = _sel_dot(_chunk_tri(N, TB, reverse), log_a)
    e0 = 0 if reverse else TB - 1
    g_last = jnp.concatenate([jnp.broadcast_to(g_cum[bb * TB + e0:bb * TB + e0 + 1], (TB, _GLA_PW))
                              for bb in range(BB)], axis=0)
    q_in = q * jnp.exp(g_cum)
    k_in = k * jnp.exp(-g_cum)
    k_tail = k * jnp.exp(g_last - g_cum)
    dec = jnp.exp(g_last)
    units = [(bb, h) for bb in range(BB) for h in range(_GLA_HEADS)]
    nu = range(len(units))

    def cut(x, un):
        bb, h = un
        return x[bb * TB:(bb + 1) * TB, h * _LANE:(h + 1) * _LANE]

    qh = [cut(q_in, un).astype(BF16) for un in units]
    vh = [cut(v, un).astype(BF16) for un in units]
    st = [ST[bb, h] for bb, h in units]
    att = [_dot_nt(qh[n], cut(k_in, units[n]).astype(BF16)) for n in nu]
    o_inter = [_dot_nt(qh[n], st[n].astype(BF16)) for n in nu]
    kv = [_dot_tn(vh[n], cut(k_tail, units[n]).astype(BF16)) for n in nu]
    o_intra = [_dot(jnp.where(keep, att[n], 0.0).astype(BF16), vh[n]) for n in nu]
    for n in nu:
        bb, h = units[n]
        o_ref[bb, :, h * _LANE:(h + 1) * _LANE] = o_intra[n] + o_inter[n]
        ST[bb, h] = st[n] * cut(dec, units[n])[0:1] + kv[n]

    @pl.when(i == pl.num_programs(1) - 1)
    def _():
        sout_ref[...] = ST[...]


def _gla_call(qkv, lora, s0, aup, ab, col_major, reverse, rows):
    B, T, _ = qkv.shape
    C = _GLA_PW
    nblk = T // _CHUNK
    if col_major:
        assert rows == _CHUNK
    qv = _seq_view(qkv, col_major, rows)
    lv = _seq_view(lora, col_major, rows)
    bb = _scan_bb(B)
    seq = functools.partial(_seq_spec, col_major=col_major, nblk=nblk, reverse=reverse, bb=bb)
    out_shape = qv.shape[:2] + (qv.shape[2] // 3,)
    st_spec = pl.BlockSpec((bb, _GLA_HEADS, _LANE, _LANE), lambda b, i: (b, 0, 0, 0))
    o, s_out = pl.pallas_call(
        functools.partial(_gla_kernel, reverse=reverse),
        grid=(B // bb, nblk),
        in_specs=[seq(C, 3 * C, 0), seq(C, 3 * C, 1), seq(C, 3 * C, 2), seq(_RW_WIDTH, _RW_WIDTH, 0), st_spec,
                  _full_spec((_LANE, C)), _full_spec((1, C))],
        out_specs=[seq(C, C, 0), st_spec],
        out_shape=[jax.ShapeDtypeStruct(out_shape, F32),
                   jax.ShapeDtypeStruct((B, _GLA_HEADS, _LANE, _LANE), F32)],
        scratch_shapes=[pltpu.VMEM((bb, _GLA_HEADS, _LANE, _LANE), F32)],
        compiler_params=_cp(("arbitrary", "arbitrary"), _VMEM_LIMIT),
        name="gla_chunked",
    )(qv, qv, qv, lv, s0, aup, ab)
    return o.reshape(B, T, C), s_out


def _fourier_kernel(x_ref, cg_ref, sg_ref, w_ref, o_ref, xcs_ref):
    T = x_ref.shape[1]

    @pl.when(pl.program_id(1) == 0)
    def _():
        x = x_ref[0]
        xcs_ref[0:T, :] = _dot3(x, cg_ref[...]).astype(BF16)
        xcs_ref[T:2 * T, :] = _dot3(x, sg_ref[...]).astype(BF16)

    o_ref[0] = _dot(w_ref[...], xcs_ref[...])


def _fourier_call(zpost, cg, sg, wt):
    B, T, _ = zpost.shape
    C = _FN_WIDTH
    tm = _tile(T, 512, 8)
    return pl.pallas_call(
        _fourier_kernel,
        grid=(B, T // tm),
        in_specs=[pl.BlockSpec((1, T, C), lambda b, i: (b, 0, 0)),
                  _full_spec((C, C)), _full_spec((C, C)),
                  pl.BlockSpec((tm, 2 * T), lambda b, i: (i, 0))],
        out_specs=pl.BlockSpec((1, tm, C), lambda b, i: (b, i, 0)),
        out_shape=jax.ShapeDtypeStruct((B, T, C), F32),
        scratch_shapes=[pltpu.VMEM((2 * T, C), BF16)],
        compiler_params=_cp(("arbitrary", "arbitrary"), _VMEM_LIMIT),
        name="fnet_dft",
    )(zpost, cg, sg, wt)


def _merge_kernel(rwf_ref, rwb_ref, bnf_ref, bnb_ref, glf_ref, glb_ref, zp_ref, fn_ref, h_ref, g1_ref, sc_ref, sh_ref,
                  nw_ref, wg_ref, jr_ref, lnw_ref, lnb_ref, gup_ref, jg_ref, gnw_ref, pa_ref, pb_ref, pc_ref, wo_ref,
                  o_ref):
    D = h_ref.shape[2]
    zp = zp_ref[0]
    og = zp[:, _FN_WIDTH:_FN_WIDTH + _GLA_PW]
    gd = zp[:, _FN_WIDTH + _GLA_PW:_FN_WIDTH + _GLA_PW + _RW_GATE_RANK]
    o = rwf_ref[0] + rwb_ref[0]
    jr = jr_ref[...]
    mu = _dot_sel(o, jr) * (1.0 / _RW_HD)
    xc = o - mu
    var = _dot_sel(xc * xc, jr) * (1.0 / _RW_HD)
    y = xc * lax.rsqrt(var + _RW_GN_EPS)
    y = y * lnw_ref[...] + lnb_ref[...] + (bnf_ref[0] + bnb_ref[0])
    rw_y = y * _dot3(_sigmoid(gd), gup_ref[...])
    g = glf_ref[0] + glb_ref[0]
    ms = _dot_sel(g * g, jg_ref[...]) * (1.0 / _GLA_DV)
    gla_y = g * lax.rsqrt(ms + _NORM_EPS) * gnw_ref[...] * _silu(og)
    h = h_ref[0]
    u = _modnorm(h, nw_ref[...], sc_ref[0], sh_ref[0]).astype(BF16)
    gates = _sigmoid(_dot(u, wg_ref[...]))
    m = (gates[:, 0:D] * _dot(rw_y.astype(BF16), pa_ref[...])
         + gates[:, D:2 * D] * _dot(gla_y.astype(BF16), pb_ref[...])
         + gates[:, 2 * D:3 * D] * _dot(fn_ref[0].astype(BF16), pc_ref[...]))
    o_ref[0] = h + g1_ref[0] * _dot(m.astype(BF16), wo_ref[...])


def _merge_call(rwf, rwb, bnf, bnb, glf, glb, zpost, fn, h, g1, sc, sh, wts):
    B, T, D = h.shape
    tm = _tile(T, 512, 8)

    def row(c):
        return pl.BlockSpec((1, tm, c), lambda b, i: (b, i, 0))

    acts = [rwf, rwb, bnf, bnb, glf, glb, zpost, fn, h]
    mods = [g1, sc, sh]
    return pl.pallas_call(
        _merge_kernel,
        grid=(B, T // tm),
        in_specs=[row(a.shape[2]) for a in acts] + [pl.BlockSpec((1, 1, D), lambda b, i: (b, 0, 0))] * len(mods)
        + [_full_spec(w.shape) for w in wts],
        out_specs=row(D),
        out_shape=jax.ShapeDtypeStruct((B, T, D), F32),
        compiler_params=_cp(("arbitrary", "arbitrary"), _VMEM_LIMIT),
        name="branch_merge",
    )(*acts, *mods, *wts)


def _router_kernel(h_ref, nw_ref, sc_ref, sh_ref, rt_ref, v_ref, aff_ref):
    u = _modnorm(h_ref[0], nw_ref[...], sc_ref[0], sh_ref[0])
    v_ref[0] = u.astype(BF16)
    logits = _dot_nt(rt_ref[...], u, HI)
    e = jnp.exp(logits - jnp.max(logits, axis=0, keepdims=True))
    aff_ref[0] = e / jnp.sum(e, axis=0, keepdims=True)


def _router_call(h, nw, sc, sh, router_t):
    B, T, D = h.shape
    E = router_t.shape[0]
    tm = _tile(T, 512, 128)
    return pl.pallas_call(
        _router_kernel,
        grid=(B, T // tm),
        in_specs=[pl.BlockSpec((1, tm, D), lambda b, i: (b, i, 0)), _full_spec((1, D)),
                  pl.BlockSpec((1, 1, D), lambda b, i: (b, 0, 0)), pl.BlockSpec((1, 1, D), lambda b, i: (b, 0, 0)),
                  _full_spec((E, D))],
        out_specs=[pl.BlockSpec((1, tm, D), lambda b, i: (b, i, 0)), pl.BlockSpec((1, E, tm), lambda b, i: (b, 0, i))],
        out_shape=[jax.ShapeDtypeStruct((B, T, D), BF16), jax.ShapeDtypeStruct((B, E, T), F32)],
        compiler_params=_cp(("arbitrary", "arbitrary"), _VMEM_LIMIT),
        name="router_softmax",
    )(h, nw, sc, sh, router_t)


def _topc_kernel(aff_ref, pos_ref, cum_ref, *, cap):
    E, T = aff_ref.shape[1], aff_ref.shape[2]
    x = pltpu.bitcast(aff_ref[0], jnp.int32)

    def body(it, thr):
        cand = thr | lax.shift_left(jnp.int32(1), 30 - it)
        cnt = jnp.sum(jnp.where(x >= cand, 1, 0), axis=1, keepdims=True)
        return jnp.where(cnt >= cap, cand, thr)

    thr = lax.fori_loop(0, 31, body, jnp.zeros((E, 1), jnp.int32))
    need = (cap - jnp.sum(jnp.where(x > thr, 1, 0), axis=1, keepdims=True)).astype(F32)
    blk = _LANE
    upper = jnp.where(lax.broadcasted_iota(jnp.int32, (blk, blk), 0) < lax.broadcasted_iota(jnp.int32, (blk, blk), 1),
                      1.0, 0.0).astype(BF16)
    off_eq = jnp.zeros((E, 1), F32)
    off_sel = jnp.zeros((E, 1), F32)
    for c in range(T // blk):
        sl = slice(c * blk, (c + 1) * blk)
        xc = pltpu.bitcast(aff_ref[0, :, sl], jnp.int32)
        eq_c = jnp.where(xc == thr, 1.0, 0.0)
        rank_eq = _dot(eq_c.astype(BF16), upper) + off_eq
        take = jnp.where(rank_eq < need, eq_c, 0.0)
        sel = jnp.where(xc > thr, 1.0, take)
        rank = _dot(sel.astype(BF16), upper) + off_sel
        pos_ref[0, :, sl] = jnp.where(sel > 0.0, rank.astype(jnp.int32), -1)
        off_eq = off_eq + jnp.sum(eq_c, axis=1, keepdims=True)
        off_sel = off_sel + jnp.sum(sel, axis=1, keepdims=True)
    chosen = jnp.where(pos_ref[0] >= 0, 1.0, 0.0).astype(BF16)
    before = jnp.where(lax.shift_right_logical(lax.broadcasted_iota(jnp.int32, (T, blk), 0), blk.bit_length() - 1)
                       < lax.broadcasted_iota(jnp.int32, (T, blk), 1), 1.0, 0.0).astype(BF16)
    cum_ref[0] = _dot(chosen, before).astype(jnp.int32)


def _topc_call(aff, cap):
    B, E, T = aff.shape
    assert T // _LANE < _LANE
    pos, cum = pl.pallas_call(
        functools.partial(_topc_kernel, cap=cap),
        grid=(B,),
        in_specs=[pl.BlockSpec((1, E, T), lambda b: (b, 0, 0))],
        out_specs=[pl.BlockSpec((1, E, T), lambda b: (b, 0, 0)), pl.BlockSpec((1, E, _LANE), lambda b: (b, 0, 0))],
        out_shape=[jax.ShapeDtypeStruct((B, E, T), jnp.int32), jax.ShapeDtypeStruct((B, E, _LANE), jnp.int32)],
        compiler_params=_cp(("arbitrary",), _VMEM_LIMIT),
        name="expert_choice_topc",
    )(aff)
    return pos, cum[:, :, :T // _LANE + 1].reshape(-1)


def _slot_block(cap):
    return min(cap, _LANE)


def _ffn_kernel(cum_ref, v_ref, pos_ref, aff_ref, w1_ref, w3_ref, w2_ref, ys_ref, xs_ref, acc_ref, gate_ref, *, cap, tc,
                expert_axis):
    e = pl.program_id(expert_axis)
    b = pl.program_id(1 - expert_axis)
    f = pl.program_id(2)
    T = v_ref.shape[1]
    E = pos_ref.shape[1]
    sbz = _slot_block(cap)

    @pl.when(f == 0)
    def _():
        acc_ref[...] = jnp.zeros(acc_ref.shape, F32)
        gate_ref[...] = jnp.zeros(gate_ref.shape, F32)
        base = (b * E + e) * (T // _LANE + 1)
        for c in range(T // tc):
            sl = slice(c * tc, (c + 1) * tc)
            lo = cum_ref[base + c * (tc // _LANE)]
            hi = cum_ref[base + (c + 1) * (tc // _LANE)]
            for sb in range(cap // sbz):
                rows = slice(sb * sbz, (sb + 1) * sbz)

                @pl.when(jnp.logical_and(lo < (sb + 1) * sbz, hi > sb * sbz))
                def _():
                    slot = lax.broadcasted_iota(jnp.int32, (sbz, tc), 0) + sb * sbz
                    hit = slot == pos_ref[0, e, :, sl]
                    acc_ref[rows, :] += _dot(jnp.where(hit, 1.0, 0.0).astype(BF16), v_ref[0, sl, :])
                    gate_ref[rows, :] += jnp.sum(jnp.where(hit, aff_ref[0, e, :, sl], 0.0), axis=1, keepdims=True)
        xs_ref[...] = acc_ref[...].astype(BF16)
        acc_ref[...] = jnp.zeros(acc_ref.shape, F32)

    x = xs_ref[...]
    h1 = _dot(x, w1_ref[0])
    hid = _silu(h1) * _dot(x, w3_ref[0])
    acc_ref[...] += _dot(hid.astype(BF16), w2_ref[0])

    @pl.when(f == pl.num_programs(2) - 1)
    def _():
        ys_ref[0, 0] = (acc_ref[...] * gate_ref[...]).astype(BF16)


def _ffn_call(cum, v, pos, aff, w1, w3, w2, cap):
    B, T, D = v.shape
    E, _, F = w1.shape
    tc = _tile(T, 512)
    expert_major = T <= 512
    fc = F if expert_major else _tile(F, 1024)
    if expert_major:
        grid = (E, B, 1)

        def ix(f):
            return lambda e, b, j: f(b, e, j)
    else:
        grid = (B, E, F // fc)

        def ix(f):
            return f
    return pl.pallas_call(
        functools.partial(_ffn_kernel, cap=cap, tc=tc, expert_axis=0 if expert_major else 1),
        grid=grid,
        in_specs=[pl.BlockSpec(memory_space=pltpu.SMEM),
                  pl.BlockSpec((1, T, D), ix(lambda b, e, f: (b, 0, 0))),
                  pl.BlockSpec((1, E, 1, T), ix(lambda b, e, f: (b, 0, 0, 0))),
                  pl.BlockSpec((1, E, 1, T), ix(lambda b, e, f: (b, 0, 0, 0))),
                  pl.BlockSpec((1, D, fc), ix(lambda b, e, f: (e, 0, f))),
                  pl.BlockSpec((1, D, fc), ix(lambda b, e, f: (e, 0, f))),
                  pl.BlockSpec((1, fc, D), ix(lambda b, e, f: (e, f, 0)))],
        out_specs=pl.BlockSpec((1, 1, cap, D), ix(lambda b, e, f: (b, e, 0, 0))),
        out_shape=jax.ShapeDtypeStruct((B, E, cap, D), BF16),
        scratch_shapes=[pltpu.VMEM((cap, D), BF16), pltpu.VMEM((cap, D), F32), pltpu.VMEM((cap, 1), F32)],
        compiler_params=_cp(("arbitrary", "arbitrary", "arbitrary"), _VMEM_LIMIT),
        name="expert_ffn",
    )(cum, v, pos, aff, w1, w3, w2)


def _scatter_kernel(pos_ref, ys_ref, h_ref, g2_ref, o_ref, *, cap):
    tm = h_ref.shape[1]
    E = ys_ref.shape[1]
    pos_t = pos_ref[0]
    slot = lax.broadcasted_iota(jnp.int32, (tm, cap), 1)
    group = 4 if E % 4 == 0 and cap % _LANE == 0 else 1
    acc = None
    for e0 in range(0, E, group):
        hit = jnp.concatenate([jnp.where(slot == pos_t[:, e:e + 1], 1.0, 0.0).astype(BF16)
                               for e in range(e0, e0 + group)], axis=1)
        part = _dot(hit, ys_ref[0, e0:e0 + group].reshape(group * cap, ys_ref.shape[3]))
        acc = part if acc is None else acc + part
    o_ref[0] = h_ref[0] + g2_ref[0] * acc


def _scatter_call(pos_t, ys, h, g2, cap):
    B, T, D = h.shape
    E = ys.shape[1]
    tm = _tile(T, 512)
    return pl.pallas_call(
        functools.partial(_scatter_kernel, cap=cap),
        grid=(B, T // tm),
        in_specs=[pl.BlockSpec((1, tm, E), lambda b, i: (b, i, 0)),
                  pl.BlockSpec((1, E, cap, D), lambda b, i: (b, 0, 0, 0)),
                  pl.BlockSpec((1, tm, D), lambda b, i: (b, i, 0)),
                  pl.BlockSpec((1, 1, D), lambda b, i: (b, 0, 0))],
        out_specs=pl.BlockSpec((1, tm, D), lambda b, i: (b, i, 0)),
        out_shape=jax.ShapeDtypeStruct((B, T, D), F32),
        compiler_params=_cp(("arbitrary", "arbitrary"), _VMEM_LIMIT),
        name="expert_scatter",
    )(pos_t, ys, h, g2)


def _final_kernel(h_ref, w_ref, o_ref):
    x = h_ref[0]
    o_ref[0] = x * lax.rsqrt(jnp.mean(x * x, axis=-1, keepdims=True) + _NORM_EPS) * w_ref[...]


def _final_call(h, w):
    B, T, D = h.shape
    tm = _tile(T, 1024, 8)
    return pl.pallas_call(
        _final_kernel,
        grid=(B, T // tm),
        in_specs=[pl.BlockSpec((1, tm, D), lambda b, i: (b, i, 0)), _full_spec((1, D))],
        out_specs=pl.BlockSpec((1, tm, D), lambda b, i: (b, i, 0)),
        out_shape=jax.ShapeDtypeStruct((B, T, D), F32),
        compiler_params=_cp(("arbitrary", "arbitrary"), _VMEM_LIMIT),
        name="final_norm",
    )(h, w)


def _pad_heads(x, nh, d, axis=-1):
    axis = axis % x.ndim
    shp = x.shape
    x = x.reshape(shp[:axis] + (nh, d) + shp[axis + 1:])
    pad = [(0, 0)] * x.ndim
    pad[axis + 1] = (0, _LANE - d)
    x = jnp.pad(x, pad)
    return x.reshape(shp[:axis] + (nh * _LANE,) + shp[axis + 1:])


def _pad_to(x, n, axis=-1):
    axis = axis % x.ndim
    pad = [(0, 0)] * x.ndim
    pad[axis] = (0, n - x.shape[axis])
    return jnp.pad(x, pad)


def _block_ones(n, blk):
    i = np.arange(n) // blk
    return jnp.asarray((i[:, None] == i[None, :]).astype(np.float32))


def _dftmat_kernel(c1_ref, s1_ref, c2_ref, s2_ref, o_ref, *, T, s):
    n1 = T // s
    rep = jnp.where(lax.shift_right_logical(lax.broadcasted_iota(jnp.int32, (n1, T), 1), s.bit_length() - 1)
                    == lax.broadcasted_iota(jnp.int32, (n1, T), 0), 1.0, 0.0)
    til = jnp.where(jnp.bitwise_and(lax.broadcasted_iota(jnp.int32, (s, T), 1), s - 1)
                    == lax.broadcasted_iota(jnp.int32, (s, T), 0), 1.0, 0.0)
    c1 = _dot_sel(c1_ref[...], rep)
    s1 = _dot_sel(s1_ref[...], rep)
    c2 = _dot_sel(c2_ref[...], til)
    s2 = _dot_sel(s2_ref[...], til)
    o_ref[:, 0:T] = (c1 * c2 - s1 * s2).astype(BF16)
    o_ref[:, T:2 * T] = (-(s1 * c2 + c1 * s2)).astype(BF16)


def _dft_consts(T):
    s = 1 << ((T.bit_length() - 1) // 2)
    i = np.arange(T)[:, None]
    a1 = 2.0 * np.pi * ((i * s * np.arange(T // s)[None, :]) % T) / T
    a2 = 2.0 * np.pi * ((i * np.arange(s)[None, :]) % T) / T
    tabs = [jnp.asarray(f(a) / math.sqrt(math.sqrt(T)), F32)
            for a, f in ((a1, np.cos), (a1, np.sin), (a2, np.cos), (a2, np.sin))]
    tm = _tile(T, 256, 8)
    wt = pl.pallas_call(
        functools.partial(_dftmat_kernel, T=T, s=s),
        grid=(T // tm,),
        in_specs=[pl.BlockSpec((tm, t.shape[1]), lambda i: (i, 0)) for t in tabs],
        out_specs=pl.BlockSpec((tm, 2 * T), lambda i: (i, 0)),
        out_shape=jax.ShapeDtypeStruct((T, 2 * T), BF16),
        compiler_params=_cp(("arbitrary",), _VMEM_LIMIT),
        name="dft_matrix",
    )(*tabs)
    g = np.arange(_FN_GD)
    ang_g = 2.0 * np.pi * ((g[:, None] * g[None, :]) % _FN_GD) / _FN_GD
    eye = np.eye(_FN_GROUPS)
    cg = np.kron(eye, np.cos(ang_g)) / math.sqrt(_FN_GD)
    sg = np.kron(eye, np.sin(ang_g)) / math.sqrt(_FN_GD)
    return wt, jnp.asarray(cg, F32), jnp.asarray(sg, F32)


def _layer_weights(i, w_in, conv_w, rw_w0, rw_w_up, rw_a0, rw_a_up, rw_k_k, rw_k_a, rw_r_k, rw_g_up, rw_ln_w,
                   rw_ln_b, gla_a_up, gla_a_b, gla_norm_w, proj_a, proj_b, proj_c, w_out):
    D = w_in.shape[1]
    rw, kw, vw = _RW_WIDTH, _GLA_HEADS * _GLA_DK, _GLA_HEADS * _GLA_DV
    o_gq = 3 * rw
    o_gk = o_gq + kw
    o_gv = o_gk + kw
    o_wd = o_gv + vw
    o_ad = o_wd + 2 * _RW_LORA
    o_ga = o_ad + 2 * _RW_LORA
    o_gd = o_ga + 2 * _GLA_RANK
    o_og = o_gd + _RW_GATE_RANK
    o_fn = o_og + vw
    o_gt = o_fn + _FN_WIDTH
    wi = w_in[i]
    cw = conv_w[i].reshape(9, -1)

    def gla_cols(x):
        return jnp.concatenate([_pad_heads(x[..., o_gq:o_gk], _GLA_HEADS, _GLA_DK),
                                _pad_heads(x[..., o_gk:o_gv], _GLA_HEADS, _GLA_DK),
                                _pad_heads(x[..., o_gv:o_wd], _GLA_HEADS, _GLA_DV)], axis=-1)

    w = {}
    w['in_rw'] = wi[:, 0:o_gq].astype(BF16)
    w['in_gla'] = gla_cols(wi).astype(BF16)
    w['in_lora'] = jnp.concatenate([wi[:, o_wd:o_ga], _pad_to(wi[:, o_ga:o_gd], _LANE)], axis=-1).astype(BF16)
    w['in_post'] = jnp.concatenate([wi[:, o_fn:o_gt], _pad_heads(wi[:, o_og:o_fn], _GLA_HEADS, _GLA_DV),
                                    wi[:, o_gd:o_og]], axis=-1).astype(BF16)
    w['in_gates'] = wi[:, o_gt:].astype(BF16)
    w['conv_rw'] = cw[:, 0:o_gq]
    w['conv_gla'] = gla_cols(cw)
    zl = jnp.zeros((_RW_LORA, rw), F32)
    w['rwkv'] = []
    w['gla'] = []
    jr = _block_ones(rw, _RW_HD)
    for d in range(2):
        wup = jnp.concatenate([rw_w_up[i, d], zl] if d == 0 else [zl, rw_w_up[i, d]], axis=0)
        aup = jnp.concatenate([rw_a_up[i, d], zl] if d == 0 else [zl, rw_a_up[i, d]], axis=0)
        w['rwkv'].append((rw_w0[i, d][None], wup, rw_a0[i, d][None], aup, rw_k_k[i][None], rw_k_a[i][None],
                          rw_r_k[i].reshape(1, rw), jr))
        ga = _pad_heads(gla_a_up[i, d], _GLA_HEADS, _GLA_DK)
        ga = jnp.pad(ga, ((d * _GLA_RANK, _LANE - (d + 1) * _GLA_RANK), (0, 0)))
        w['gla'].append((ga, _pad_heads(gla_a_b[i, d][None], _GLA_HEADS, _GLA_DK)))
    w['merge'] = (jr, rw_ln_w[i][None], rw_ln_b[i][None], rw_g_up[i], _block_ones(_GLA_PW, _LANE),
                  _pad_heads(jnp.tile(gla_norm_w[i], _GLA_HEADS)[None], _GLA_HEADS, _GLA_DV),
                  proj_a[i].astype(BF16), _pad_heads(proj_b[i], _GLA_HEADS, _GLA_DV, axis=0).astype(BF16),
                  proj_c[i].astype(BF16), w_out[i].astype(BF16))
    return w


def _moe(h, nw, sc, sh, g2, router_t, w1, w3, w2):
    B, T, D = h.shape
    E = router_t.shape[0]
    cap = 2 * T // E
    v, aff = _router_call(h, nw, sc, sh, router_t)
    pos, cum = _topc_call(aff, cap)
    ys = _ffn_call(cum, v, pos.reshape(B, E, 1, T), aff.reshape(B, E, 1, T), w1, w3, w2, cap)
    return _scatter_call(jnp.swapaxes(pos, 1, 2), ys, h, g2, cap)


def kernel(x, c, ctx, c_ctx, w_mod, b_mod, norm1_w, norm2_w, w_in, conv_w, rw_w0, rw_w_up, rw_a0, rw_a_up, rw_k_k, rw_k_a, rw_r_k, rw_g_up, rw_ln_w, rw_ln_b, gla_a_up, gla_a_b, gla_norm_w, proj_a, proj_b, proj_c, w_out, router, exp_w1, exp_w3, exp_w2, final_norm_w):
    B, S, D = x.shape
    TC = ctx.shape[1]
    L = w_mod.shape[0]
    W = _GRID_W
    rows = S // W
    assert S % _CHUNK == 0 and TC % _CHUNK == 0

    cc = _pad_to(jnp.concatenate([c, c_ctx[None]], axis=0), 16, axis=0)
    mods = _mod_call(cc, w_mod, b_mod)
    dft = {S: _dft_consts(S), TC: _dft_consts(TC)}
    s0_rw = jnp.zeros((B, _RW_HEADS // 2, _LANE, _LANE), F32)
    s0_gla = jnp.zeros((B, _GLA_HEADS, _LANE, _LANE), F32)

    h_lat, h_ctx = x, ctx
    for i in range(L):
        col_major = i % 2 == 1
        need_ctx = i < L - 1
        lw = _layer_weights(i, w_in, conv_w, rw_w0, rw_w_up, rw_a0, rw_a_up, rw_k_k, rw_k_a, rw_r_k, rw_g_up,
                            rw_ln_w, rw_ln_b, gla_a_up, gla_a_b, gla_norm_w, proj_a, proj_b, proj_c, w_out)
        m_lat = mods[i, :B].reshape(B, 1, _N_MOD, D)
        m_ctx = jnp.broadcast_to(mods[i, B:B + 1].reshape(1, 1, _N_MOD, D), (B, 1, _N_MOD, D))
        n1 = norm1_w[i][None]
        n2 = norm2_w[i][None]

        def in_proj(h, m, img_rows, img_w):
            z = {k: _norm_matmul(h, n1, m[:, :, 1], m[:, :, 0], lw['in_' + k])
                 for k in ('rw', 'gla', 'lora', 'post')}
            z['rw'] = _conv_call(z['rw'], lw['conv_rw'], img_rows, img_w)
            z['gla'] = _conv_call(z['gla'], lw['conv_gla'], img_rows, img_w)
            return z

        z_lat = in_proj(h_lat, m_lat, rows, W)
        z_ctx = in_proj(h_ctx, m_ctx, 1, TC)

        rw_lat, rw_ctx, gl_lat, gl_ctx = [], [], [], []
        for d in range(2):
            rev = d == 1
            o_c, b_c, st = _rwkv_call(z_ctx['rw'], z_ctx['lora'], s0_rw, lw['rwkv'][d], False, rev, 1)
            o_l, b_l, _ = _rwkv_call(z_lat['rw'], z_lat['lora'], st, lw['rwkv'][d], col_major, rev, rows)
            rw_ctx.append((o_c, b_c))
            rw_lat.append((o_l, b_l))
            g_c, st = _gla_call(z_ctx['gla'], z_ctx['lora'], s0_gla, *lw['gla'][d], False, rev, 1)
            g_l, _ = _gla_call(z_lat['gla'], z_lat['lora'], st, *lw['gla'][d], col_major, rev, rows)
            gl_ctx.append(g_c)
            gl_lat.append(g_l)

        def mix(h, m, z, rwo, glo, T):
            wt, cg, sg = dft[T]
            fn = _fourier_call(z['post'], cg, sg, wt)
            return _merge_call(rwo[0][0], rwo[1][0], rwo[0][1], rwo[1][1], glo[0], glo[1], z['post'], fn, h,
                               m[:, :, 2], m[:, :, 1], m[:, :, 0], (n1, lw['in_gates']) + lw['merge'])

        router_t = router[i].T
        w1 = exp_w1[i].astype(BF16)
        w3 = exp_w3[i].astype(BF16)
        w2 = exp_w2[i].astype(BF16)
        h_lat = mix(h_lat, m_lat, z_lat, rw_lat, gl_lat, S)
        h_lat = _moe(h_lat, n2, m_lat[:, :, 4], m_lat[:, :, 3], m_lat[:, :, 5], router_t, w1, w3, w2)
        if need_ctx:
            h_ctx = mix(h_ctx, m_ctx, z_ctx, rw_ctx, gl_ctx, TC)
            h_ctx = _moe(h_ctx, n2, m_ctx[:, :, 4], m_ctx[:, :, 3], m_ctx[:, :, 5], router_t, w1, w3, w2)
    return _final_call(h_lat, final_norm_w[None])
```

```python
import functools
import math

import jax
import jax.numpy as jnp
import numpy as np
from jax import lax
from jax.experimental import pallas as pl
from jax.experimental.pallas import tpu as pltpu

F32 = jnp.float32
BF16 = jnp.bfloat16
HI = lax.Precision.HIGHEST

_GRID_W = 64
_NORM_EPS = 1e-6
_N_MOD = 6
_RW_HEADS = 6
_RW_HD = 64
_RW_WIDTH = _RW_HEADS * _RW_HD
_RW_LORA = 64
_RW_GATE_RANK = 128
_RW_GN_EPS = 64e-5
_GLA_HEADS = 4
_GLA_DK = 48
_GLA_DV = 96
_GLA_RANK = 16
_GLA_TAU = 16.0
_CHUNK = 64
_FN_GROUPS = 4
_FN_GD = 64
_FN_WIDTH = _FN_GROUPS * _FN_GD
_LANE = 128
_GLA_PW = _GLA_HEADS * _LANE
_VMEM_LIMIT = 56 * 1024 * 1024


def _cp(sem, vmem=None):
    return pltpu.CompilerParams(dimension_semantics=sem, vmem_limit_bytes=vmem)


def _tile(n, cap, mult=128):
    if n <= cap:
        return n
    best = None
    for t in range(mult, cap + 1, mult):
        if n % t == 0:
            best = t
    assert best is not None, (n, cap)
    return best


def _sigmoid(x):
    return 1.0 / (1.0 + jnp.exp(-x))


def _silu(x):
    return x * _sigmoid(x)


def _softplus(x):
    return jnp.maximum(x, 0.0) + jnp.log(1.0 + jnp.exp(-jnp.abs(x)))


def _dot(a, b, precision=None):
    return jnp.dot(a, b, preferred_element_type=F32, precision=precision)


def _dot_nt(a, b, precision=None):
    return lax.dot_general(a, b, (((1,), (1,)), ((), ())), preferred_element_type=F32, precision=precision)


def _dot_tn(a, b, precision=None):
    return lax.dot_general(a, b, (((0,), (0,)), ((), ())), preferred_element_type=F32, precision=precision)


def _mod_kernel(c_ref, w_ref, b_ref, o_ref):
    o_ref[0] = _dot(_silu(c_ref[...]), w_ref[0], HI) + b_ref[0]


def _mod_call(cc, w_mod, b_mod):
    L, D, N = w_mod.shape
    R = cc.shape[0]
    tn = _tile(N, 1536)
    return pl.pallas_call(
        _mod_kernel,
        grid=(L, N // tn),
        in_specs=[pl.BlockSpec((R, D), lambda l, j: (0, 0)),
                  pl.BlockSpec((1, D, tn), lambda l, j: (l, 0, j)),
                  pl.BlockSpec((1, 1, tn), lambda l, j: (l, 0, j))],
        out_specs=pl.BlockSpec((1, R, tn), lambda l, j: (l, 0, j)),
        out_shape=jax.ShapeDtypeStruct((L, R, N), F32),
        compiler_params=_cp(("arbitrary", "arbitrary"), _VMEM_LIMIT),
        name="adaln_mod",
    )(cc, w_mod, b_mod.reshape(L, 1, N))


def _modnorm(x, nw, sc, sh):
    y = x * lax.rsqrt(jnp.mean(x * x, axis=-1, keepdims=True) + _NORM_EPS)
    return (y * nw) * (1.0 + sc) + sh


def _nm_kernel(h_ref, nw_ref, sc_ref, sh_ref, w_ref, o_ref, u_ref):
    @pl.when(pl.program_id(2) == 0)
    def _():
        u_ref[...] = _modnorm(h_ref[0], nw_ref[...], sc_ref[0], sh_ref[0]).astype(BF16)

    o_ref[0] = _dot(u_ref[...], w_ref[...])


def _norm_matmul(h, nw, sc, sh, w):
    B, T, D = h.shape
    N = w.shape[1]
    tm = _tile(T, 1024, 8)
    tn = _tile(N, 1536)
    return pl.pallas_call(
        _nm_kernel,
        grid=(B, T // tm, N // tn),
        in_specs=[pl.BlockSpec((1, tm, D), lambda b, i, j: (b, i, 0)),
                  pl.BlockSpec((1, D), lambda b, i, j: (0, 0)),
                  pl.BlockSpec((1, 1, D), lambda b, i, j: (b, 0, 0)),
                  pl.BlockSpec((1, 1, D), lambda b, i, j: (b, 0, 0)),
                  pl.BlockSpec((D, tn), lambda b, i, j: (0, j))],
        out_specs=pl.BlockSpec((1, tm, tn), lambda b, i, j: (b, i, j)),
        out_shape=jax.ShapeDtypeStruct((B, T, N), F32),
        scratch_shapes=[pltpu.VMEM((tm, D), BF16)],
        compiler_params=_cp(("arbitrary", "arbitrary", "arbitrary"), _VMEM_LIMIT),
        name="norm_in_proj",
    )(h, nw, sc, sh, w)


def _conv_kernel(x_ref, w_ref, o_ref, xp_ref, *, rows, W, T, PAD, CH):
    cw = x_ref.shape[2]
    xp_ref[0:PAD, :] = jnp.zeros((PAD, cw), F32)
    xp_ref[PAD + T:PAD + T + PAD, :] = jnp.zeros((PAD, cw), F32)
    xp_ref[PAD:PAD + T, :] = x_ref[0]
    for c0 in range(0, T, CH):
        col = jnp.bitwise_and(lax.broadcasted_iota(jnp.int32, (CH, cw), 0) + c0, W - 1)
        acc = jnp.zeros((CH, cw), F32)
        for b in range(3):
            part = None
            for a in range(3):
                if rows == 1 and a != 1:
                    continue
                off = (a - 1) * W + (b - 1)
                term = xp_ref[PAD + c0 + off:PAD + c0 + off + CH, :] * w_ref[a * 3 + b:a * 3 + b + 1, :]
                part = term if part is None else part + term
            if b == 0:
                part = jnp.where(col >= 1, part, 0.0)
            elif b == 2:
                part = jnp.where(col <= W - 2, part, 0.0)
            acc = acc + part
        o_ref[0, c0:c0 + CH, :] = acc


def _conv_call(z, w9, rows, W):
    B, T, C = z.shape
    assert W & (W - 1) == 0 and rows * W == T
    cw = _LANE
    PAD = W + 8 if rows > 1 else 8
    CH = min(T, 128)
    kern = functools.partial(_conv_kernel, rows=rows, W=W, T=T, PAD=PAD, CH=CH)
    return pl.pallas_call(
        kern,
        grid=(B, C // cw),
        in_specs=[pl.BlockSpec((1, T, cw), lambda b, j: (b, 0, j)),
                  pl.BlockSpec((9, cw), lambda b, j: (0, j))],
        out_specs=pl.BlockSpec((1, T, cw), lambda b, j: (b, 0, j)),
        out_shape=jax.ShapeDtypeStruct((B, T, C), F32),
        scratch_shapes=[pltpu.VMEM((T + 2 * PAD, cw), F32)],
        compiler_params=_cp(("arbitrary", "arbitrary"), _VMEM_LIMIT),
        name="short_conv",
    )(z, w9)


def _seq_view(arr, col_major, rows):
    B, T, C = arr.shape
    return arr.reshape(B, rows, (T // rows) * C) if col_major else arr


def _seq_spec(C, ctot, part, col_major, nblk, reverse, bb):
    nper = ctot // C

    def blk(i):
        return nblk - 1 - i if reverse else i

    if col_major:
        return pl.BlockSpec((bb, _CHUNK, C), lambda b, i: (b, 0, blk(i) * nper + part))
    return pl.BlockSpec((bb, _CHUNK, C), lambda b, i: (b, blk(i), part))


def _scan_bb(B):
    return 4 if B % 4 == 0 else (2 if B % 2 == 0 else 1)


def _chunk_tri(n, tb, reverse):
    row = lax.broadcasted_iota(jnp.int32, (n, n), 0)
    col = lax.broadcasted_iota(jnp.int32, (n, n), 1)
    sh = tb.bit_length() - 1
    same = lax.shift_right_logical(row, sh) == lax.shift_right_logical(col, sh)
    order = (row <= col) if reverse else (row >= col)
    return jnp.where(jnp.logical_and(same, order), 1.0, 0.0)


def _full_spec(shape):
    nd = len(shape)
    return pl.BlockSpec(shape, lambda b, i: (0,) * nd)


def _split(x):
    hi = x.astype(BF16)
    return hi, (x - hi.astype(F32)).astype(BF16)


def _dot3(a, b, nt=False):
    f = _dot_nt if nt else _dot
    ah, al = _split(a)
    bh, bl = _split(b)
    return f(ah, bh) + (f(ah, bl) + f(al, bh))


def _dot_sel(a, sel):
    sel = sel.astype(BF16)
    a1 = a.astype(BF16)
    r1 = a - a1.astype(F32)
    a2 = r1.astype(BF16)
    a3 = (r1 - a2.astype(F32)).astype(BF16)
    return _dot(a1, sel) + (_dot(a2, sel) + _dot(a3, sel))


def _sel_dot(sel, b):
    sel = sel.astype(BF16)
    b1 = b.astype(BF16)
    r1 = b - b1.astype(F32)
    b2 = r1.astype(BF16)
    b3 = (r1 - b2.astype(F32)).astype(BF16)
    return _dot(sel, b1) + (_dot(sel, b2) + _dot(sel, b3))


def _rwkv_kernel(r_ref, k_ref, v_ref, lo_ref, s0_ref, w0_ref, wup_ref, a0_ref, aup_ref, kkw_ref, kaw_ref,
                 rkw_ref, j_ref, o_ref, bon_ref, sout_ref, ZT, *, reverse):
    i = pl.program_id(1)
    TB = _CHUNK
    NP = _RW_HEADS // 2
    BB = r_ref.shape[0]
    N = BB * TB

    @pl.when(i == 0)
    def _():
        ZT[...] = s0_ref[...]

    r = r_ref[...].reshape(N, _RW_WIDTH)
    k = k_ref[...].reshape(N, _RW_WIDTH)
    v = v_ref[...].reshape(N, _RW_WIDTH)
    lo = lo_ref[...].reshape(N, _RW_WIDTH)
    jm = j_ref[...]
    w_log = -_softplus(-(w0_ref[...] + _dot3(jnp.tanh(lo[:, 0:_LANE]), wup_ref[...]))) - 0.5
    a = _sigmoid(a0_ref[...] + _dot3(lo[:, _LANE:2 * _LANE], aup_ref[...]))
    kk0 = k * kkw_ref[...]
    kk = kk0 / jnp.maximum(jnp.sqrt(_dot_sel(kk0 * kk0, jm)), 1e-12)
    km = k * (1.0 + (a - 1.0) * kaw_ref[...])
    bon_ref[...] = (_dot_sel(r * km * rkw_ref[...], jm) * v).reshape(BB, TB, _RW_WIDTH)

    lw = -jnp.exp(w_log)
    g = _sel_dot(_chunk_tri(N, TB, reverse), lw)
    e0 = 0 if reverse else TB - 1
    g_end = jnp.concatenate([jnp.broadcast_to(g[bb * TB + e0:bb * TB + e0 + 1], (TB, _RW_WIDTH))
                             for bb in range(BB)], axis=0)
    pm = kk * jnp.exp(g - lw)
    qm = (kk * a) * jnp.exp(-g)
    khm = km * jnp.exp(-g)
    rhm = r * jnp.exp(g)
    qgm = (kk * a) * jnp.exp(g_end - g)
    kgm = km * jnp.exp(g_end - g)
    gam = jnp.exp(g_end)

    t_i = lax.broadcasted_iota(jnp.int32, (TB, _LANE), 0)
    lane = lax.broadcasted_iota(jnp.int32, (TB, _LANE), 1)
    s_i = jnp.bitwise_and(lane, _RW_HD - 1)
    m_a = lane < _RW_HD
    strict = (s_i > t_i) if reverse else (s_i < t_i)
    incl = (s_i >= t_i) if reverse else (s_i <= t_i)
    eye = jnp.where(s_i == t_i, 1.0, 0.0)
    same = ((lax.broadcasted_iota(jnp.int32, (_LANE, _LANE), 0) < _RW_HD)
            == (lax.broadcasted_iota(jnp.int32, (_LANE, _LANE), 1) < _RW_HD))

    def bd(x):
        return jnp.concatenate([jnp.where(m_a, x, 0.0), jnp.where(m_a, 0.0, x)], axis=0)

    units = [(bb, p) for bb in range(BB) for p in range(NP)]
    nu = range(len(units))

    def cut(x, un):
        bb, p = un
        return x[bb * TB:(bb + 1) * TB, p * _LANE:(p + 1) * _LANE]

    lo_h, hi_h = slice(0, _LANE), slice(_LANE, 2 * _LANE)
    gram = [_dot3(jnp.concatenate([cut(pm, un), cut(rhm, un)], axis=0),
                  jnp.concatenate([bd(cut(qm, un)), bd(cut(khm, un))], axis=0), nt=True) for un in units]
    l_pq = [jnp.where(strict, x[0:TB, lo_h], 0.0) for x in gram]
    l_pk = [jnp.where(strict, x[0:TB, hi_h], 0.0) for x in gram]
    m_rq = [jnp.where(incl, x[TB:2 * TB, lo_h], 0.0) for x in gram]
    m_rk = [jnp.where(incl, x[TB:2 * TB, hi_h], 0.0) for x in gram]
    wm = [_dot(jnp.concatenate([l_pk[n], m_rk[n]], axis=0).astype(BF16), bd(cut(v, units[n])).astype(BF16))
          for n in nu]
    kv = [_dot_tn(cut(v, un).astype(BF16), cut(kgm, un).astype(BF16)) for un in units]
    s_m = [eye - x for x in l_pq]
    m_m = [_dot3(x, bd(x)) for x in l_pq]
    nlev = TB.bit_length() - 1
    for lev in range(1, nlev):
        if lev < nlev - 1:
            xs = [_dot3(m_m[n], jnp.concatenate([bd(s_m[n]), bd(m_m[n])], axis=1)) for n in nu]
            s_m = [s_m[n] + xs[n][:, lo_h] for n in nu]
            m_m = [x[:, hi_h] for x in xs]
        else:
            s_m = [s_m[n] + _dot3(m_m[n], bd(s_m[n])) for n in nu]
    ta = [_dot3(s_m[n], jnp.concatenate([bd(cut(pm, units[n])), bd(wm[n][0:TB])], axis=1)) for n in nu]
    zt = [ZT[bb, p] for bb, p in units]
    az = [_dot_nt(jnp.concatenate([ta[n][:, lo_h], cut(rhm, units[n])], axis=0).astype(BF16), zt[n].astype(BF16))
          for n in nu]
    u = [az[n][0:TB] + ta[n][:, hi_h] for n in nu]
    mu = [_dot(m_rq[n].astype(BF16), bd(u[n]).astype(BF16)) for n in nu]
    qu = [_dot_tn(u[n].astype(BF16), cut(qgm, units[n]).astype(BF16)) for n in nu]
    for n in nu:
        bb, p = units[n]
        o_ref[bb, :, p * _LANE:(p + 1) * _LANE] = az[n][TB:2 * TB] - mu[n] + wm[n][TB:2 * TB]
        ZT[bb, p] = zt[n] * cut(gam, units[n])[0:1] + jnp.where(same, kv[n] - qu[n], 0.0)

    @pl.when(i == pl.num_programs(1) - 1)
    def _():
        sout_ref[...] = ZT[...]


def _rwkv_call(rkv, lora, s0, wts, col_major, reverse, rows):
    B, T, _ = rkv.shape
    C = _RW_WIDTH
    nblk = T // _CHUNK
    if col_major:
        assert rows == _CHUNK
    rv = _seq_view(rkv, col_major, rows)
    lv = _seq_view(lora, col_major, rows)
    NP = _RW_HEADS // 2
    bb = _scan_bb(B)
    seq = functools.partial(_seq_spec, col_major=col_major, nblk=nblk, reverse=reverse, bb=bb)
    out_shape = rv.shape[:2] + (rv.shape[2] // 3,)
    st_spec = pl.BlockSpec((bb, NP, _LANE, _LANE), lambda b, i: (b, 0, 0, 0))
    o, bon, s_out = pl.pallas_call(
        functools.partial(_rwkv_kernel, reverse=reverse),
        grid=(B // bb, nblk),
        in_specs=[seq(C, 3 * C, 0), seq(C, 3 * C, 1), seq(C, 3 * C, 2), seq(C, C, 0), st_spec,
                  _full_spec((1, C)), _full_spec((2 * _RW_LORA, C)), _full_spec((1, C)),
                  _full_spec((2 * _RW_LORA, C)), _full_spec((1, C)), _full_spec((1, C)), _full_spec((1, C)),
                  _full_spec((C, C))],
        out_specs=[seq(C, C, 0), seq(C, C, 0), st_spec],
        out_shape=[jax.ShapeDtypeStruct(out_shape, F32), jax.ShapeDtypeStruct(out_shape, F32),
                   jax.ShapeDtypeStruct((B, NP, _LANE, _LANE), F32)],
        scratch_shapes=[pltpu.VMEM((bb, NP, _LANE, _LANE), F32)],
        compiler_params=_cp(("arbitrary", "arbitrary"), _VMEM_LIMIT),
        name="rwkv7_scan",
    )(rv, rv, rv, lv, s0, *wts)
    return o.reshape(B, T, C), bon.reshape(B, T, C), s_out


def _gla_kernel(q_ref, k_ref, v_ref, lo_ref, s0_ref, aup_ref, ab_ref, o_ref, sout_ref, ST, *, reverse):
    i = pl.program_id(1)
    TB = _CHUNK

    @pl.when(i == 0)
    def _():
        ST[...] = s0_ref[...]

    BB = q_ref.shape[0]
    N = BB * TB
    q = _silu(q_ref[...].reshape(N, _GLA_PW)) * (_GLA_DK ** -0.5)
    k = _silu(k_ref[...].reshape(N, _GLA_PW))
    v = _silu(v_ref[...].reshape(N, _GLA_PW))
    gad = lo_ref[...].reshape(N, _RW_WIDTH)[:, 2 * _LANE:3 * _LANE]
    log_a = -_softplus(-(_dot3(gad, aup_ref[...]) + ab_ref[...])) / _GLA_TAU
    row = lax.broadcasted_iota(jnp.int32, (TB, TB), 0)
    col = lax.broadcasted_iota(jnp.int32, (TB, TB), 1)
    keep = (row <= col) if reverse else (row >= col)
    g_cum = _sel_dot(_chunk_tri(N, TB, reverse), log_a)
    e0 = 0 if reverse else TB - 1
    g_last = jnp.concatenate([jnp.broadcast_to(g_cum[bb * TB + e0:bb * TB + e0 + 1], (TB, _GLA_PW))
                              for bb in range(BB)], axis=0)
    q_in = q * jnp.exp(g_cum)
    k_in = k * jnp.exp(-g_cum)
    k_tail = k * jnp.exp(g_last - g_cum)
    dec = jnp.exp(g_last)
    units = [(bb, h) for bb in range(BB) for h in range(_GLA_HEADS)]
    nu = range(len(units))

    def cut(x, un):
        bb, h = un
        return x[bb * TB:(bb + 1) * TB, h * _LANE:(h + 1) * _LANE]

    qh = [cut(q_in, un).astype(BF16) for un in units]
    vh = [cut(v, un).astype(BF16) for un in units]
    st = [ST[bb, h] for bb, h in units]
    att = [_dot_nt(qh[n], cut(k_in, units[n]).astype(BF16)) for n in nu]
    o_inter = [_dot_nt(qh[n], st[n].astype(BF16)) for n in nu]
    kv = [_dot_tn(vh[n], cut(k_tail, units[n]).astype(BF16)) for n in nu]
    o_intra = [_dot(jnp.where(keep, att[n], 0.0).astype(BF16), vh[n]) for n in nu]
    for n in nu:
        bb, h = units[n]
        o_ref[bb, :, h * _LANE:(h + 1) * _LANE] = o_intra[n] + o_inter[n]
        ST[bb, h] = st[n] * cut(dec, units[n])[0:1] + kv[n]

    @pl.when(i == pl.num_programs(1) - 1)
    def _():
        sout_ref[...] = ST[...]


def _gla_call(qkv, lora, s0, aup, ab, col_major, reverse, rows):
    B, T, _ = qkv.shape
    C = _GLA_PW
    nblk = T // _CHUNK
    if col_major:
        assert rows == _CHUNK
    qv = _seq_view(qkv, col_major, rows)
    lv = _seq_view(lora, col_major, rows)
    bb = _scan_bb(B)
    seq = functools.partial(_seq_spec, col_major=col_major, nblk=nblk, reverse=reverse, bb=bb)
    out_shape = qv.shape[:2] + (qv.shape[2] // 3,)
    st_spec = pl.BlockSpec((bb, _GLA_HEADS, _LANE, _LANE), lambda b, i: (b, 0, 0, 0))
    o, s_out = pl.pallas_call(
        functools.partial(_gla_kernel, reverse=reverse),
        grid=(B // bb, nblk),
        in_specs=[seq(C, 3 * C, 0), seq(C, 3 * C, 1), seq(C, 3 * C, 2), seq(_RW_WIDTH, _RW_WIDTH, 0), st_spec,
                  _full_spec((_LANE, C)), _full_spec((1, C))],
        out_specs=[seq(C, C, 0), st_spec],
        out_shape=[jax.ShapeDtypeStruct(out_shape, F32),
                   jax.ShapeDtypeStruct((B, _GLA_HEADS, _LANE, _LANE), F32)],
        scratch_shapes=[pltpu.VMEM((bb, _GLA_HEADS, _LANE, _LANE), F32)],
        compiler_params=_cp(("arbitrary", "arbitrary"), _VMEM_LIMIT),
        name="gla_chunked",
    )(qv, qv, qv, lv, s0, aup, ab)
    return o.reshape(B, T, C), s_out


def _fourier_kernel(x_ref, cg_ref, sg_ref, w_ref, o_ref, xcs_ref):
    T = x_ref.shape[1]

    @pl.when(pl.program_id(1) == 0)
    def _():
        x = x_ref[0]
        xcs_ref[0:T, :] = _dot3(x, cg_ref[...]).astype(BF16)
        xcs_ref[T:2 * T, :] = _dot3(x, sg_ref[...]).astype(BF16)

    o_ref[0] = _dot(w_ref[...], xcs_ref[...])


def _fourier_call(zpost, cg, sg, wt):
    B, T, _ = zpost.shape
    C = _FN_WIDTH
    tm = _tile(T, 512, 8)
    return pl.pallas_call(
        _fourier_kernel,
        grid=(B, T // tm),
        in_specs=[pl.BlockSpec((1, T, C), lambda b, i: (b, 0, 0)),
                  _full_spec((C, C)), _full_spec((C, C)),
                  pl.BlockSpec((tm, 2 * T), lambda b, i: (i, 0))],
        out_specs=pl.BlockSpec((1, tm, C), lambda b, i: (b, i, 0)),
        out_shape=jax.ShapeDtypeStruct((B, T, C), F32),
        scratch_shapes=[pltpu.VMEM((2 * T, C), BF16)],
        compiler_params=_cp(("arbitrary", "arbitrary"), _VMEM_LIMIT),
        name="fnet_dft",
    )(zpost, cg, sg, wt)


def _merge_kernel(rwf_ref, rwb_ref, bnf_ref, bnb_ref, glf_ref, glb_ref, zp_ref, fn_ref, h_ref, g1_ref, sc_ref, sh_ref,
                  nw_ref, wg_ref, jr_ref, lnw_ref, lnb_ref, gup_ref, jg_ref, gnw_ref, pa_ref, pb_ref, pc_ref, wo_ref,
                  o_ref):
    D = h_ref.shape[2]
    zp = zp_ref[0]
    og = zp[:, _FN_WIDTH:_FN_WIDTH + _GLA_PW]
    gd = zp[:, _FN_WIDTH + _GLA_PW:_FN_WIDTH + _GLA_PW + _RW_GATE_RANK]
    o = rwf_ref[0] + rwb_ref[0]
    jr = jr_ref[...]
    mu = _dot_sel(o, jr) * (1.0 / _RW_HD)
    xc = o - mu
    var = _dot_sel(xc * xc, jr) * (1.0 / _RW_HD)
    y = xc * lax.rsqrt(var + _RW_GN_EPS)
    y = y * lnw_ref[...] + lnb_ref[...] + (bnf_ref[0] + bnb_ref[0])
    rw_y = y * _dot3(_sigmoid(gd), gup_ref[...])
    g = glf_ref[0] + glb_ref[0]
    ms = _dot_sel(g * g, jg_ref[...]) * (1.0 / _GLA_DV)
    gla_y = g * lax.rsqrt(ms + _NORM_EPS) * gnw_ref[...] * _silu(og)
    h = h_ref[0]
    u = _modnorm(h, nw_ref[...], sc_ref[0], sh_ref[0]).astype(BF16)
    gates = _sigmoid(_dot(u, wg_ref[...]))
    m = (gates[:, 0:D] * _dot(rw_y.astype(BF16), pa_ref[...])
         + gates[:, D:2 * D] * _dot(gla_y.astype(BF16), pb_ref[...])
         + gates[:, 2 * D:3 * D] * _dot(fn_ref[0].astype(BF16), pc_ref[...]))
    o_ref[0] = h + g1_ref[0] * _dot(m.astype(BF16), wo_ref[...])


def _merge_call(rwf, rwb, bnf, bnb, glf, glb, zpost, fn, h, g1, sc, sh, wts):
    B, T, D = h.shape
    tm = _tile(T, 512, 8)

    def row(c):
        return pl.BlockSpec((1, tm, c), lambda b, i: (b, i, 0))

    acts = [rwf, rwb, bnf, bnb, glf, glb, zpost, fn, h]
    mods = [g1, sc, sh]
    return pl.pallas_call(
        _merge_kernel,
        grid=(B, T // tm),
        in_specs=[row(a.shape[2]) for a in acts] + [pl.BlockSpec((1, 1, D), lambda b, i: (b, 0, 0))] * len(mods)
        + [_full_spec(w.shape) for w in wts],
        out_specs=row(D),
        out_shape=jax.ShapeDtypeStruct((B, T, D), F32),
        compiler_params=_cp(("arbitrary", "arbitrary"), _VMEM_LIMIT),
        name="branch_merge",
    )(*acts, *mods, *wts)


def _router_kernel(h_ref, nw_ref, sc_ref, sh_ref, rt_ref, v_ref, aff_ref):
    u = _modnorm(h_ref[0], nw_ref[...], sc_ref[0], sh_ref[0])
    v_ref[0] = u.astype(BF16)
    logits = _dot_nt(rt_ref[...], u, HI)
    e = jnp.exp(logits - jnp.max(logits, axis=0, keepdims=True))
    aff_ref[0] = e / jnp.sum(e, axis=0, keepdims=True)


def _router_call(h, nw, sc, sh, router_t):
    B, T, D = h.shape
    E = router_t.shape[0]
    tm = _tile(T, 512, 128)
    return pl.pallas_call(
        _router_kernel,
        grid=(B, T // tm),
        in_specs=[pl.BlockSpec((1, tm, D), lambda b, i: (b, i, 0)), _full_spec((1, D)),
                  pl.BlockSpec((1, 1, D), lambda b, i: (b, 0, 0)), pl.BlockSpec((1, 1, D), lambda b, i: (b, 0, 0)),
                  _full_spec((E, D))],
        out_specs=[pl.BlockSpec((1, tm, D), lambda b, i: (b, i, 0)), pl.BlockSpec((1, E, tm), lambda b, i: (b, 0, i))],
        out_shape=[jax.ShapeDtypeStruct((B, T, D), BF16), jax.ShapeDtypeStruct((B, E, T), F32)],
        compiler_params=_cp(("arbitrary", "arbitrary"), _VMEM_LIMIT),
        name="router_softmax",
    )(h, nw, sc, sh, router_t)


def _topc_kernel(aff_ref, pos_ref, cum_ref, *, cap):
    E, T = aff_ref.shape[1], aff_ref.shape[2]
    x = pltpu.bitcast(aff_ref[0], jnp.int32)

    def body(it, thr):
        cand = thr | lax.shift_left(jnp.int32(1), 30 - it)
        cnt = jnp.sum(jnp.where(x >= cand, 1, 0), axis=1, keepdims=True)
        return jnp.where(cnt >= cap, cand, thr)

    thr = lax.fori_loop(0, 31, body, jnp.zeros((E, 1), jnp.int32))
    need = (cap - jnp.sum(jnp.where(x > thr, 1, 0), axis=1, keepdims=True)).astype(F32)
    blk = _LANE
    upper = jnp.where(lax.broadcasted_iota(jnp.int32, (blk, blk), 0) < lax.broadcasted_iota(jnp.int32, (blk, blk), 1),
                      1.0, 0.0).astype(BF16)
    off_eq = jnp.zeros((E, 1), F32)
    off_sel = jnp.zeros((E, 1), F32)
    for c in range(T // blk):
        sl = slice(c * blk, (c + 1) * blk)
        xc = pltpu.bitcast(aff_ref[0, :, sl], jnp.int32)
        eq_c = jnp.where(xc == thr, 1.0, 0.0)
        rank_eq = _dot(eq_c.astype(BF16), upper) + off_eq
        take = jnp.where(rank_eq < need, eq_c, 0.0)
        sel = jnp.where(xc > thr, 1.0, take)
        rank = _dot(sel.astype(BF16), upper) + off_sel
        pos_ref[0, :, sl] = jnp.where(sel > 0.0, rank.astype(jnp.int32), -1)
        off_eq = off_eq + jnp.sum(eq_c, axis=1, keepdims=True)
        off_sel = off_sel + jnp.sum(sel, axis=1, keepdims=True)
    chosen = jnp.where(pos_ref[0] >= 0, 1.0, 0.0).astype(BF16)
    before = jnp.where(lax.shift_right_logical(lax.broadcasted_iota(jnp.int32, (T, blk), 0), blk.bit_length() - 1)
                       < lax.broadcasted_iota(jnp.int32, (T, blk), 1), 1.0, 0.0).astype(BF16)
    cum_ref[0] = _dot(chosen, before).astype(jnp.int32)


def _topc_call(aff, cap):
    B, E, T = aff.shape
    assert T // _LANE < _LANE
    pos, cum = pl.pallas_call(
        functools.partial(_topc_kernel, cap=cap),
        grid=(B,),
        in_specs=[pl.BlockSpec((1, E, T), lambda b: (b, 0, 0))],
        out_specs=[pl.BlockSpec((1, E, T), lambda b: (b, 0, 0)), pl.BlockSpec((1, E, _LANE), lambda b: (b, 0, 0))],
        out_shape=[jax.ShapeDtypeStruct((B, E, T), jnp.int32), jax.ShapeDtypeStruct((B, E, _LANE), jnp.int32)],
        compiler_params=_cp(("arbitrary",), _VMEM_LIMIT),
        name="expert_choice_topc",
    )(aff)
    return pos, cum[:, :, :T // _LANE + 1].reshape(-1)


def _slot_block(cap):
    return min(cap, _LANE)


def _ffn_kernel(cum_ref, v_ref, pos_ref, aff_ref, w1_ref, w3_ref, w2_ref, ys_ref, xs_ref, acc_ref, gate_ref, *, cap, tc,
                expert_major):
    if expert_major:
        e, f, b = pl.program_id(0), pl.program_id(1), pl.program_id(2)
        nf = pl.num_programs(1)
        bi = b
    else:
        b, e, f = pl.program_id(0), pl.program_id(1), pl.program_id(2)
        nf = pl.num_programs(2)
        bi = 0
    T = v_ref.shape[1]
    E = pos_ref.shape[1]
    sbz = _slot_block(cap)
    xs_ref, acc_ref, gate_ref = xs_ref.at[bi], acc_ref.at[bi], gate_ref.at[bi]

    @pl.when(f == 0)
    def _():
        acc_ref[...] = jnp.zeros(acc_ref.shape, F32)
        gate_ref[...] = jnp.zeros(gate_ref.shape, F32)
        base = (b * E + e) * (T // _LANE + 1)
        for c in range(T // tc):
            sl = slice(c * tc, (c + 1) * tc)
            lo = cum_ref[base + c * (tc // _LANE)]
            hi = cum_ref[base + (c + 1) * (tc // _LANE)]
            for sb in range(cap // sbz):
                rows = slice(sb * sbz, (sb + 1) * sbz)

                @pl.when(jnp.logical_and(lo < (sb + 1) * sbz, hi > sb * sbz))
                def _():
                    slot = lax.broadcasted_iota(jnp.int32, (sbz, tc), 0) + sb * sbz
                    hit = slot == pos_ref[0, e, :, sl]
                    acc_ref[rows, :] += _dot(jnp.where(hit, 1.0, 0.0).astype(BF16), v_ref[0, sl, :])
                    gate_ref[rows, :] += jnp.sum(jnp.where(hit, aff_ref[0, e, :, sl], 0.0), axis=1, keepdims=True)
        xs_ref[...] = acc_ref[...].astype(BF16)
        acc_ref[...] = jnp.zeros(acc_ref.shape, F32)

    x = xs_ref[...]
    h1 = _dot(x, w1_ref[0, 0].astype(BF16))
    hid = _silu(h1) * _dot(x, w3_ref[0, 0].astype(BF16))
    acc_ref[...] += _dot(hid.astype(BF16), w2_ref[0, 0].astype(BF16))

    if expert_major:
        ys_ref[0, 0, 0] = (acc_ref[...] * gate_ref[...]).astype(BF16)
    else:
        @pl.when(f == nf - 1)
        def _():
            ys_ref[0, 0] = (acc_ref[...] * gate_ref[...]).astype(BF16)


def _ffn_call(cum, v, pos, aff, w1, w3, w2, layer, cap):
    B, T, D = v.shape
    _, E, _, F = w1.shape
    tc = _tile(T, 512)
    expert_major = T <= 512
    if expert_major:
        fc = _tile(F, 1024)
        grid = (E, F // fc, B)

        def ix(f):
            return lambda e, j, b: f(b, e, j)
    else:
        fc = _tile(F, 512)
        grid = (B, E, F // fc)

        def ix(f):
            return f
    nb = B if expert_major else 1
    if expert_major:
        out_spec = pl.BlockSpec((1, 1, 1, cap, D), lambda e, j, b: (j, b, e, 0, 0))
        out_shape = jax.ShapeDtypeStruct((F // fc, B, E, cap, D), BF16)
    else:
        out_spec = pl.BlockSpec((1, 1, cap, D), lambda b, e, f: (b, e, 0, 0))
        out_shape = jax.ShapeDtypeStruct((B, E, cap, D), BF16)
    ys = pl.pallas_call(
        functools.partial(_ffn_kernel, cap=cap, tc=tc, expert_major=expert_major),
        grid=grid,
        in_specs=[pl.BlockSpec(memory_space=pltpu.SMEM),
                  pl.BlockSpec((1, T, D), ix(lambda b, e, f: (b, 0, 0))),
                  pl.BlockSpec((1, E, 1, T), ix(lambda b, e, f: (b, 0, 0, 0))),
                  pl.BlockSpec((1, E, 1, T), ix(lambda b, e, f: (b, 0, 0, 0))),
                  pl.BlockSpec((1, 1, D, fc), ix(lambda b, e, f: (layer, e, 0, f))),
                  pl.BlockSpec((1, 1, D, fc), ix(lambda b, e, f: (layer, e, 0, f))),
                  pl.BlockSpec((1, 1, fc, D), ix(lambda b, e, f: (layer, e, f, 0)))],
        out_specs=out_spec,
        out_shape=out_shape,
        scratch_shapes=[pltpu.VMEM((nb, cap, D), BF16), pltpu.VMEM((nb, cap, D), F32),
                        pltpu.VMEM((nb, cap, 1), F32)],
        compiler_params=_cp(("arbitrary", "arbitrary", "arbitrary"), _VMEM_LIMIT),
        name="expert_ffn",
    )(cum, v, pos, aff, w1, w3, w2)
    return ys[-1] if expert_major else ys


def _scatter_kernel(pos_ref, ys_ref, h_ref, g2_ref, o_ref, *, cap):
    tm = h_ref.shape[1]
    E = ys_ref.shape[1]
    pos_t = pos_ref[0]
    slot = lax.broadcasted_iota(jnp.int32, (tm, cap), 1)
    group = 4 if E % 4 == 0 and cap % _LANE == 0 else 1
    acc = None
    for e0 in range(0, E, group):
        hit = jnp.concatenate([jnp.where(slot == pos_t[:, e:e + 1], 1.0, 0.0).astype(BF16)
                               for e in range(e0, e0 + group)], axis=1)
        part = _dot(hit, ys_ref[0, e0:e0 + group].reshape(group * cap, ys_ref.shape[3]))
        acc = part if acc is None else acc + part
    o_ref[0] = h_ref[0] + g2_ref[0] * acc


def _scatter_call(pos_t, ys, h, g2, cap):
    B, T, D = h.shape
    E = ys.shape[1]
    tm = _tile(T, 512)
    return pl.pallas_call(
        functools.partial(_scatter_kernel, cap=cap),
        grid=(B, T // tm),
        in_specs=[pl.BlockSpec((1, tm, E), lambda b, i: (b, i, 0)),
                  pl.BlockSpec((1, E, cap, D), lambda b, i: (b, 0, 0, 0)),
                  pl.BlockSpec((1, tm, D), lambda b, i: (b, i, 0)),
                  pl.BlockSpec((1, 1, D), lambda b, i: (b, 0, 0))],
        out_specs=pl.BlockSpec((1, tm, D), lambda b, i: (b, i, 0)),
        out_shape=jax.ShapeDtypeStruct((B, T, D), F32),
        compiler_params=_cp(("arbitrary", "arbitrary"), _VMEM_LIMIT),
        name="expert_scatter",
    )(pos_t, ys, h, g2)


def _final_kernel(h_ref, w_ref, o_ref):
    x = h_ref[0]
    o_ref[0] = x * lax.rsqrt(jnp.mean(x * x, axis=-1, keepdims=True) + _NORM_EPS) * w_ref[...]


def _final_call(h, w):
    B, T, D = h.shape
    tm = _tile(T, 1024, 8)
    return pl.pallas_call(
        _final_kernel,
        grid=(B, T // tm),
        in_specs=[pl.BlockSpec((1, tm, D), lambda b, i: (b, i, 0)), _full_spec((1, D))],
        out_specs=pl.BlockSpec((1, tm, D), lambda b, i: (b, i, 0)),
        out_shape=jax.ShapeDtypeStruct((B, T, D), F32),
        compiler_params=_cp(("arbitrary", "arbitrary"), _VMEM_LIMIT),
        name="final_norm",
    )(h, w)


def _pad_heads(x, nh, d, axis=-1):
    axis = axis % x.ndim
    shp = x.shape
    x = x.reshape(shp[:axis] + (nh, d) + shp[axis + 1:])
    pad = [(0, 0)] * x.ndim
    pad[axis + 1] = (0, _LANE - d)
    x = jnp.pad(x, pad)
    return x.reshape(shp[:axis] + (nh * _LANE,) + shp[axis + 1:])


def _pad_to(x, n, axis=-1):
    axis = axis % x.ndim
    pad = [(0, 0)] * x.ndim
    pad[axis] = (0, n - x.shape[axis])
    return jnp.pad(x, pad)


def _block_ones(n, blk):
    i = np.arange(n) // blk
    return jnp.asarray((i[:, None] == i[None, :]).astype(np.float32))


def _dftmat_kernel(c1_ref, s1_ref, c2_ref, s2_ref, o_ref, *, T, s):
    n1 = T // s
    rep = jnp.where(lax.shift_right_logical(lax.broadcasted_iota(jnp.int32, (n1, T), 1), s.bit_length() - 1)
                    == lax.broadcasted_iota(jnp.int32, (n1, T), 0), 1.0, 0.0)
    til = jnp.where(jnp.bitwise_and(lax.broadcasted_iota(jnp.int32, (s, T), 1), s - 1)
                    == lax.broadcasted_iota(jnp.int32, (s, T), 0), 1.0, 0.0)
    c1 = _dot_sel(c1_ref[...], rep)
    s1 = _dot_sel(s1_ref[...], rep)
    c2 = _dot_sel(c2_ref[...], til)
    s2 = _dot_sel(s2_ref[...], til)
    o_ref[:, 0:T] = (c1 * c2 - s1 * s2).astype(BF16)
    o_ref[:, T:2 * T] = (-(s1 * c2 + c1 * s2)).astype(BF16)


def _dft_consts(T):
    s = 1 << ((T.bit_length() - 1) // 2)
    i = np.arange(T)[:, None]
    a1 = 2.0 * np.pi * ((i * s * np.arange(T // s)[None, :]) % T) / T
    a2 = 2.0 * np.pi * ((i * np.arange(s)[None, :]) % T) / T
    tabs = [jnp.asarray(f(a) / math.sqrt(math.sqrt(T)), F32)
            for a, f in ((a1, np.cos), (a1, np.sin), (a2, np.cos), (a2, np.sin))]
    tm = _tile(T, 256, 8)
    wt = pl.pallas_call(
        functools.partial(_dftmat_kernel, T=T, s=s),
        grid=(T // tm,),
        in_specs=[pl.BlockSpec((tm, t.shape[1]), lambda i: (i, 0)) for t in tabs],
        out_specs=pl.BlockSpec((tm, 2 * T), lambda i: (i, 0)),
        out_shape=jax.ShapeDtypeStruct((T, 2 * T), BF16),
        compiler_params=_cp(("arbitrary",), _VMEM_LIMIT),
        name="dft_matrix",
    )(*tabs)
    g = np.arange(_FN_GD)
    ang_g = 2.0 * np.pi * ((g[:, None] * g[None, :]) % _FN_GD) / _FN_GD
    eye = np.eye(_FN_GROUPS)
    cg = np.kron(eye, np.cos(ang_g)) / math.sqrt(_FN_GD)
    sg = np.kron(eye, np.sin(ang_g)) / math.sqrt(_FN_GD)
    return wt, jnp.asarray(cg, F32), jnp.asarray(sg, F32)


def _layer_weights(i, w_in, conv_w, rw_w0, rw_w_up, rw_a0, rw_a_up, rw_k_k, rw_k_a, rw_r_k, rw_g_up, rw_ln_w,
                   rw_ln_b, gla_a_up, gla_a_b, gla_norm_w, proj_a, proj_b, proj_c, w_out):
    D = w_in.shape[1]
    rw, kw, vw = _RW_WIDTH, _GLA_HEADS * _GLA_DK, _GLA_HEADS * _GLA_DV
    o_gq = 3 * rw
    o_gk = o_gq + kw
    o_gv = o_gk + kw
    o_wd = o_gv + vw
    o_ad = o_wd + 2 * _RW_LORA
    o_ga = o_ad + 2 * _RW_LORA
    o_gd = o_ga + 2 * _GLA_RANK
    o_og = o_gd + _RW_GATE_RANK
    o_fn = o_og + vw
    o_gt = o_fn + _FN_WIDTH
    wi = w_in[i]
    cw = conv_w[i].reshape(9, -1)

    def gla_cols(x):
        return jnp.concatenate([_pad_heads(x[..., o_gq:o_gk], _GLA_HEADS, _GLA_DK),
                                _pad_heads(x[..., o_gk:o_gv], _GLA_HEADS, _GLA_DK),
                                _pad_heads(x[..., o_gv:o_wd], _GLA_HEADS, _GLA_DV)], axis=-1)

    w = {}
    w['in_rw'] = wi[:, 0:o_gq].astype(BF16)
    w['in_gla'] = gla_cols(wi).astype(BF16)
    w['in_lora'] = jnp.concatenate([wi[:, o_wd:o_ga], _pad_to(wi[:, o_ga:o_gd], _LANE)], axis=-1).astype(BF16)
    w['in_post'] = jnp.concatenate([wi[:, o_fn:o_gt], _pad_heads(wi[:, o_og:o_fn], _GLA_HEADS, _GLA_DV),
                                    wi[:, o_gd:o_og]], axis=-1).astype(BF16)
    w['in_gates'] = wi[:, o_gt:].astype(BF16)
    w['conv_rw'] = cw[:, 0:o_gq]
    w['conv_gla'] = gla_cols(cw)
    zl = jnp.zeros((_RW_LORA, rw), F32)
    w['rwkv'] = []
    w['gla'] = []
    jr = _block_ones(rw, _RW_HD)
    for d in range(2):
        wup = jnp.concatenate([rw_w_up[i, d], zl] if d == 0 else [zl, rw_w_up[i, d]], axis=0)
        aup = jnp.concatenate([rw_a_up[i, d], zl] if d == 0 else [zl, rw_a_up[i, d]], axis=0)
        w['rwkv'].append((rw_w0[i, d][None], wup, rw_a0[i, d][None], aup, rw_k_k[i][None], rw_k_a[i][None],
                          rw_r_k[i].reshape(1, rw), jr))
        ga = _pad_heads(gla_a_up[i, d], _GLA_HEADS, _GLA_DK)
        ga = jnp.pad(ga, ((d * _GLA_RANK, _LANE - (d + 1) * _GLA_RANK), (0, 0)))
        w['gla'].append((ga, _pad_heads(gla_a_b[i, d][None], _GLA_HEADS, _GLA_DK)))
    w['merge'] = (jr, rw_ln_w[i][None], rw_ln_b[i][None], rw_g_up[i], _block_ones(_GLA_PW, _LANE),
                  _pad_heads(jnp.tile(gla_norm_w[i], _GLA_HEADS)[None], _GLA_HEADS, _GLA_DV),
                  proj_a[i].astype(BF16), _pad_heads(proj_b[i], _GLA_HEADS, _GLA_DV, axis=0).astype(BF16),
                  proj_c[i].astype(BF16), w_out[i].astype(BF16))
    return w


def _moe(h, nw, sc, sh, g2, router_t, w1, w3, w2, layer):
    B, T, D = h.shape
    E = router_t.shape[0]
    cap = 2 * T // E
    v, aff = _router_call(h, nw, sc, sh, router_t)
    pos, cum = _topc_call(aff, cap)
    ys = _ffn_call(cum, v, pos.reshape(B, E, 1, T), aff.reshape(B, E, 1, T), w1, w3, w2, layer, cap)
    return _scatter_call(jnp.swapaxes(pos, 1, 2), ys, h, g2, cap)


def kernel(x, c, ctx, c_ctx, w_mod, b_mod, norm1_w, norm2_w, w_in, conv_w, rw_w0, rw_w_up, rw_a0, rw_a_up, rw_k_k, rw_k_a, rw_r_k, rw_g_up, rw_ln_w, rw_ln_b, gla_a_up, gla_a_b, gla_norm_w, proj_a, proj_b, proj_c, w_out, router, exp_w1, exp_w3, exp_w2, final_norm_w):
    B, S, D = x.shape
    TC = ctx.shape[1]
    L = w_mod.shape[0]
    W = _GRID_W
    rows = S // W
    assert S % _CHUNK == 0 and TC % _CHUNK == 0

    cc = _pad_to(jnp.concatenate([c, c_ctx[None]], axis=0), 16, axis=0)
    mods = _mod_call(cc, w_mod, b_mod)
    dft = {S: _dft_consts(S), TC: _dft_consts(TC)}
    s0_rw = jnp.zeros((B, _RW_HEADS // 2, _LANE, _LANE), F32)
    s0_gla = jnp.zeros((B, _GLA_HEADS, _LANE, _LANE), F32)

    h_lat, h_ctx = x, ctx
    for i in range(L):
        col_major = i % 2 == 1
        need_ctx = i < L - 1
        lw = _layer_weights(i, w_in, conv_w, rw_w0, rw_w_up, rw_a0, rw_a_up, rw_k_k, rw_k_a, rw_r_k, rw_g_up,
                            rw_ln_w, rw_ln_b, gla_a_up, gla_a_b, gla_norm_w, proj_a, proj_b, proj_c, w_out)
        m_lat = mods[i, :B].reshape(B, 1, _N_MOD, D)
        m_ctx = jnp.broadcast_to(mods[i, B:B + 1].reshape(1, 1, _N_MOD, D), (B, 1, _N_MOD, D))
        n1 = norm1_w[i][None]
        n2 = norm2_w[i][None]

        def in_proj(h, m, img_rows, img_w):
            z = {k: _norm_matmul(h, n1, m[:, :, 1], m[:, :, 0], lw['in_' + k])
                 for k in ('rw', 'gla', 'lora', 'post')}
            z['rw'] = _conv_call(z['rw'], lw['conv_rw'], img_rows, img_w)
            z['gla'] = _conv_call(z['gla'], lw['conv_gla'], img_rows, img_w)
            return z

        z_lat = in_proj(h_lat, m_lat, rows, W)
        z_ctx = in_proj(h_ctx, m_ctx, 1, TC)

        rw_lat, rw_ctx, gl_lat, gl_ctx = [], [], [], []
        for d in range(2):
            rev = d == 1
            o_c, b_c, st = _rwkv_call(z_ctx['rw'], z_ctx['lora'], s0_rw, lw['rwkv'][d], False, rev, 1)
            o_l, b_l, _ = _rwkv_call(z_lat['rw'], z_lat['lora'], st, lw['rwkv'][d], col_major, rev, rows)
            rw_ctx.append((o_c, b_c))
            rw_lat.append((o_l, b_l))
            g_c, st = _gla_call(z_ctx['gla'], z_ctx['lora'], s0_gla, *lw['gla'][d], False, rev, 1)
            g_l, _ = _gla_call(z_lat['gla'], z_lat['lora'], st, *lw['gla'][d], col_major, rev, rows)
            gl_ctx.append(g_c)
            gl_lat.append(g_l)

        def mix(h, m, z, rwo, glo, T):
            wt, cg, sg = dft[T]
            fn = _fourier_call(z['post'], cg, sg, wt)
            return _merge_call(rwo[0][0], rwo[1][0], rwo[0][1], rwo[1][1], glo[0], glo[1], z['post'], fn, h,
                               m[:, :, 2], m[:, :, 1], m[:, :, 0], (n1, lw['in_gates']) + lw['merge'])

        router_t = router[i].T
        h_lat = mix(h_lat, m_lat, z_lat, rw_lat, gl_lat, S)
        h_lat = _moe(h_lat, n2, m_lat[:, :, 4], m_lat[:, :, 3], m_lat[:, :, 5], router_t, exp_w1, exp_w3, exp_w2, i)
        if need_ctx:
            h_ctx = mix(h_ctx, m_ctx, z_ctx, rw_ctx, gl_ctx, TC)
            h_ctx = _moe(h_ctx, n2, m_ctx[:, :, 4], m_ctx[:, :, 3], m_ctx[:, :, 5], router_t, exp_w1, exp_w3,
                         exp_w2, i)
    return _final_call(h_lat, final_norm_w[None])
```

```python
import functools
import math

import jax
import jax.numpy as jnp
import numpy as np
from jax import lax
from jax.experimental import pallas as pl
from jax.experimental.pallas import tpu as pltpu

F32 = jnp.float32
BF16 = jnp.bfloat16
HI = lax.Precision.HIGHEST

_GRID_W = 64
_NORM_EPS = 1e-6
_N_MOD = 6
_RW_HEADS = 6
_RW_HD = 64
_RW_WIDTH = _RW_HEADS * _RW_HD
_RW_LORA = 64
_RW_GATE_RANK = 128
_RW_GN_EPS = 64e-5
_GLA_HEADS = 4
_GLA_DK = 48
_GLA_DV = 96
_GLA_RANK = 16
_GLA_TAU = 16.0
_CHUNK = 64
_FN_GROUPS = 4
_FN_GD = 64
_FN_WIDTH = _FN_GROUPS * _FN_GD
_LANE = 128
_GLA_PW = _GLA_HEADS * _LANE
_VMEM_LIMIT = 56 * 1024 * 1024


def _cp(sem, vmem=None):
    return pltpu.CompilerParams(dimension_semantics=sem, vmem_limit_bytes=vmem)


def _tile(n, cap, mult=128):
    if n <= cap:
        return n
    best = None
    for t in range(mult, cap + 1, mult):
        if n % t == 0:
            best = t
    assert best is not None, (n, cap)
    return best


def _sigmoid(x):
    return 1.0 / (1.0 + jnp.exp(-x))


def _silu(x):
    return x * _sigmoid(x)


def _softplus(x):
    return jnp.maximum(x, 0.0) + jnp.log(1.0 + jnp.exp(-jnp.abs(x)))


def _dot(a, b, precision=None):
    return jnp.dot(a, b, preferred_element_type=F32, precision=precision)


def _dot_nt(a, b, precision=None):
    return lax.dot_general(a, b, (((1,), (1,)), ((), ())), preferred_element_type=F32, precision=precision)


def _dot_tn(a, b, precision=None):
    return lax.dot_general(a, b, (((0,), (0,)), ((), ())), preferred_element_type=F32, precision=precision)


def _mod_kernel(c_ref, w_ref, b_ref, o_ref):
    o_ref[0] = _dot(_silu(c_ref[...]), w_ref[0], HI) + b_ref[0]


def _mod_call(cc, w_mod, b_mod):
    L, D, N = w_mod.shape
    R = cc.shape[0]
    tn = _tile(N, 1536)
    return pl.pallas_call(
        _mod_kernel,
        grid=(L, N // tn),
        in_specs=[pl.BlockSpec((R, D), lambda l, j: (0, 0)),
                  pl.BlockSpec((1, D, tn), lambda l, j: (l, 0, j)),
                  pl.BlockSpec((1, 1, tn), lambda l, j: (l, 0, j))],
        out_specs=pl.BlockSpec((1, R, tn), lambda l, j: (l, 0, j)),
        out_shape=jax.ShapeDtypeStruct((L, R, N), F32),
        compiler_params=_cp(("arbitrary", "arbitrary"), _VMEM_LIMIT),
        name="adaln_mod",
    )(cc, w_mod, b_mod.reshape(L, 1, N))


def _modnorm(x, nw, sc, sh):
    y = x * lax.rsqrt(jnp.mean(x * x, axis=-1, keepdims=True) + _NORM_EPS)
    return (y * nw) * (1.0 + sc) + sh


def _nm_kernel(h_ref, nw_ref, sc_ref, sh_ref, w_ref, o_ref, u_ref):
    @pl.when(pl.program_id(2) == 0)
    def _():
        u_ref[...] = _modnorm(h_ref[0], nw_ref[...], sc_ref[0], sh_ref[0]).astype(BF16)

    o_ref[0] = _dot(u_ref[...], w_ref[...])


def _norm_matmul(h, nw, sc, sh, w):
    B, T, D = h.shape
    N = w.shape[1]
    tm = _tile(T, 1024, 8)
    tn = _tile(N, 1536)
    return pl.pallas_call(
        _nm_kernel,
        grid=(B, T // tm, N // tn),
        in_specs=[pl.BlockSpec((1, tm, D), lambda b, i, j: (b, i, 0)),
                  pl.BlockSpec((1, D), lambda b, i, j: (0, 0)),
                  pl.BlockSpec((1, 1, D), lambda b, i, j: (b, 0, 0)),
                  pl.BlockSpec((1, 1, D), lambda b, i, j: (b, 0, 0)),
                  pl.BlockSpec((D, tn), lambda b, i, j: (0, j))],
        out_specs=pl.BlockSpec((1, tm, tn), lambda b, i, j: (b, i, j)),
        out_shape=jax.ShapeDtypeStruct((B, T, N), F32),
        scratch_shapes=[pltpu.VMEM((tm, D), BF16)],
        compiler_params=_cp(("arbitrary", "arbitrary", "arbitrary"), _VMEM_LIMIT),
        name="norm_in_proj",
    )(h, nw, sc, sh, w)


def _conv_kernel(x_ref, w_ref, o_ref, xp_ref, *, rows, W, T, PAD, CH):
    cw = x_ref.shape[2]
    xp_ref[0:PAD, :] = jnp.zeros((PAD, cw), F32)
    xp_ref[PAD + T:PAD + T + PAD, :] = jnp.zeros((PAD, cw), F32)
    xp_ref[PAD:PAD + T, :] = x_ref[0]
    for c0 in range(0, T, CH):
        col = jnp.bitwise_and(lax.broadcasted_iota(jnp.int32, (CH, cw), 0) + c0, W - 1)
        acc = jnp.zeros((CH, cw), F32)
        for b in range(3):
            part = None
            for a in range(3):
                if rows == 1 and a != 1:
                    continue
                off = (a - 1) * W + (b - 1)
                term = xp_ref[PAD + c0 + off:PAD + c0 + off + CH, :] * w_ref[a * 3 + b:a * 3 + b + 1, :]
                part = term if part is None else part + term
            if b == 0:
                part = jnp.where(col >= 1, part, 0.0)
            elif b == 2:
                part = jnp.where(col <= W - 2, part, 0.0)
            acc = acc + part
        o_ref[0, c0:c0 + CH, :] = acc


def _conv_call(z, w9, rows, W):
    B, T, C = z.shape
    assert W & (W - 1) == 0 and rows * W == T
    cw = _LANE
    PAD = W + 8 if rows > 1 else 8
    CH = min(T, 128)
    kern = functools.partial(_conv_kernel, rows=rows, W=W, T=T, PAD=PAD, CH=CH)
    return pl.pallas_call(
        kern,
        grid=(B, C // cw),
        in_specs=[pl.BlockSpec((1, T, cw), lambda b, j: (b, 0, j)),
                  pl.BlockSpec((9, cw), lambda b, j: (0, j))],
        out_specs=pl.BlockSpec((1, T, cw), lambda b, j: (b, 0, j)),
        out_shape=jax.ShapeDtypeStruct((B, T, C), F32),
        scratch_shapes=[pltpu.VMEM((T + 2 * PAD, cw), F32)],
        compiler_params=_cp(("arbitrary", "arbitrary"), _VMEM_LIMIT),
        name="short_conv",
    )(z, w9)


def _seq_view(arr, col_major, rows):
    B, T, C = arr.shape
    return arr.reshape(B, rows, (T // rows) * C) if col_major else arr


def _seq_spec(C, ctot, part, col_major, nblk, reverse, bb):
    nper = ctot // C

    def blk(i):
        return nblk - 1 - i if reverse else i

    if col_major:
        return pl.BlockSpec((bb, _CHUNK, C), lambda b, i: (b, 0, blk(i) * nper + part))
    return pl.BlockSpec((bb, _CHUNK, C), lambda b, i: (b, blk(i), part))


def _scan_bb(B):
    return 4 if B % 4 == 0 else (2 if B % 2 == 0 else 1)


def _chunk_tri(n, tb, reverse):
    row = lax.broadcasted_iota(jnp.int32, (n, n), 0)
    col = lax.broadcasted_iota(jnp.int32, (n, n), 1)
    sh = tb.bit_length() - 1
    same = lax.shift_right_logical(row, sh) == lax.shift_right_logical(col, sh)
    order = (row <= col) if reverse else (row >= col)
    return jnp.where(jnp.logical_and(same, order), 1.0, 0.0)


def _full_spec(shape):
    nd = len(shape)
    return pl.BlockSpec(shape, lambda b, i: (0,) * nd)


def _split(x):
    hi = x.astype(BF16)
    return hi, (x - hi.astype(F32)).astype(BF16)


def _dot3(a, b, nt=False):
    f = _dot_nt if nt else _dot
    ah, al = _split(a)
    bh, bl = _split(b)
    return f(ah, bh) + (f(ah, bl) + f(al, bh))


def _dot_sel(a, sel):
    sel = sel.astype(BF16)
    a1 = a.astype(BF16)
    r1 = a - a1.astype(F32)
    a2 = r1.astype(BF16)
    a3 = (r1 - a2.astype(F32)).astype(BF16)
    return _dot(a1, sel) + (_dot(a2, sel) + _dot(a3, sel))


def _sel_dot(sel, b):
    sel = sel.astype(BF16)
    b1 = b.astype(BF16)
    r1 = b - b1.astype(F32)
    b2 = r1.astype(BF16)
    b3 = (r1 - b2.astype(F32)).astype(BF16)
    return _dot(sel, b1) + (_dot(sel, b2) + _dot(sel, b3))


def _rwkv_kernel(r_ref, k_ref, v_ref, lo_ref, s0_ref, w0_ref, wup_ref, a0_ref, aup_ref, kkw_ref, kaw_ref,
                 j_ref, o_ref, sout_ref, ZT, *, reverse):
    i = pl.program_id(1)
    TB = _CHUNK
    NP = _RW_HEADS // 2
    BB = r_ref.shape[0]
    N = BB * TB

    @pl.when(i == 0)
    def _():
        ZT[...] = s0_ref[...]

    r = r_ref[...].reshape(N, _RW_WIDTH)
    k = k_ref[...].reshape(N, _RW_WIDTH)
    v = v_ref[...].reshape(N, _RW_WIDTH)
    lo = lo_ref[...].reshape(N, _RW_WIDTH)
    jm = j_ref[...]
    w_log = -_softplus(-(w0_ref[...] + _dot3(jnp.tanh(lo[:, 0:_LANE]), wup_ref[...]))) - 0.5
    a = _sigmoid(a0_ref[...] + _dot3(lo[:, _LANE:2 * _LANE], aup_ref[...]))
    kk0 = k * kkw_ref[...]
    kk = kk0 / jnp.maximum(jnp.sqrt(_dot_sel(kk0 * kk0, jm)), 1e-12)
    km = k * (1.0 + (a - 1.0) * kaw_ref[...])

    lw = -jnp.exp(w_log)
    g = _sel_dot(_chunk_tri(N, TB, reverse), lw)
    e0 = 0 if reverse else TB - 1
    g_end = jnp.concatenate([jnp.broadcast_to(g[bb * TB + e0:bb * TB + e0 + 1], (TB, _RW_WIDTH))
                             for bb in range(BB)], axis=0)
    pm = kk * jnp.exp(g - lw)
    qm = (kk * a) * jnp.exp(-g)
    khm = km * jnp.exp(-g)
    rhm = r * jnp.exp(g)
    qgm = (kk * a) * jnp.exp(g_end - g)
    kgm = km * jnp.exp(g_end - g)
    gam = jnp.exp(g_end)

    t_i = lax.broadcasted_iota(jnp.int32, (TB, _LANE), 0)
    lane = lax.broadcasted_iota(jnp.int32, (TB, _LANE), 1)
    s_i = jnp.bitwise_and(lane, _RW_HD - 1)
    m_a = lane < _RW_HD
    strict = (s_i > t_i) if reverse else (s_i < t_i)
    incl = (s_i >= t_i) if reverse else (s_i <= t_i)
    eye = jnp.where(s_i == t_i, 1.0, 0.0)
    same = ((lax.broadcasted_iota(jnp.int32, (_LANE, _LANE), 0) < _RW_HD)
            == (lax.broadcasted_iota(jnp.int32, (_LANE, _LANE), 1) < _RW_HD))

    def bd(x):
        return jnp.concatenate([jnp.where(m_a, x, 0.0), jnp.where(m_a, 0.0, x)], axis=0)

    units = [(bb, p) for bb in range(BB) for p in range(NP)]
    nu = range(len(units))

    def cut(x, un):
        bb, p = un
        return x[bb * TB:(bb + 1) * TB, p * _LANE:(p + 1) * _LANE]

    lo_h, hi_h = slice(0, _LANE), slice(_LANE, 2 * _LANE)
    gram = [_dot3(jnp.concatenate([cut(pm, un), cut(rhm, un)], axis=0),
                  jnp.concatenate([bd(cut(qm, un)), bd(cut(khm, un))], axis=0), nt=True) for un in units]
    l_pq = [jnp.where(strict, x[0:TB, lo_h], 0.0) for x in gram]
    l_pk = [jnp.where(strict, x[0:TB, hi_h], 0.0) for x in gram]
    m_rq = [jnp.where(incl, x[TB:2 * TB, lo_h], 0.0) for x in gram]
    m_rk = [jnp.where(incl, x[TB:2 * TB, hi_h], 0.0) for x in gram]
    wm = [_dot(jnp.concatenate([l_pk[n], m_rk[n]], axis=0).astype(BF16), bd(cut(v, units[n])).astype(BF16))
          for n in nu]
    kv = [_dot_tn(cut(v, un).astype(BF16), cut(kgm, un).astype(BF16)) for un in units]
    s_m = [eye - x for x in l_pq]
    m_m = [_dot3(x, bd(x)) for x in l_pq]
    nlev = TB.bit_length() - 1
    for lev in range(1, nlev):
        if lev < nlev - 1:
            xs = [_dot3(m_m[n], jnp.concatenate([bd(s_m[n]), bd(m_m[n])], axis=1)) for n in nu]
            s_m = [s_m[n] + xs[n][:, lo_h] for n in nu]
            m_m = [x[:, hi_h] for x in xs]
        else:
            s_m = [s_m[n] + _dot3(m_m[n], bd(s_m[n])) for n in nu]
    ta = [_dot3(s_m[n], jnp.concatenate([bd(cut(pm, units[n])), bd(wm[n][0:TB])], axis=1)) for n in nu]
    zt = [ZT[bb, p] for bb, p in units]
    az = [_dot_nt(jnp.concatenate([ta[n][:, lo_h], cut(rhm, units[n])], axis=0).astype(BF16), zt[n].astype(BF16))
          for n in nu]
    u = [az[n][0:TB] + ta[n][:, hi_h] for n in nu]
    mu = [_dot(m_rq[n].astype(BF16), bd(u[n]).astype(BF16)) for n in nu]
    qu = [_dot_tn(u[n].astype(BF16), cut(qgm, units[n]).astype(BF16)) for n in nu]
    for n in nu:
        bb, p = units[n]
        o_ref[bb, :, p * _LANE:(p + 1) * _LANE] = az[n][TB:2 * TB] - mu[n] + wm[n][TB:2 * TB]
        ZT[bb, p] = zt[n] * cut(gam, units[n])[0:1] + jnp.where(same, kv[n] - qu[n], 0.0)

    @pl.when(i == pl.num_programs(1) - 1)
    def _():
        sout_ref[...] = ZT[...]


def _rwkv_call(rkv, lora, s0, wts, col_major, reverse, rows):
    B, T, _ = rkv.shape
    C = _RW_WIDTH
    nblk = T // _CHUNK
    if col_major:
        assert rows == _CHUNK
    rv = _seq_view(rkv, col_major, rows)
    lv = _seq_view(lora, col_major, rows)
    NP = _RW_HEADS // 2
    bb = _scan_bb(B)
    seq = functools.partial(_seq_spec, col_major=col_major, nblk=nblk, reverse=reverse, bb=bb)
    out_shape = rv.shape[:2] + (rv.shape[2] // 3,)
    st_spec = pl.BlockSpec((bb, NP, _LANE, _LANE), lambda b, i: (b, 0, 0, 0))
    o, s_out = pl.pallas_call(
        functools.partial(_rwkv_kernel, reverse=reverse),
        grid=(B // bb, nblk),
        in_specs=[seq(C, 3 * C, 0), seq(C, 3 * C, 1), seq(C, 3 * C, 2), seq(C, C, 0), st_spec,
                  _full_spec((1, C)), _full_spec((2 * _RW_LORA, C)), _full_spec((1, C)),
                  _full_spec((2 * _RW_LORA, C)), _full_spec((1, C)), _full_spec((1, C)),
                  _full_spec((C, C))],
        out_specs=[seq(C, C, 0), st_spec],
        out_shape=[jax.ShapeDtypeStruct(out_shape, F32),
                   jax.ShapeDtypeStruct((B, NP, _LANE, _LANE), F32)],
        scratch_shapes=[pltpu.VMEM((bb, NP, _LANE, _LANE), F32)],
        compiler_params=_cp(("arbitrary", "arbitrary"), _VMEM_LIMIT),
        name="rwkv7_scan",
    )(rv, rv, rv, lv, s0, *wts)
    return o.reshape(B, T, C), s_out


def _gla_kernel(q_ref, k_ref, v_ref, lo_ref, s0_ref, aup_ref, ab_ref, o_ref, sout_ref, ST, *, reverse):
    i = pl.program_id(1)
    TB = _CHUNK

    @pl.when(i == 0)
    def _():
        ST[...] = s0_ref[...]

    BB = q_ref.shape[0]
    N = BB * TB
    q = _silu(q_ref[...].reshape(N, _GLA_PW)) * (_GLA_DK ** -0.5)
    k = _silu(k_ref[...].reshape(N, _GLA_PW))
    v = _silu(v_ref[...].reshape(N, _GLA_PW))
    gad = lo_ref[...].reshape(N, _RW_WIDTH)[:, 2 * _LANE:3 * _LANE]
    log_a = -_softplus(-(_dot3(gad, aup_ref[...]) + ab_ref[...])) / _GLA_TAU
    row = lax.broadcasted_iota(jnp.int32, (TB, TB), 0)
    col = lax.broadcasted_iota(jnp.int32, (TB, TB), 1)
    keep = (row <= col) if reverse else (row >= col)
    g_cum = _sel_dot(_chunk_tri(N, TB, reverse), log_a)
    e0 = 0 if reverse else TB - 1
    g_last = jnp.concatenate([jnp.broadcast_to(g_cum[bb * TB + e0:bb * TB + e0 + 1], (TB, _GLA_PW))
                              for bb in range(BB)], axis=0)
    q_in = q * jnp.exp(g_cum)
    k_in = k * jnp.exp(-g_cum)
    k_tail = k * jnp.exp(g_last - g_cum)
    dec = jnp.exp(g_last)
    units = [(bb, h) for bb in range(BB) for h in range(_GLA_HEADS)]
    nu = range(len(units))

    def cut(x, un):
        bb, h = un
        return x[bb * TB:(bb + 1) * TB, h * _LANE:(h + 1) * _LANE]

    qh = [cut(q_in, un).astype(BF16) for un in units]
    vh = [cut(v, un).astype(BF16) for un in units]
    st = [ST[bb, h] for bb, h in units]
    att = [_dot_nt(qh[n], cut(k_in, units[n]).astype(BF16)) for n in nu]
    o_inter = [_dot_nt(qh[n], st[n].astype(BF16)) for n in nu]
    kv = [_dot_tn(vh[n], cut(k_tail, units[n]).astype(BF16)) for n in nu]
    o_intra = [_dot(jnp.where(keep, att[n], 0.0).astype(BF16), vh[n]) for n in nu]
    for n in nu:
        bb, h = units[n]
        o_ref[bb, :, h * _LANE:(h + 1) * _LANE] = o_intra[n] + o_inter[n]
        ST[bb, h] = st[n] * cut(dec, units[n])[0:1] + kv[n]

    @pl.when(i == pl.num_programs(1) - 1)
    def _():
        sout_ref[...] = ST[...]


def _gla_call(qkv, lora, s0, aup, ab, col_major, reverse, rows):
    B, T, _ = qkv.shape
    C = _GLA_PW
    nblk = T // _CHUNK
    if col_major:
        assert rows == _CHUNK
    qv = _seq_view(qkv, col_major, rows)
    lv = _seq_view(lora, col_major, rows)
    bb = _scan_bb(B)
    seq = functools.partial(_seq_spec, col_major=col_major, nblk=nblk, reverse=reverse, bb=bb)
    out_shape = qv.shape[:2] + (qv.shape[2] // 3,)
    st_spec = pl.BlockSpec((bb, _GLA_HEADS, _LANE, _LANE), lambda b, i: (b, 0, 0, 0))
    o, s_out = pl.pallas_call(
        functools.partial(_gla_kernel, reverse=reverse),
        grid=(B // bb, nblk),
        in_specs=[seq(C, 3 * C, 0), seq(C, 3 * C, 1), seq(C, 3 * C, 2), seq(_RW_WIDTH, _RW_WIDTH, 0), st_spec,
                  _full_spec((_LANE, C)), _full_spec((1, C))],
        out_specs=[seq(C, C, 0), st_spec],
        out_shape=[jax.ShapeDtypeStruct(out_shape, F32),
                   jax.ShapeDtypeStruct((B, _GLA_HEADS, _LANE, _LANE), F32)],
        scratch_shapes=[pltpu.VMEM((bb, _GLA_HEADS, _LANE, _LANE), F32)],
        compiler_params=_cp(("arbitrary", "arbitrary"), _VMEM_LIMIT),
        name="gla_chunked",
    )(qv, qv, qv, lv, s0, aup, ab)
    return o.reshape(B, T, C), s_out


def _fourier_kernel(x_ref, cg_ref, sg_ref, w_ref, o_ref, xcs_ref):
    T = x_ref.shape[1]

    @pl.when(pl.program_id(1) == 0)
    def _():
        x = x_ref[0]
        xcs_ref[0:T, :] = _dot3(x, cg_ref[...]).astype(BF16)
        xcs_ref[T:2 * T, :] = _dot3(x, sg_ref[...]).astype(BF16)

    o_ref[0] = _dot(w_ref[...], xcs_ref[...])


def _fourier_call(zpost, cg, sg, wt):
    B, T, _ = zpost.shape
    C = _FN_WIDTH
    tm = _tile(T, 512, 8)
    return pl.pallas_call(
        _fourier_kernel,
        grid=(B, T // tm),
        in_specs=[pl.BlockSpec((1, T, C), lambda b, i: (b, 0, 0)),
                  _full_spec((C, C)), _full_spec((C, C)),
                  pl.BlockSpec((tm, 2 * T), lambda b, i: (i, 0))],
        out_specs=pl.BlockSpec((1, tm, C), lambda b, i: (b, i, 0)),
        out_shape=jax.ShapeDtypeStruct((B, T, C), F32),
        scratch_shapes=[pltpu.VMEM((2 * T, C), BF16)],
        compiler_params=_cp(("arbitrary", "arbitrary"), _VMEM_LIMIT),
        name="fnet_dft",
    )(zpost, cg, sg, wt)


def _merge_kernel(rwf_ref, rwb_ref, glf_ref, glb_ref, zp_ref, fn_ref, h_ref, zr_ref, zl_ref, g1_ref, sc_ref, sh_ref,
                  nw_ref, wg_ref, jr_ref, lnw_ref, lnb_ref, gup_ref, jg_ref, gnw_ref, pa_ref, pb_ref, pc_ref, wo_ref,
                  a0f_ref, aupf_ref, a0b_ref, aupb_ref, kaw_ref, rkw_ref, o_ref):
    D = h_ref.shape[2]
    zp = zp_ref[0]
    og = zp[:, _FN_WIDTH:_FN_WIDTH + _GLA_PW]
    gd = zp[:, _FN_WIDTH + _GLA_PW:_FN_WIDTH + _GLA_PW + _RW_GATE_RANK]
    jr = jr_ref[...]
    zr = zr_ref[0]
    icl = zl_ref[0][:, _LANE:2 * _LANE]
    a_sum = (_sigmoid(a0f_ref[...] + _dot3(icl, aupf_ref[...])) + _sigmoid(a0b_ref[...] + _dot3(icl, aupb_ref[...])))
    rk = zr[:, 0:_RW_WIDTH] * zr[:, _RW_WIDTH:2 * _RW_WIDTH] * rkw_ref[...]
    bonus = _dot_sel(rk * (2.0 + (a_sum - 2.0) * kaw_ref[...]), jr) * zr[:, 2 * _RW_WIDTH:3 * _RW_WIDTH]
    o = rwf_ref[0] + rwb_ref[0]
    mu = _dot_sel(o, jr) * (1.0 / _RW_HD)
    xc = o - mu
    var = _dot_sel(xc * xc, jr) * (1.0 / _RW_HD)
    y = xc * lax.rsqrt(var + _RW_GN_EPS)
    y = y * lnw_ref[...] + lnb_ref[...] + bonus
    rw_y = y * _dot3(_sigmoid(gd), gup_ref[...])
    g = glf_ref[0] + glb_ref[0]
    ms = _dot_sel(g * g, jg_ref[...]) * (1.0 / _GLA_DV)
    gla_y = g * lax.rsqrt(ms + _NORM_EPS) * gnw_ref[...] * _silu(og)
    h = h_ref[0]
    u = _modnorm(h, nw_ref[...], sc_ref[0], sh_ref[0]).astype(BF16)
    gates = _sigmoid(_dot(u, wg_ref[...]))
    m = (gates[:, 0:D] * _dot(rw_y.astype(BF16), pa_ref[...])
         + gates[:, D:2 * D] * _dot(gla_y.astype(BF16), pb_ref[...])
         + gates[:, 2 * D:3 * D] * _dot(fn_ref[0].astype(BF16), pc_ref[...]))
    o_ref[0] = h + g1_ref[0] * _dot(m.astype(BF16), wo_ref[...])


def _merge_call(rwf, rwb, glf, glb, zpost, fn, h, zrw, zlora, g1, sc, sh, wts):
    B, T, D = h.shape
    tm = _tile(T, 512, 8)

    def row(c):
        return pl.BlockSpec((1, tm, c), lambda b, i: (b, i, 0))

    acts = [rwf, rwb, glf, glb, zpost, fn, h, zrw, zlora]
    mods = [g1, sc, sh]
    return pl.pallas_call(
        _merge_kernel,
        grid=(B, T // tm),
        in_specs=[row(a.shape[2]) for a in acts] + [pl.BlockSpec((1, 1, D), lambda b, i: (b, 0, 0))] * len(mods)
        + [_full_spec(w.shape) for w in wts],
        out_specs=row(D),
        out_shape=jax.ShapeDtypeStruct((B, T, D), F32),
        compiler_params=_cp(("arbitrary", "arbitrary"), _VMEM_LIMIT),
        name="branch_merge",
    )(*acts, *mods, *wts)


def _router_kernel(h_ref, nw_ref, sc_ref, sh_ref, rt_ref, v_ref, aff_ref):
    u = _modnorm(h_ref[0], nw_ref[...], sc_ref[0], sh_ref[0])
    v_ref[0] = u.astype(BF16)
    logits = _dot_nt(rt_ref[...], u, HI)
    e = jnp.exp(logits - jnp.max(logits, axis=0, keepdims=True))
    aff_ref[0] = e / jnp.sum(e, axis=0, keepdims=True)


def _router_call(h, nw, sc, sh, router_t):
    B, T, D = h.shape
    E = router_t.shape[0]
    tm = _tile(T, 512, 128)
    return pl.pallas_call(
        _router_kernel,
        grid=(B, T // tm),
        in_specs=[pl.BlockSpec((1, tm, D), lambda b, i: (b, i, 0)), _full_spec((1, D)),
                  pl.BlockSpec((1, 1, D), lambda b, i: (b, 0, 0)), pl.BlockSpec((1, 1, D), lambda b, i: (b, 0, 0)),
                  _full_spec((E, D))],
        out_specs=[pl.BlockSpec((1, tm, D), lambda b, i: (b, i, 0)), pl.BlockSpec((1, E, tm), lambda b, i: (b, 0, i))],
        out_shape=[jax.ShapeDtypeStruct((B, T, D), BF16), jax.ShapeDtypeStruct((B, E, T), F32)],
        compiler_params=_cp(("arbitrary", "arbitrary"), _VMEM_LIMIT),
        name="router_softmax",
    )(h, nw, sc, sh, router_t)


def _topc_kernel(aff_ref, pos_ref, cum_ref, *, cap):
    E, T = aff_ref.shape[1], aff_ref.shape[2]
    x = pltpu.bitcast(aff_ref[0], jnp.int32)

    def body(it, thr):
        cand = thr | lax.shift_left(jnp.int32(1), 30 - it)
        cnt = jnp.sum(jnp.where(x >= cand, 1, 0), axis=1, keepdims=True)
        return jnp.where(cnt >= cap, cand, thr)

    thr = lax.fori_loop(0, 31, body, jnp.zeros((E, 1), jnp.int32))
    need = (cap - jnp.sum(jnp.where(x > thr, 1, 0), axis=1, keepdims=True)).astype(F32)
    blk = _LANE
    upper = jnp.where(lax.broadcasted_iota(jnp.int32, (blk, blk), 0) < lax.broadcasted_iota(jnp.int32, (blk, blk), 1),
                      1.0, 0.0).astype(BF16)
    off_eq = jnp.zeros((E, 1), F32)
    off_sel = jnp.zeros((E, 1), F32)
    for c in range(T // blk):
        sl = slice(c * blk, (c + 1) * blk)
        xc = pltpu.bitcast(aff_ref[0, :, sl], jnp.int32)
        eq_c = jnp.where(xc == thr, 1.0, 0.0)
        rank_eq = _dot(eq_c.astype(BF16), upper) + off_eq
        take = jnp.where(rank_eq < need, eq_c, 0.0)
        sel = jnp.where(xc > thr, 1.0, take)
        rank = _dot(sel.astype(BF16), upper) + off_sel
        pos_ref[0, :, sl] = jnp.where(sel > 0.0, rank.astype(jnp.int32), -1)
        off_eq = off_eq + jnp.sum(eq_c, axis=1, keepdims=True)
        off_sel = off_sel + jnp.sum(sel, axis=1, keepdims=True)
    chosen = jnp.where(pos_ref[0] >= 0, 1.0, 0.0).astype(BF16)
    before = jnp.where(lax.shift_right_logical(lax.broadcasted_iota(jnp.int32, (T, blk), 0), blk.bit_length() - 1)
                       < lax.broadcasted_iota(jnp.int32, (T, blk), 1), 1.0, 0.0).astype(BF16)
    cum_ref[0] = _dot(chosen, before).astype(jnp.int32)


def _topc_call(aff, cap):
    B, E, T = aff.shape
    assert T // _LANE < _LANE
    pos, cum = pl.pallas_call(
        functools.partial(_topc_kernel, cap=cap),
        grid=(B,),
        in_specs=[pl.BlockSpec((1, E, T), lambda b: (b, 0, 0))],
        out_specs=[pl.BlockSpec((1, E, T), lambda b: (b, 0, 0)), pl.BlockSpec((1, E, _LANE), lambda b: (b, 0, 0))],
        out_shape=[jax.ShapeDtypeStruct((B, E, T), jnp.int32), jax.ShapeDtypeStruct((B, E, _LANE), jnp.int32)],
        compiler_params=_cp(("arbitrary",), _VMEM_LIMIT),
        name="expert_choice_topc",
    )(aff)
    return pos, cum[:, :, :T // _LANE + 1].reshape(-1)


def _slot_block(cap):
    return min(cap, _LANE)


def _ffn_kernel(cum_ref, v_ref, pos_ref, aff_ref, w1_ref, w3_ref, w2_ref, ys_ref, xs_ref, acc_ref, gate_ref, *, cap, tc,
                expert_axis):
    e = pl.program_id(expert_axis)
    b = pl.program_id(1 - expert_axis)
    f = pl.program_id(2)
    T = v_ref.shape[1]
    E = pos_ref.shape[1]
    sbz = _slot_block(cap)

    @pl.when(f == 0)
    def _():
        acc_ref[...] = jnp.zeros(acc_ref.shape, F32)
        gate_ref[...] = jnp.zeros(gate_ref.shape, F32)
        base = (b * E + e) * (T // _LANE + 1)
        for c in range(T // tc):
            sl = slice(c * tc, (c + 1) * tc)
            lo = cum_ref[base + c * (tc // _LANE)]
            hi = cum_ref[base + (c + 1) * (tc // _LANE)]
            for sb in range(cap // sbz):
                rows = slice(sb * sbz, (sb + 1) * sbz)

                @pl.when(jnp.logical_and(lo < (sb + 1) * sbz, hi > sb * sbz))
                def _():
                    slot = lax.broadcasted_iota(jnp.int32, (sbz, tc), 0) + sb * sbz
                    hit = slot == pos_ref[0, e, :, sl]
                    acc_ref[rows, :] += _dot(jnp.where(hit, 1.0, 0.0).astype(BF16), v_ref[0, sl, :])
                    gate_ref[rows, :] += jnp.sum(jnp.where(hit, aff_ref[0, e, :, sl], 0.0), axis=1, keepdims=True)
        xs_ref[...] = acc_ref[...].astype(BF16)
        acc_ref[...] = jnp.zeros(acc_ref.shape, F32)

    x = xs_ref[...]
    h1 = _dot(x, w1_ref[0])
    hid = _silu(h1) * _dot(x, w3_ref[0])
    acc_ref[...] += _dot(hid.astype(BF16), w2_ref[0])

    @pl.when(f == pl.num_programs(2) - 1)
    def _():
        ys_ref[0, 0] = (acc_ref[...] * gate_ref[...]).astype(BF16)


def _ffn_call(cum, v, pos, aff, w1, w3, w2, cap):
    B, T, D = v.shape
    E, _, F = w1.shape
    tc = _tile(T, 512)
    expert_major = T <= 512
    fc = F if expert_major else _tile(F, 1024)
    if expert_major:
        grid = (E, B, 1)

        def ix(f):
            return lambda e, b, j: f(b, e, j)
    else:
        grid = (B, E, F // fc)

        def ix(f):
            return f
    return pl.pallas_call(
        functools.partial(_ffn_kernel, cap=cap, tc=tc, expert_axis=0 if expert_major else 1),
        grid=grid,
        in_specs=[pl.BlockSpec(memory_space=pltpu.SMEM),
                  pl.BlockSpec((1, T, D), ix(lambda b, e, f: (b, 0, 0))),
                  pl.BlockSpec((1, E, 1, T), ix(lambda b, e, f: (b, 0, 0, 0))),
                  pl.BlockSpec((1, E, 1, T), ix(lambda b, e, f: (b, 0, 0, 0))),
                  pl.BlockSpec((1, D, fc), ix(lambda b, e, f: (e, 0, f))),
                  pl.BlockSpec((1, D, fc), ix(lambda b, e, f: (e, 0, f))),
                  pl.BlockSpec((1, fc, D), ix(lambda b, e, f: (e, f, 0)))],
        out_specs=pl.BlockSpec((1, 1, cap, D), ix(lambda b, e, f: (b, e, 0, 0))),
        out_shape=jax.ShapeDtypeStruct((B, E, cap, D), BF16),
        scratch_shapes=[pltpu.VMEM((cap, D), BF16), pltpu.VMEM((cap, D), F32), pltpu.VMEM((cap, 1), F32)],
        compiler_params=_cp(("arbitrary", "arbitrary", "arbitrary"), _VMEM_LIMIT),
        name="expert_ffn",
    )(cum, v, pos, aff, w1, w3, w2)


def _scatter_kernel(pos_ref, ys_ref, h_ref, g2_ref, o_ref, *, cap):
    tm = h_ref.shape[1]
    E = ys_ref.shape[1]
    pos_t = pos_ref[0]
    slot = lax.broadcasted_iota(jnp.int32, (tm, cap), 1)
    group = 4 if E % 4 == 0 and cap % _LANE == 0 else 1
    acc = None
    for e0 in range(0, E, group):
        hit = jnp.concatenate([jnp.where(slot == pos_t[:, e:e + 1], 1.0, 0.0).astype(BF16)
                               for e in range(e0, e0 + group)], axis=1)
        part = _dot(hit, ys_ref[0, e0:e0 + group].reshape(group * cap, ys_ref.shape[3]))
        acc = part if acc is None else acc + part
    o_ref[0] = h_ref[0] + g2_ref[0] * acc


def _scatter_call(pos_t, ys, h, g2, cap):
    B, T, D = h.shape
    E = ys.shape[1]
    tm = _tile(T, 512)
    return pl.pallas_call(
        functools.partial(_scatter_kernel, cap=cap),
        grid=(B, T // tm),
        in_specs=[pl.BlockSpec((1, tm, E), lambda b, i: (b, i, 0)),
                  pl.BlockSpec((1, E, cap, D), lambda b, i: (b, 0, 0, 0)),
                  pl.BlockSpec((1, tm, D), lambda b, i: (b, i, 0)),
                  pl.BlockSpec((1, 1, D), lambda b, i: (b, 0, 0))],
        out_specs=pl.BlockSpec((1, tm, D), lambda b, i: (b, i, 0)),
        out_shape=jax.ShapeDtypeStruct((B, T, D), F32),
        compiler_params=_cp(("arbitrary", "arbitrary"), _VMEM_LIMIT),
        name="expert_scatter",
    )(pos_t, ys, h, g2)


def _final_kernel(h_ref, w_ref, o_ref):
    x = h_ref[0]
    o_ref[0] = x * lax.rsqrt(jnp.mean(x * x, axis=-1, keepdims=True) + _NORM_EPS) * w_ref[...]


def _final_call(h, w):
    B, T, D = h.shape
    tm = _tile(T, 1024, 8)
    return pl.pallas_call(
        _final_kernel,
        grid=(B, T // tm),
        in_specs=[pl.BlockSpec((1, tm, D), lambda b, i: (b, i, 0)), _full_spec((1, D))],
        out_specs=pl.BlockSpec((1, tm, D), lambda b, i: (b, i, 0)),
        out_shape=jax.ShapeDtypeStruct((B, T, D), F32),
        compiler_params=_cp(("arbitrary", "arbitrary"), _VMEM_LIMIT),
        name="final_norm",
    )(h, w)


def _pad_heads(x, nh, d, axis=-1):
    axis = axis % x.ndim
    shp = x.shape
    x = x.reshape(shp[:axis] + (nh, d) + shp[axis + 1:])
    pad = [(0, 0)] * x.ndim
    pad[axis + 1] = (0, _LANE - d)
    x = jnp.pad(x, pad)
    return x.reshape(shp[:axis] + (nh * _LANE,) + shp[axis + 1:])


def _pad_to(x, n, axis=-1):
    axis = axis % x.ndim
    pad = [(0, 0)] * x.ndim
    pad[axis] = (0, n - x.shape[axis])
    return jnp.pad(x, pad)


def _block_ones(n, blk):
    i = np.arange(n) // blk
    return jnp.asarray((i[:, None] == i[None, :]).astype(np.float32))


def _dftmat_kernel(c1_ref, s1_ref, c2_ref, s2_ref, o_ref, *, T, s):
    n1 = T // s
    rep = jnp.where(lax.shift_right_logical(lax.broadcasted_iota(jnp.int32, (n1, T), 1), s.bit_length() - 1)
                    == lax.broadcasted_iota(jnp.int32, (n1, T), 0), 1.0, 0.0)
    til = jnp.where(jnp.bitwise_and(lax.broadcasted_iota(jnp.int32, (s, T), 1), s - 1)
                    == lax.broadcasted_iota(jnp.int32, (s, T), 0), 1.0, 0.0)
    c1 = _dot_sel(c1_ref[...], rep)
    s1 = _dot_sel(s1_ref[...], rep)
    c2 = _dot_sel(c2_ref[...], til)
    s2 = _dot_sel(s2_ref[...], til)
    o_ref[:, 0:T] = (c1 * c2 - s1 * s2).astype(BF16)
    o_ref[:, T:2 * T] = (-(s1 * c2 + c1 * s2)).astype(BF16)


def _dft_consts(T):
    s = 1 << ((T.bit_length() - 1) // 2)
    i = np.arange(T)[:, None]
    a1 = 2.0 * np.pi * ((i * s * np.arange(T // s)[None, :]) % T) / T
    a2 = 2.0 * np.pi * ((i * np.arange(s)[None, :]) % T) / T
    tabs = [jnp.asarray(f(a) / math.sqrt(math.sqrt(T)), F32)
            for a, f in ((a1, np.cos), (a1, np.sin), (a2, np.cos), (a2, np.sin))]
    tm = _tile(T, 256, 8)
    wt = pl.pallas_call(
        functools.partial(_dftmat_kernel, T=T, s=s),
        grid=(T // tm,),
        in_specs=[pl.BlockSpec((tm, t.shape[1]), lambda i: (i, 0)) for t in tabs],
        out_specs=pl.BlockSpec((tm, 2 * T), lambda i: (i, 0)),
        out_shape=jax.ShapeDtypeStruct((T, 2 * T), BF16),
        compiler_params=_cp(("arbitrary",), _VMEM_LIMIT),
        name="dft_matrix",
    )(*tabs)
    g = np.arange(_FN_GD)
    ang_g = 2.0 * np.pi * ((g[:, None] * g[None, :]) % _FN_GD) / _FN_GD
    eye = np.eye(_FN_GROUPS)
    cg = np.kron(eye, np.cos(ang_g)) / math.sqrt(_FN_GD)
    sg = np.kron(eye, np.sin(ang_g)) / math.sqrt(_FN_GD)
    return wt, jnp.asarray(cg, F32), jnp.asarray(sg, F32)


def _layer_weights(i, w_in, conv_w, rw_w0, rw_w_up, rw_a0, rw_a_up, rw_k_k, rw_k_a, rw_r_k, rw_g_up, rw_ln_w,
                   rw_ln_b, gla_a_up, gla_a_b, gla_norm_w, proj_a, proj_b, proj_c, w_out):
    D = w_in.shape[1]
    rw, kw, vw = _RW_WIDTH, _GLA_HEADS * _GLA_DK, _GLA_HEADS * _GLA_DV
    o_gq = 3 * rw
    o_gk = o_gq + kw
    o_gv = o_gk + kw
    o_wd = o_gv + vw
    o_ad = o_wd + 2 * _RW_LORA
    o_ga = o_ad + 2 * _RW_LORA
    o_gd = o_ga + 2 * _GLA_RANK
    o_og = o_gd + _RW_GATE_RANK
    o_fn = o_og + vw
    o_gt = o_fn + _FN_WIDTH
    wi = w_in[i]
    cw = conv_w[i].reshape(9, -1)

    def gla_cols(x):
        return jnp.concatenate([_pad_heads(x[..., o_gq:o_gk], _GLA_HEADS, _GLA_DK),
                                _pad_heads(x[..., o_gk:o_gv], _GLA_HEADS, _GLA_DK),
                                _pad_heads(x[..., o_gv:o_wd], _GLA_HEADS, _GLA_DV)], axis=-1)

    w = {}
    w['in_rw'] = wi[:, 0:o_gq].astype(BF16)
    w['in_gla'] = gla_cols(wi).astype(BF16)
    w['in_lora'] = jnp.concatenate([wi[:, o_wd:o_ga], _pad_to(wi[:, o_ga:o_gd], _LANE)], axis=-1).astype(BF16)
    w['in_post'] = jnp.concatenate([wi[:, o_fn:o_gt], _pad_heads(wi[:, o_og:o_fn], _GLA_HEADS, _GLA_DV),
                                    wi[:, o_gd:o_og]], axis=-1).astype(BF16)
    w['in_gates'] = wi[:, o_gt:].astype(BF16)
    w['conv_rw'] = cw[:, 0:o_gq]
    w['conv_gla'] = gla_cols(cw)
    zl = jnp.zeros((_RW_LORA, rw), F32)
    w['rwkv'] = []
    w['gla'] = []
    jr = _block_ones(rw, _RW_HD)
    for d in range(2):
        wup = jnp.concatenate([rw_w_up[i, d], zl] if d == 0 else [zl, rw_w_up[i, d]], axis=0)
        aup = jnp.concatenate([rw_a_up[i, d], zl] if d == 0 else [zl, rw_a_up[i, d]], axis=0)
        w['rwkv'].append((rw_w0[i, d][None], wup, rw_a0[i, d][None], aup, rw_k_k[i][None], rw_k_a[i][None], jr))
        ga = _pad_heads(gla_a_up[i, d], _GLA_HEADS, _GLA_DK)
        ga = jnp.pad(ga, ((d * _GLA_RANK, _LANE - (d + 1) * _GLA_RANK), (0, 0)))
        w['gla'].append((ga, _pad_heads(gla_a_b[i, d][None], _GLA_HEADS, _GLA_DK)))
    w['merge'] = (jr, rw_ln_w[i][None], rw_ln_b[i][None], rw_g_up[i], _block_ones(_GLA_PW, _LANE),
                  _pad_heads(jnp.tile(gla_norm_w[i], _GLA_HEADS)[None], _GLA_HEADS, _GLA_DV),
                  proj_a[i].astype(BF16), _pad_heads(proj_b[i], _GLA_HEADS, _GLA_DV, axis=0).astype(BF16),
                  proj_c[i].astype(BF16), w_out[i].astype(BF16),
                  w['rwkv'][0][2], w['rwkv'][0][3], w['rwkv'][1][2], w['rwkv'][1][3], rw_k_a[i][None],
                  rw_r_k[i].reshape(1, rw))
    return w


def _moe(h, nw, sc, sh, g2, router_t, w1, w3, w2):
    B, T, D = h.shape
    E = router_t.shape[0]
    cap = 2 * T // E
    v, aff = _router_call(h, nw, sc, sh, router_t)
    pos, cum = _topc_call(aff, cap)
    ys = _ffn_call(cum, v, pos.reshape(B, E, 1, T), aff.reshape(B, E, 1, T), w1, w3, w2, cap)
    return _scatter_call(jnp.swapaxes(pos, 1, 2), ys, h, g2, cap)


def kernel(x, c, ctx, c_ctx, w_mod, b_mod, norm1_w, norm2_w, w_in, conv_w, rw_w0, rw_w_up, rw_a0, rw_a_up, rw_k_k, rw_k_a, rw_r_k, rw_g_up, rw_ln_w, rw_ln_b, gla_a_up, gla_a_b, gla_norm_w, proj_a, proj_b, proj_c, w_out, router, exp_w1, exp_w3, exp_w2, final_norm_w):
    B, S, D = x.shape
    TC = ctx.shape[1]
    L = w_mod.shape[0]
    W = _GRID_W
    rows = S // W
    assert S % _CHUNK == 0 and TC % _CHUNK == 0

    cc = _pad_to(jnp.concatenate([c, c_ctx[None]], axis=0), 16, axis=0)
    mods = _mod_call(cc, w_mod, b_mod)
    dft = {S: _dft_consts(S), TC: _dft_consts(TC)}
    s0_rw = jnp.zeros((B, _RW_HEADS // 2, _LANE, _LANE), F32)
    s0_gla = jnp.zeros((B, _GLA_HEADS, _LANE, _LANE), F32)

    h_lat, h_ctx = x, ctx
    for i in range(L):
        col_major = i % 2 == 1
        need_ctx = i < L - 1
        lw = _layer_weights(i, w_in, conv_w, rw_w0, rw_w_up, rw_a0, rw_a_up, rw_k_k, rw_k_a, rw_r_k, rw_g_up,
                            rw_ln_w, rw_ln_b, gla_a_up, gla_a_b, gla_norm_w, proj_a, proj_b, proj_c, w_out)
        m_lat = mods[i, :B].reshape(B, 1, _N_MOD, D)
        m_ctx = jnp.broadcast_to(mods[i, B:B + 1].reshape(1, 1, _N_MOD, D), (B, 1, _N_MOD, D))
        n1 = norm1_w[i][None]
        n2 = norm2_w[i][None]

        def in_proj(h, m, img_rows, img_w):
            z = {k: _norm_matmul(h, n1, m[:, :, 1], m[:, :, 0], lw['in_' + k])
                 for k in ('rw', 'gla', 'lora', 'post')}
            z['rw'] = _conv_call(z['rw'], lw['conv_rw'], img_rows, img_w)
            z['gla'] = _conv_call(z['gla'], lw['conv_gla'], img_rows, img_w)
            return z

        z_lat = in_proj(h_lat, m_lat, rows, W)
        z_ctx = in_proj(h_ctx, m_ctx, 1, TC)

        rw_lat, rw_ctx, gl_lat, gl_ctx = [], [], [], []
        for d in range(2):
            rev = d == 1
            o_c, st = _rwkv_call(z_ctx['rw'], z_ctx['lora'], s0_rw, lw['rwkv'][d], False, rev, 1)
            o_l, _ = _rwkv_call(z_lat['rw'], z_lat['lora'], st, lw['rwkv'][d], col_major, rev, rows)
            rw_ctx.append(o_c)
            rw_lat.append(o_l)
            g_c, st = _gla_call(z_ctx['gla'], z_ctx['lora'], s0_gla, *lw['gla'][d], False, rev, 1)
            g_l, _ = _gla_call(z_lat['gla'], z_lat['lora'], st, *lw['gla'][d], col_major, rev, rows)
            gl_ctx.append(g_c)
            gl_lat.append(g_l)

        def mix(h, m, z, rwo, glo, T):
            wt, cg, sg = dft[T]
            fn = _fourier_call(z['post'], cg, sg, wt)
            return _merge_call(rwo[0], rwo[1], glo[0], glo[1], z['post'], fn, h, z['rw'], z['lora'],
                               m[:, :, 2], m[:, :, 1], m[:, :, 0], (n1, lw['in_gates']) + lw['merge'])

        router_t = router[i].T
        w1 = exp_w1[i].astype(BF16)
        w3 = exp_w3[i].astype(BF16)
        w2 = exp_w2[i].astype(BF16)
        h_lat = mix(h_lat, m_lat, z_lat, rw_lat, gl_lat, S)
        h_lat = _moe(h_lat, n2, m_lat[:, :, 4], m_lat[:, :, 3], m_lat[:, :, 5], router_t, w1, w3, w2)
        if need_ctx:
            h_ctx = mix(h_ctx, m_ctx, z_ctx, rw_ctx, gl_ctx, TC)
            h_ctx = _moe(h_ctx, n2, m_ctx[:, :, 4], m_ctx[:, :, 3], m_ctx[:, :, 5], router_t, w1, w3, w2)
    return _final_call(h_lat, final_norm_w[None])
```

```python
import functools
import math

import jax
import jax.numpy as jnp
import numpy as np
from jax import lax
from jax.experimental import pallas as pl
from jax.experimental.pallas import tpu as pltpu

F32 = jnp.float32
BF16 = jnp.bfloat16
HI = lax.Precision.HIGHEST

_GRID_W = 64
_NORM_EPS = 1e-6
_N_MOD = 6
_RW_HEADS = 6
_RW_HD = 64
_RW_WIDTH = _RW_HEADS * _RW_HD
_RW_LORA = 64
_RW_GATE_RANK = 128
_RW_GN_EPS = 64e-5
_GLA_HEADS = 4
_GLA_DK = 48
_GLA_DV = 96
_GLA_RANK = 16
_GLA_TAU = 16.0
_CHUNK = 64
_FN_GROUPS = 4
_FN_GD = 64
_FN_WIDTH = _FN_GROUPS * _FN_GD
_LANE = 128
_GLA_PW = _GLA_HEADS * _LANE
_VMEM_LIMIT = 56 * 1024 * 1024


def _cp(sem, vmem=None):
    return pltpu.CompilerParams(dimension_semantics=sem, vmem_limit_bytes=vmem)


def _tile(n, cap, mult=128):
    if n <= cap:
        return n
    best = None
    for t in range(mult, cap + 1, mult):
        if n % t == 0:
            best = t
    assert best is not None, (n, cap)
    return best


def _sigmoid(x):
    return 1.0 / (1.0 + jnp.exp(-x))


def _silu(x):
    return x * _sigmoid(x)


def _softplus(x):
    return jnp.maximum(x, 0.0) + jnp.log(1.0 + jnp.exp(-jnp.abs(x)))


def _dot(a, b, precision=None):
    return jnp.dot(a, b, preferred_element_type=F32, precision=precision)


def _dot_nt(a, b, precision=None):
    return lax.dot_general(a, b, (((1,), (1,)), ((), ())), preferred_element_type=F32, precision=precision)


def _dot_tn(a, b, precision=None):
    return lax.dot_general(a, b, (((0,), (0,)), ((), ())), preferred_element_type=F32, precision=precision)


def _mod_kernel(c_ref, w_ref, b_ref, o_ref):
    o_ref[0] = _dot(_silu(c_ref[...]), w_ref[0], HI) + b_ref[0]


def _mod_call(cc, w_mod, b_mod):
    L, D, N = w_mod.shape
    R = cc.shape[0]
    tn = _tile(N, 1536)
    return pl.pallas_call(
        _mod_kernel,
        grid=(L, N // tn),
        in_specs=[pl.BlockSpec((R, D), lambda l, j: (0, 0)),
                  pl.BlockSpec((1, D, tn), lambda l, j: (l, 0, j)),
                  pl.BlockSpec((1, 1, tn), lambda l, j: (l, 0, j))],
        out_specs=pl.BlockSpec((1, R, tn), lambda l, j: (l, 0, j)),
        out_shape=jax.ShapeDtypeStruct((L, R, N), F32),
        compiler_params=_cp(("arbitrary", "arbitrary"), _VMEM_LIMIT),
        name="adaln_mod",
    )(cc, w_mod, b_mod.reshape(L, 1, N))


def _modnorm(x, nw, sc, sh):
    y = x * lax.rsqrt(jnp.mean(x * x, axis=-1, keepdims=True) + _NORM_EPS)
    return (y * nw) * (1.0 + sc) + sh


def _nm_kernel(h_ref, nw_ref, sc_ref, sh_ref, w_ref, o_ref, u_ref):
    @pl.when(pl.program_id(2) == 0)
    def _():
        u_ref[...] = _modnorm(h_ref[0], nw_ref[...], sc_ref[0], sh_ref[0]).astype(BF16)

    o_ref[0] = _dot(u_ref[...], w_ref[...])


def _norm_matmul(h, nw, sc, sh, w):
    B, T, D = h.shape
    N = w.shape[1]
    tm = _tile(T, 1024, 8)
    tn = _tile(N, 1536)
    return pl.pallas_call(
        _nm_kernel,
        grid=(B, T // tm, N // tn),
        in_specs=[pl.BlockSpec((1, tm, D), lambda b, i, j: (b, i, 0)),
                  pl.BlockSpec((1, D), lambda b, i, j: (0, 0)),
                  pl.BlockSpec((1, 1, D), lambda b, i, j: (b, 0, 0)),
                  pl.BlockSpec((1, 1, D), lambda b, i, j: (b, 0, 0)),
                  pl.BlockSpec((D, tn), lambda b, i, j: (0, j))],
        out_specs=pl.BlockSpec((1, tm, tn), lambda b, i, j: (b, i, j)),
        out_shape=jax.ShapeDtypeStruct((B, T, N), F32),
        scratch_shapes=[pltpu.VMEM((tm, D), BF16)],
        compiler_params=_cp(("arbitrary", "arbitrary", "arbitrary"), _VMEM_LIMIT),
        name="norm_in_proj",
    )(h, nw, sc, sh, w)


def _conv_kernel(x_ref, w_ref, o_ref, xp_ref, *, rows, W, T, PAD, CH):
    cw = x_ref.shape[2]
    xp_ref[0:PAD, :] = jnp.zeros((PAD, cw), F32)
    xp_ref[PAD + T:PAD + T + PAD, :] = jnp.zeros((PAD, cw), F32)
    xp_ref[PAD:PAD + T, :] = x_ref[0]
    for c0 in range(0, T, CH):
        col = jnp.bitwise_and(lax.broadcasted_iota(jnp.int32, (CH, cw), 0) + c0, W - 1)
        acc = jnp.zeros((CH, cw), F32)
        for a in range(3):
            if rows == 1 and a != 1:
                continue
            for b in range(3):
                off = (a - 1) * W + (b - 1)
                xs = xp_ref[PAD + c0 + off:PAD + c0 + off + CH, :]
                if b == 0:
                    xs = jnp.where(col >= 1, xs, 0.0)
                elif b == 2:
                    xs = jnp.where(col <= W - 2, xs, 0.0)
                acc = acc + xs * w_ref[a * 3 + b:a * 3 + b + 1, :]
        o_ref[0, c0:c0 + CH, :] = acc


def _conv_call(z, w9, rows, W):
    B, T, C = z.shape
    assert W & (W - 1) == 0 and rows * W == T
    cw = _LANE
    PAD = W + 8 if rows > 1 else 8
    CH = min(T, 128)
    kern = functools.partial(_conv_kernel, rows=rows, W=W, T=T, PAD=PAD, CH=CH)
    return pl.pallas_call(
        kern,
        grid=(B, C // cw),
        in_specs=[pl.BlockSpec((1, T, cw), lambda b, j: (b, 0, j)),
                  pl.BlockSpec((9, cw), lambda b, j: (0, j))],
        out_specs=pl.BlockSpec((1, T, cw), lambda b, j: (b, 0, j)),
        out_shape=jax.ShapeDtypeStruct((B, T, C), F32),
        scratch_shapes=[pltpu.VMEM((T + 2 * PAD, cw), F32)],
        compiler_params=_cp(("arbitrary", "arbitrary"), _VMEM_LIMIT),
        name="short_conv",
    )(z, w9)


def _seq_view(arr, col_major, rows):
    B, T, C = arr.shape
    return arr.reshape(B, rows, (T // rows) * C) if col_major else arr


def _seq_spec(C, ctot, part, col_major, nblk, reverse, bb):
    nper = ctot // C

    def blk(i):
        return nblk - 1 - i if reverse else i

    if col_major:
        return pl.BlockSpec((bb, _CHUNK, C), lambda b, i: (b, 0, blk(i) * nper + part))
    return pl.BlockSpec((bb, _CHUNK, C), lambda b, i: (b, blk(i), part))


def _scan_bb(B):
    return 4 if B % 4 == 0 else (2 if B % 2 == 0 else 1)


def _chunk_tri(n, tb, reverse):
    row = lax.broadcasted_iota(jnp.int32, (n, n), 0)
    col = lax.broadcasted_iota(jnp.int32, (n, n), 1)
    sh = tb.bit_length() - 1
    same = lax.shift_right_logical(row, sh) == lax.shift_right_logical(col, sh)
    order = (row <= col) if reverse else (row >= col)
    return jnp.where(jnp.logical_and(same, order), 1.0, 0.0)


def _full_spec(shape):
    nd = len(shape)
    return pl.BlockSpec(shape, lambda b, i: (0,) * nd)


def _split(x):
    hi = x.astype(BF16)
    return hi, (x - hi.astype(F32)).astype(BF16)


def _dot3(a, b, nt=False):
    f = _dot_nt if nt else _dot
    ah, al = _split(a)
    bh, bl = _split(b)
    return f(ah, bh) + (f(ah, bl) + f(al, bh))


def _dot_sel(a, sel):
    sel = sel.astype(BF16)
    a1 = a.astype(BF16)
    r1 = a - a1.astype(F32)
    a2 = r1.astype(BF16)
    a3 = (r1 - a2.astype(F32)).astype(BF16)
    return _dot(a1, sel) + (_dot(a2, sel) + _dot(a3, sel))


def _sel_dot(sel, b):
    sel = sel.astype(BF16)
    b1 = b.astype(BF16)
    r1 = b - b1.astype(F32)
    b2 = r1.astype(BF16)
    b3 = (r1 - b2.astype(F32)).astype(BF16)
    return _dot(sel, b1) + (_dot(sel, b2) + _dot(sel, b3))


def _rwkv_kernel(r_ref, k_ref, v_ref, lo_ref, s0_ref, w0_ref, wup_ref, a0_ref, aup_ref, kkw_ref, kaw_ref,
                 rkw_ref, j_ref, o_ref, bon_ref, sout_ref, ZT, *, reverse):
    i = pl.program_id(1)
    TB = _CHUNK
    NP = _RW_HEADS // 2
    BB = r_ref.shape[0]
    N = BB * TB

    @pl.when(i == 0)
    def _():
        ZT[...] = s0_ref[...]

    r = r_ref[...].reshape(N, _RW_WIDTH)
    k = k_ref[...].reshape(N, _RW_WIDTH)
    v = v_ref[...].reshape(N, _RW_WIDTH)
    lo = lo_ref[...].reshape(N, _RW_WIDTH)
    jm = j_ref[...]
    w_log = -_softplus(-(w0_ref[...] + _dot3(jnp.tanh(lo[:, 0:_LANE]), wup_ref[...]))) - 0.5
    a = _sigmoid(a0_ref[...] + _dot3(lo[:, _LANE:2 * _LANE], aup_ref[...]))
    kk0 = k * kkw_ref[...]
    kk = kk0 / jnp.maximum(jnp.sqrt(_dot_sel(kk0 * kk0, jm)), 1e-12)
    km = k * (1.0 + (a - 1.0) * kaw_ref[...])
    bon_ref[...] = (_dot_sel(r * km * rkw_ref[...], jm) * v).reshape(BB, TB, _RW_WIDTH)

    lw = -jnp.exp(w_log)
    g = _sel_dot(_chunk_tri(N, TB, reverse), lw)
    e0 = 0 if reverse else TB - 1
    g_end = jnp.concatenate([jnp.broadcast_to(g[bb * TB + e0:bb * TB + e0 + 1], (TB, _RW_WIDTH))
                             for bb in range(BB)], axis=0)
    pm = kk * jnp.exp(g - lw)
    qm = (kk * a) * jnp.exp(-g)
    khm = km * jnp.exp(-g)
    rhm = r * jnp.exp(g)
    qgm = (kk * a) * jnp.exp(g_end - g)
    kgm = km * jnp.exp(g_end - g)
    gam = jnp.exp(g_end)

    t_i = lax.broadcasted_iota(jnp.int32, (TB, _LANE), 0)
    lane = lax.broadcasted_iota(jnp.int32, (TB, _LANE), 1)
    s_i = jnp.bitwise_and(lane, _RW_HD - 1)
    m_a = lane < _RW_HD
    strict = (s_i > t_i) if reverse else (s_i < t_i)
    incl = (s_i >= t_i) if reverse else (s_i <= t_i)
    eye = jnp.where(s_i == t_i, 1.0, 0.0)
    same = ((lax.broadcasted_iota(jnp.int32, (_LANE, _LANE), 0) < _RW_HD)
            == (lax.broadcasted_iota(jnp.int32, (_LANE, _LANE), 1) < _RW_HD))

    def bd(x):
        return jnp.concatenate([jnp.where(m_a, x, 0.0), jnp.where(m_a, 0.0, x)], axis=0)

    units = [(bb, p) for bb in range(BB) for p in range(NP)]
    nu = range(len(units))

    def cut(x, un):
        bb, p = un
        return x[bb * TB:(bb + 1) * TB, p * _LANE:(p + 1) * _LANE]

    lo_h, hi_h = slice(0, _LANE), slice(_LANE, 2 * _LANE)
    gram = [_dot3(jnp.concatenate([cut(pm, un), cut(rhm, un)], axis=0),
                  jnp.concatenate([bd(cut(qm, un)), bd(cut(khm, un))], axis=0), nt=True) for un in units]
    l_pq = [jnp.where(strict, x[0:TB, lo_h], 0.0) for x in gram]
    l_pk = [jnp.where(strict, x[0:TB, hi_h], 0.0) for x in gram]
    m_rq = [jnp.where(incl, x[TB:2 * TB, lo_h], 0.0) for x in gram]
    m_rk = [jnp.where(incl, x[TB:2 * TB, hi_h], 0.0) for x in gram]
    wm = [_dot(jnp.concatenate([l_pk[n], m_rk[n]], axis=0).astype(BF16), bd(cut(v, units[n])).astype(BF16))
          for n in nu]
    kv = [_dot_tn(cut(v, un).astype(BF16), cut(kgm, un).astype(BF16)) for un in units]
    s_m = [eye - x for x in l_pq]
    m_m = [_dot3(x, bd(x)) for x in l_pq]
    nlev = TB.bit_length() - 1
    for lev in range(1, nlev):
        if lev < nlev - 1:
            xs = [_dot3(m_m[n], jnp.concatenate([bd(s_m[n]), bd(m_m[n])], axis=1)) for n in nu]
            s_m = [s_m[n] + xs[n][:, lo_h] for n in nu]
            m_m = [x[:, hi_h] for x in xs]
        else:
            s_m = [s_m[n] + _dot3(m_m[n], bd(s_m[n])) for n in nu]
    ta = [_dot3(s_m[n], jnp.concatenate([bd(cut(pm, units[n])), bd(wm[n][0:TB])], axis=1)) for n in nu]
    zt = [ZT[bb, p] for bb, p in units]
    az = [_dot_nt(jnp.concatenate([ta[n][:, lo_h], cut(rhm, units[n])], axis=0).astype(BF16), zt[n].astype(BF16))
          for n in nu]
    u = [az[n][0:TB] + ta[n][:, hi_h] for n in nu]
    mu = [_dot(m_rq[n].astype(BF16), bd(u[n]).astype(BF16)) for n in nu]
    qu = [_dot_tn(u[n].astype(BF16), cut(qgm, units[n]).astype(BF16)) for n in nu]
    for n in nu:
        bb, p = units[n]
        o_ref[bb, :, p * _LANE:(p + 1) * _LANE] = az[n][TB:2 * TB] - mu[n] + wm[n][TB:2 * TB]
        ZT[bb, p] = zt[n] * cut(gam, units[n])[0:1] + jnp.where(same, kv[n] - qu[n], 0.0)

    @pl.when(i == pl.num_programs(1) - 1)
    def _():
        sout_ref[...] = ZT[...]


def _rwkv_call(rkv, lora, s0, wts, col_major, reverse, rows):
    B, T, _ = rkv.shape
    C = _RW_WIDTH
    nblk = T // _CHUNK
    if col_major:
        assert rows == _CHUNK
    rv = _seq_view(rkv, col_major, rows)
    lv = _seq_view(lora, col_major, rows)
    NP = _RW_HEADS // 2
    bb = _scan_bb(B)
    seq = functools.partial(_seq_spec, col_major=col_major, nblk=nblk, reverse=reverse, bb=bb)
    out_shape = rv.shape[:2] + (rv.shape[2] // 3,)
    st_spec = pl.BlockSpec((bb, NP, _LANE, _LANE), lambda b, i: (b, 0, 0, 0))
    o, bon, s_out = pl.pallas_call(
        functools.partial(_rwkv_kernel, reverse=reverse),
        grid=(B // bb, nblk),
        in_specs=[seq(C, 3 * C, 0), seq(C, 3 * C, 1), seq(C, 3 * C, 2), seq(C, C, 0), st_spec,
                  _full_spec((1, C)), _full_spec((2 * _RW_LORA, C)), _full_spec((1, C)),
                  _full_spec((2 * _RW_LORA, C)), _full_spec((1, C)), _full_spec((1, C)), _full_spec((1, C)),
                  _full_spec((C, C))],
        out_specs=[seq(C, C, 0), seq(C, C, 0), st_spec],
        out_shape=[jax.ShapeDtypeStruct(out_shape, F32), jax.ShapeDtypeStruct(out_shape, F32),
                   jax.ShapeDtypeStruct((B, NP, _LANE, _LANE), F32)],
        scratch_shapes=[pltpu.VMEM((bb, NP, _LANE, _LANE), F32)],
        compiler_params=_cp(("arbitrary", "arbitrary"), _VMEM_LIMIT),
        name="rwkv7_scan",
    )(rv, rv, rv, lv, s0, *wts)
    return o.reshape(B, T, C), bon.reshape(B, T, C), s_out


def _gla_kernel(q_ref, k_ref, v_ref, lo_ref, s0_ref, aup_ref, ab_ref, o_ref, sout_ref, ST, *, reverse):
    i = pl.program_id(1)
    TB = _CHUNK

    @pl.when(i == 0)
    def _():
        ST[...] = s0_ref[...]

    BB = q_ref.shape[0]
    N = BB * TB
    q = _silu(q_ref[...].reshape(N, _GLA_PW)) * (_GLA_DK ** -0.5)
    k = _silu(k_ref[...].reshape(N, _GLA_PW))
    v = _silu(v_ref[...].reshape(N, _GLA_PW))
    gad = lo_ref[...].reshape(N, _RW_WIDTH)[:, 2 * _LANE:3 * _LANE]
    log_a = -_softplus(-(_dot3(gad, aup_ref[...]) + ab_ref[...])) / _GLA_TAU
    row = lax.broadcasted_iota(jnp.int32, (TB, TB), 0)
    col = lax.broadcasted_iota(jnp.int32, (TB, TB), 1)
    keep = (row <= col) if reverse else (row >= col)
    g_cum = _sel_dot(_chunk_tri(N, TB, reverse), log_a)
    e0 = 0 if reverse else TB - 1
    g_last = jnp.concatenate([jnp.broadcast_to(g_cum[bb * TB + e0:bb * TB + e0 + 1], (TB, _GLA_PW))
                              for bb in range(BB)], axis=0)
    q_in = q * jnp.exp(g_cum)
    k_in = k * jnp.exp(-g_cum)
    k_tail = k * jnp.exp(g_last - g_cum)
    dec = jnp.exp(g_last)
    units = [(bb, h) for bb in range(BB) for h in range(_GLA_HEADS)]
    nu = range(len(units))

    def cut(x, un):
        bb, h = un
        return x[bb * TB:(bb + 1) * TB, h * _LANE:(h + 1) * _LANE]

    qh = [cut(q_in, un).astype(BF16) for un in units]
    vh = [cut(v, un).astype(BF16) for un in units]
    st = [ST[bb, h] for bb, h in units]
    att = [_dot_nt(qh[n], cut(k_in, units[n]).astype(BF16)) for n in nu]
    o_inter = [_dot_nt(qh[n], st[n].astype(BF16)) for n in nu]
    kv = [_dot_tn(vh[n], cut(k_tail, units[n]).astype(BF16)) for n in nu]
    o_intra = [_dot(jnp.where(keep, att[n], 0.0).astype(BF16), vh[n]) for n in nu]
    for n in nu:
        bb, h = units[n]
        o_ref[bb, :, h * _LANE:(h + 1) * _LANE] = o_intra[n] + o_inter[n]
        ST[bb, h] = st[n] * cut(dec, units[n])[0:1] + kv[n]

    @pl.when(i == pl.num_programs(1) - 1)
    def _():
        sout_ref[...] = ST[...]


def _gla_call(qkv, lora, s0, aup, ab, col_major, reverse, rows):
    B, T, _ = qkv.shape
    C = _GLA_PW
    nblk = T // _CHUNK
    if col_major:
        assert rows == _CHUNK
    qv = _seq_view(qkv, col_major, rows)
    lv = _seq_view(lora, col_major, rows)
    bb = _scan_bb(B)
    seq = functools.partial(_seq_spec, col_major=col_major, nblk=nblk, reverse=reverse, bb=bb)
    out_shape = qv.shape[:2] + (qv.shape[2] // 3,)
    st_spec = pl.BlockSpec((bb, _GLA_HEADS, _LANE, _LANE), lambda b, i: (b, 0, 0, 0))
    o, s_out = pl.pallas_call(
        functools.partial(_gla_kernel, reverse=reverse),
        grid=(B // bb, nblk),
        in_specs=[seq(C, 3 * C, 0), seq(C, 3 * C, 1), seq(C, 3 * C, 2), seq(_RW_WIDTH, _RW_WIDTH, 0), st_spec,
                  _full_spec((_LANE, C)), _full_spec((1, C))],
        out_specs=[seq(C, C, 0), st_spec],
        out_shape=[jax.ShapeDtypeStruct(out_shape, F32),
                   jax.ShapeDtypeStruct((B, _GLA_HEADS, _LANE, _LANE), F32)],
        scratch_shapes=[pltpu.VMEM((bb, _GLA_HEADS, _LANE, _LANE), F32)],
        compiler_params=_cp(("arbitrary", "arbitrary"), _VMEM_LIMIT),
        name="gla_chunked",
    )(qv, qv, qv, lv, s0, aup, ab)
    return o.reshape(B, T, C), s_out


def _fourier_chan_kernel(x_ref, cg_ref, sg_ref, o_ref):
    T = x_ref.shape[1]
    x = x_ref[0]
    o_ref[0, 0:T, :] = _dot3(x, cg_ref[...]).astype(BF16)
    o_ref[0, T:2 * T, :] = _dot3(x, sg_ref[...]).astype(BF16)


def _fourier_kernel(w_ref, xcs_ref, o_ref):
    o_ref[0] = _dot(w_ref[...], xcs_ref[0])


def _fourier_call(zpost, cg, sg, wt):
    B, T, _ = zpost.shape
    C = _FN_WIDTH
    xcs = pl.pallas_call(
        _fourier_chan_kernel,
        grid=(B,),
        in_specs=[pl.BlockSpec((1, T, C), lambda b: (b, 0, 0)),
                  pl.BlockSpec((C, C), lambda b: (0, 0)), pl.BlockSpec((C, C), lambda b: (0, 0))],
        out_specs=pl.BlockSpec((1, 2 * T, C), lambda b: (b, 0, 0)),
        out_shape=jax.ShapeDtypeStruct((B, 2 * T, C), BF16),
        compiler_params=_cp(("arbitrary",), _VMEM_LIMIT),
        name="fnet_chan_dft",
    )(zpost, cg, sg)
    tm = _tile(T, 512, 8)
    return pl.pallas_call(
        _fourier_kernel,
        grid=(T // tm, B),
        in_specs=[pl.BlockSpec((tm, 2 * T), lambda i, b: (i, 0)),
                  pl.BlockSpec((1, 2 * T, C), lambda i, b: (b, 0, 0))],
        out_specs=pl.BlockSpec((1, tm, C), lambda i, b: (b, i, 0)),
        out_shape=jax.ShapeDtypeStruct((B, T, C), F32),
        compiler_params=_cp(("arbitrary", "arbitrary"), _VMEM_LIMIT),
        name="fnet_dft",
    )(wt, xcs)


def _merge_kernel(rwf_ref, rwb_ref, bnf_ref, bnb_ref, glf_ref, glb_ref, zp_ref, fn_ref, h_ref, g1_ref, sc_ref, sh_ref,
                  nw_ref, wg_ref, jr_ref, lnw_ref, lnb_ref, gup_ref, jg_ref, gnw_ref, pa_ref, pb_ref, pc_ref, wo_ref,
                  o_ref):
    D = h_ref.shape[2]
    zp = zp_ref[0]
    og = zp[:, _FN_WIDTH:_FN_WIDTH + _GLA_PW]
    gd = zp[:, _FN_WIDTH + _GLA_PW:_FN_WIDTH + _GLA_PW + _RW_GATE_RANK]
    o = rwf_ref[0] + rwb_ref[0]
    jr = jr_ref[...]
    mu = _dot_sel(o, jr) * (1.0 / _RW_HD)
    xc = o - mu
    var = _dot_sel(xc * xc, jr) * (1.0 / _RW_HD)
    y = xc * lax.rsqrt(var + _RW_GN_EPS)
    y = y * lnw_ref[...] + lnb_ref[...] + (bnf_ref[0] + bnb_ref[0])
    rw_y = y * _dot3(_sigmoid(gd), gup_ref[...])
    g = glf_ref[0] + glb_ref[0]
    ms = _dot_sel(g * g, jg_ref[...]) * (1.0 / _GLA_DV)
    gla_y = g * lax.rsqrt(ms + _NORM_EPS) * gnw_ref[...] * _silu(og)
    h = h_ref[0]
    u = _modnorm(h, nw_ref[...], sc_ref[0], sh_ref[0]).astype(BF16)
    gates = _sigmoid(_dot(u, wg_ref[...]))
    m = (gates[:, 0:D] * _dot(rw_y.astype(BF16), pa_ref[...])
         + gates[:, D:2 * D] * _dot(gla_y.astype(BF16), pb_ref[...])
         + gates[:, 2 * D:3 * D] * _dot(fn_ref[0].astype(BF16), pc_ref[...]))
    o_ref[0] = h + g1_ref[0] * _dot(m.astype(BF16), wo_ref[...])


def _merge_call(rwf, rwb, bnf, bnb, glf, glb, zpost, fn, h, g1, sc, sh, wts):
    B, T, D = h.shape
    tm = _tile(T, 512, 8)

    def row(c):
        return pl.BlockSpec((1, tm, c), lambda b, i: (b, i, 0))

    acts = [rwf, rwb, bnf, bnb, glf, glb, zpost, fn, h]
    mods = [g1, sc, sh]
    return pl.pallas_call(
        _merge_kernel,
        grid=(B, T // tm),
        in_specs=[row(a.shape[2]) for a in acts] + [pl.BlockSpec((1, 1, D), lambda b, i: (b, 0, 0))] * len(mods)
        + [_full_spec(w.shape) for w in wts],
        out_specs=row(D),
        out_shape=jax.ShapeDtypeStruct((B, T, D), F32),
        compiler_params=_cp(("arbitrary", "arbitrary"), _VMEM_LIMIT),
        name="branch_merge",
    )(*acts, *mods, *wts)


def _router_kernel(h_ref, nw_ref, sc_ref, sh_ref, rt_ref, v_ref, aff_ref):
    u = _modnorm(h_ref[0], nw_ref[...], sc_ref[0], sh_ref[0])
    v_ref[0] = u.astype(BF16)
    logits = _dot_nt(rt_ref[...], u, HI)
    e = jnp.exp(logits - jnp.max(logits, axis=0, keepdims=True))
    aff_ref[0] = e / jnp.sum(e, axis=0, keepdims=True)


def _router_call(h, nw, sc, sh, router_t):
    B, T, D = h.shape
    E = router_t.shape[0]
    tm = _tile(T, 512, 128)
    return pl.pallas_call(
        _router_kernel,
        grid=(B, T // tm),
        in_specs=[pl.BlockSpec((1, tm, D), lambda b, i: (b, i, 0)), _full_spec((1, D)),
                  pl.BlockSpec((1, 1, D), lambda b, i: (b, 0, 0)), pl.BlockSpec((1, 1, D), lambda b, i: (b, 0, 0)),
                  _full_spec((E, D))],
        out_specs=[pl.BlockSpec((1, tm, D), lambda b, i: (b, i, 0)), pl.BlockSpec((1, E, tm), lambda b, i: (b, 0, i))],
        out_shape=[jax.ShapeDtypeStruct((B, T, D), BF16), jax.ShapeDtypeStruct((B, E, T), F32)],
        compiler_params=_cp(("arbitrary", "arbitrary"), _VMEM_LIMIT),
        name="router_softmax",
    )(h, nw, sc, sh, router_t)


def _topc_kernel(aff_ref, pos_ref, cum_ref, *, cap):
    E, T = aff_ref.shape[1], aff_ref.shape[2]
    x = pltpu.bitcast(aff_ref[0], jnp.int32)

    def body(it, thr):
        cand = thr | lax.shift_left(jnp.int32(1), 30 - it)
        cnt = jnp.sum(jnp.where(x >= cand, 1, 0), axis=1, keepdims=True)
        return jnp.where(cnt >= cap, cand, thr)

    thr = lax.fori_loop(0, 31, body, jnp.zeros((E, 1), jnp.int32))
    need = (cap - jnp.sum(jnp.where(x > thr, 1, 0), axis=1, keepdims=True)).astype(F32)
    blk = _LANE
    upper = jnp.where(lax.broadcasted_iota(jnp.int32, (blk, blk), 0) < lax.broadcasted_iota(jnp.int32, (blk, blk), 1),
                      1.0, 0.0).astype(BF16)
    off_eq = jnp.zeros((E, 1), F32)
    off_sel = jnp.zeros((E, 1), F32)
    for c in range(T // blk):
        sl = slice(c * blk, (c + 1) * blk)
        xc = pltpu.bitcast(aff_ref[0, :, sl], jnp.int32)
        eq_c = jnp.where(xc == thr, 1.0, 0.0)
        rank_eq = _dot(eq_c.astype(BF16), upper) + off_eq
        take = jnp.where(rank_eq < need, eq_c, 0.0)
        sel = jnp.where(xc > thr, 1.0, take)
        rank = _dot(sel.astype(BF16), upper) + off_sel
        pos_ref[0, :, sl] = jnp.where(sel > 0.0, rank.astype(jnp.int32), -1)
        off_eq = off_eq + jnp.sum(eq_c, axis=1, keepdims=True)
        off_sel = off_sel + jnp.sum(sel, axis=1, keepdims=True)
    chosen = jnp.where(pos_ref[0] >= 0, 1.0, 0.0).astype(BF16)
    before = jnp.where(lax.shift_right_logical(lax.broadcasted_iota(jnp.int32, (T, blk), 0), blk.bit_length() - 1)
                       < lax.broadcasted_iota(jnp.int32, (T, blk), 1), 1.0, 0.0).astype(BF16)
    cum_ref[0] = _dot(chosen, before).astype(jnp.int32)


def _topc_call(aff, cap):
    B, E, T = aff.shape
    assert T // _LANE < _LANE
    pos, cum = pl.pallas_call(
        functools.partial(_topc_kernel, cap=cap),
        grid=(B,),
        in_specs=[pl.BlockSpec((1, E, T), lambda b: (b, 0, 0))],
        out_specs=[pl.BlockSpec((1, E, T), lambda b: (b, 0, 0)), pl.BlockSpec((1, E, _LANE), lambda b: (b, 0, 0))],
        out_shape=[jax.ShapeDtypeStruct((B, E, T), jnp.int32), jax.ShapeDtypeStruct((B, E, _LANE), jnp.int32)],
        compiler_params=_cp(("arbitrary",), _VMEM_LIMIT),
        name="expert_choice_topc",
    )(aff)
    return pos, cum[:, :, :T // _LANE + 1].reshape(-1)


def _slot_block(cap):
    return min(cap, _LANE)


def _ffn_kernel(cum_ref, v_ref, pos_ref, aff_ref, w1_ref, w3_ref, w2_ref, ys_ref, xs_ref, acc_ref, gate_ref, *, cap, tc,
                expert_axis):
    e = pl.program_id(expert_axis)
    b = pl.program_id(1 - expert_axis)
    f = pl.program_id(2)
    T = v_ref.shape[1]
    E = pos_ref.shape[1]
    sbz = _slot_block(cap)

    @pl.when(f == 0)
    def _():
        acc_ref[...] = jnp.zeros(acc_ref.shape, F32)
        gate_ref[...] = jnp.zeros(gate_ref.shape, F32)
        base = (b * E + e) * (T // _LANE + 1)
        for c in range(T // tc):
            sl = slice(c * tc, (c + 1) * tc)
            lo = cum_ref[base + c * (tc // _LANE)]
            hi = cum_ref[base + (c + 1) * (tc // _LANE)]
            for sb in range(cap // sbz):
                rows = slice(sb * sbz, (sb + 1) * sbz)

                @pl.when(jnp.logical_and(lo < (sb + 1) * sbz, hi > sb * sbz))
                def _():
                    slot = lax.broadcasted_iota(jnp.int32, (sbz, tc), 0) + sb * sbz
                    hit = slot == pos_ref[0, e, :, sl]
                    acc_ref[rows, :] += _dot(jnp.where(hit, 1.0, 0.0).astype(BF16), v_ref[0, sl, :])
                    gate_ref[rows, :] += jnp.sum(jnp.where(hit, aff_ref[0, e, :, sl], 0.0), axis=1, keepdims=True)
        xs_ref[...] = acc_ref[...].astype(BF16)
        acc_ref[...] = jnp.zeros(acc_ref.shape, F32)

    x = xs_ref[...]
    h1 = _dot(x, w1_ref[0])
    hid = _silu(h1) * _dot(x, w3_ref[0])
    acc_ref[...] += _dot(hid.astype(BF16), w2_ref[0])

    @pl.when(f == pl.num_programs(2) - 1)
    def _():
        ys_ref[0, 0] = (acc_ref[...] * gate_ref[...]).astype(BF16)


def _ffn_call(cum, v, pos, aff, w1, w3, w2, cap):
    B, T, D = v.shape
    E, _, F = w1.shape
    tc = _tile(T, 512)
    expert_major = T <= 512
    fc = F if expert_major else _tile(F, 1024)
    if expert_major:
        grid = (E, B, 1)

        def ix(f):
            return lambda e, b, j: f(b, e, j)
    else:
        grid = (B, E, F // fc)

        def ix(f):
            return f
    return pl.pallas_call(
        functools.partial(_ffn_kernel, cap=cap, tc=tc, expert_axis=0 if expert_major else 1),
        grid=grid,
        in_specs=[pl.BlockSpec(memory_space=pltpu.SMEM),
                  pl.BlockSpec((1, T, D), ix(lambda b, e, f: (b, 0, 0))),
                  pl.BlockSpec((1, E, 1, T), ix(lambda b, e, f: (b, 0, 0, 0))),
                  pl.BlockSpec((1, E, 1, T), ix(lambda b, e, f: (b, 0, 0, 0))),
                  pl.BlockSpec((1, D, fc), ix(lambda b, e, f: (e, 0, f))),
                  pl.BlockSpec((1, D, fc), ix(lambda b, e, f: (e, 0, f))),
                  pl.BlockSpec((1, fc, D), ix(lambda b, e, f: (e, f, 0)))],
        out_specs=pl.BlockSpec((1, 1, cap, D), ix(lambda b, e, f: (b, e, 0, 0))),
        out_shape=jax.ShapeDtypeStruct((B, E, cap, D), BF16),
        scratch_shapes=[pltpu.VMEM((cap, D), BF16), pltpu.VMEM((cap, D), F32), pltpu.VMEM((cap, 1), F32)],
        compiler_params=_cp(("arbitrary", "arbitrary", "arbitrary"), _VMEM_LIMIT),
        name="expert_ffn",
    )(cum, v, pos, aff, w1, w3, w2)


def _scatter_kernel(pos_ref, ys_ref, h_ref, g2_ref, o_ref, *, cap):
    tm = h_ref.shape[1]
    E = ys_ref.shape[1]
    pos_t = pos_ref[0]
    slot = lax.broadcasted_iota(jnp.int32, (tm, cap), 1)
    acc = None
    for e in range(E):
        hit = jnp.where(slot == pos_t[:, e:e + 1], 1.0, 0.0).astype(BF16)
        part = _dot(hit, ys_ref[0, e])
        acc = part if acc is None else acc + part
    o_ref[0] = h_ref[0] + g2_ref[0] * acc


def _scatter_call(pos_t, ys, h, g2, cap):
    B, T, D = h.shape
    E = ys.shape[1]
    tm = _tile(T, 512)
    return pl.pallas_call(
        functools.partial(_scatter_kernel, cap=cap),
        grid=(B, T // tm),
        in_specs=[pl.BlockSpec((1, tm, E), lambda b, i: (b, i, 0)),
                  pl.BlockSpec((1, E, cap, D), lambda b, i: (b, 0, 0, 0)),
                  pl.BlockSpec((1, tm, D), lambda b, i: (b, i, 0)),
                  pl.BlockSpec((1, 1, D), lambda b, i: (b, 0, 0))],
        out_specs=pl.BlockSpec((1, tm, D), lambda b, i: (b, i, 0)),
        out_shape=jax.ShapeDtypeStruct((B, T, D), F32),
        compiler_params=_cp(("arbitrary", "arbitrary"), _VMEM_LIMIT),
        name="expert_scatter",
    )(pos_t, ys, h, g2)


def _final_kernel(h_ref, w_ref, o_ref):
    x = h_ref[0]
    o_ref[0] = x * lax.rsqrt(jnp.mean(x * x, axis=-1, keepdims=True) + _NORM_EPS) * w_ref[...]


def _final_call(h, w):
    B, T, D = h.shape
    tm = _tile(T, 1024, 8)
    return pl.pallas_call(
        _final_kernel,
        grid=(B, T // tm),
        in_specs=[pl.BlockSpec((1, tm, D), lambda b, i: (b, i, 0)), _full_spec((1, D))],
        out_specs=pl.BlockSpec((1, tm, D), lambda b, i: (b, i, 0)),
        out_shape=jax.ShapeDtypeStruct((B, T, D), F32),
        compiler_params=_cp(("arbitrary", "arbitrary"), _VMEM_LIMIT),
        name="final_norm",
    )(h, w)


def _pad_heads(x, nh, d, axis=-1):
    axis = axis % x.ndim
    shp = x.shape
    x = x.reshape(shp[:axis] + (nh, d) + shp[axis + 1:])
    pad = [(0, 0)] * x.ndim
    pad[axis + 1] = (0, _LANE - d)
    x = jnp.pad(x, pad)
    return x.reshape(shp[:axis] + (nh * _LANE,) + shp[axis + 1:])


def _pad_to(x, n, axis=-1):
    axis = axis % x.ndim
    pad = [(0, 0)] * x.ndim
    pad[axis] = (0, n - x.shape[axis])
    return jnp.pad(x, pad)


def _block_ones(n, blk):
    i = np.arange(n) // blk
    return jnp.asarray((i[:, None] == i[None, :]).astype(np.float32))


def _dft_consts(T):
    s = 1 << ((T.bit_length() - 1) // 2)
    i = np.arange(T)[:, None]
    a1 = 2.0 * np.pi * ((i * s * np.arange(T // s)[None, :]) % T) / T
    a2 = 2.0 * np.pi * ((i * np.arange(s)[None, :]) % T) / T
    c1, s1, c2, s2 = (jnp.asarray(f(a) / math.sqrt(math.sqrt(T)), F32)
                      for a, f in ((a1, np.cos), (a1, np.sin), (a2, np.cos), (a2, np.sin)))
    cos = (c1[:, :, None] * c2[:, None, :] - s1[:, :, None] * s2[:, None, :]).reshape(T, T)
    sin = (s1[:, :, None] * c2[:, None, :] + c1[:, :, None] * s2[:, None, :]).reshape(T, T)
    wt = jnp.concatenate([cos, -sin], axis=1)
    g = np.arange(_FN_GD)
    ang_g = 2.0 * np.pi * ((g[:, None] * g[None, :]) % _FN_GD) / _FN_GD
    eye = np.eye(_FN_GROUPS)
    cg = np.kron(eye, np.cos(ang_g)) / math.sqrt(_FN_GD)
    sg = np.kron(eye, np.sin(ang_g)) / math.sqrt(_FN_GD)
    return jnp.asarray(wt, BF16), jnp.asarray(cg, F32), jnp.asarray(sg, F32)


def _layer_weights(i, w_in, conv_w, rw_w0, rw_w_up, rw_a0, rw_a_up, rw_k_k, rw_k_a, rw_r_k, rw_g_up, rw_ln_w,
                   rw_ln_b, gla_a_up, gla_a_b, gla_norm_w, proj_a, proj_b, proj_c, w_out):
    D = w_in.shape[1]
    rw, kw, vw = _RW_WIDTH, _GLA_HEADS * _GLA_DK, _GLA_HEADS * _GLA_DV
    o_gq = 3 * rw
    o_gk = o_gq + kw
    o_gv = o_gk + kw
    o_wd = o_gv + vw
    o_ad = o_wd + 2 * _RW_LORA
    o_ga = o_ad + 2 * _RW_LORA
    o_gd = o_ga + 2 * _GLA_RANK
    o_og = o_gd + _RW_GATE_RANK
    o_fn = o_og + vw
    o_gt = o_fn + _FN_WIDTH
    wi = w_in[i]
    cw = conv_w[i].reshape(9, -1)

    def gla_cols(x):
        return jnp.concatenate([_pad_heads(x[..., o_gq:o_gk], _GLA_HEADS, _GLA_DK),
                                _pad_heads(x[..., o_gk:o_gv], _GLA_HEADS, _GLA_DK),
                                _pad_heads(x[..., o_gv:o_wd], _GLA_HEADS, _GLA_DV)], axis=-1)

    w = {}
    w['in_rw'] = wi[:, 0:o_gq].astype(BF16)
    w['in_gla'] = gla_cols(wi).astype(BF16)
    w['in_lora'] = jnp.concatenate([wi[:, o_wd:o_ga], _pad_to(wi[:, o_ga:o_gd], _LANE)], axis=-1).astype(BF16)
    w['in_post'] = jnp.concatenate([wi[:, o_fn:o_gt], _pad_heads(wi[:, o_og:o_fn], _GLA_HEADS, _GLA_DV),
                                    wi[:, o_gd:o_og]], axis=-1).astype(BF16)
    w['in_gates'] = wi[:, o_gt:].astype(BF16)
    w['conv_rw'] = cw[:, 0:o_gq]
    w['conv_gla'] = gla_cols(cw)
    zl = jnp.zeros((_RW_LORA, rw), F32)
    w['rwkv'] = []
    w['gla'] = []
    jr = _block_ones(rw, _RW_HD)
    for d in range(2):
        wup = jnp.concatenate([rw_w_up[i, d], zl] if d == 0 else [zl, rw_w_up[i, d]], axis=0)
        aup = jnp.concatenate([rw_a_up[i, d], zl] if d == 0 else [zl, rw_a_up[i, d]], axis=0)
        w['rwkv'].append((rw_w0[i, d][None], wup, rw_a0[i, d][None], aup, rw_k_k[i][None], rw_k_a[i][None],
                          rw_r_k[i].reshape(1, rw), jr))
        ga = _pad_heads(gla_a_up[i, d], _GLA_HEADS, _GLA_DK)
        ga = jnp.pad(ga, ((d * _GLA_RANK, _LANE - (d + 1) * _GLA_RANK), (0, 0)))
        w['gla'].append((ga, _pad_heads(gla_a_b[i, d][None], _GLA_HEADS, _GLA_DK)))
    w['merge'] = (jr, rw_ln_w[i][None], rw_ln_b[i][None], rw_g_up[i], _block_ones(_GLA_PW, _LANE),
                  _pad_heads(jnp.tile(gla_norm_w[i], _GLA_HEADS)[None], _GLA_HEADS, _GLA_DV),
                  proj_a[i].astype(BF16), _pad_heads(proj_b[i], _GLA_HEADS, _GLA_DV, axis=0).astype(BF16),
                  proj_c[i].astype(BF16), w_out[i].astype(BF16))
    return w


def _moe(h, nw, sc, sh, g2, router_t, w1, w3, w2):
    B, T, D = h.shape
    E = router_t.shape[0]
    cap = 2 * T // E
    v, aff = _router_call(h, nw, sc, sh, router_t)
    pos, cum = _topc_call(aff, cap)
    ys = _ffn_call(cum, v, pos.reshape(B, E, 1, T), aff.reshape(B, E, 1, T), w1, w3, w2, cap)
    return _scatter_call(jnp.swapaxes(pos, 1, 2), ys, h, g2, cap)


def kernel(x, c, ctx, c_ctx, w_mod, b_mod, norm1_w, norm2_w, w_in, conv_w, rw_w0, rw_w_up, rw_a0, rw_a_up, rw_k_k, rw_k_a, rw_r_k, rw_g_up, rw_ln_w, rw_ln_b, gla_a_up, gla_a_b, gla_norm_w, proj_a, proj_b, proj_c, w_out, router, exp_w1, exp_w3, exp_w2, final_norm_w):
    B, S, D = x.shape
    TC = ctx.shape[1]
    L = w_mod.shape[0]
    W = _GRID_W
    rows = S // W
    assert S % _CHUNK == 0 and TC % _CHUNK == 0

    cc = _pad_to(jnp.concatenate([c, c_ctx[None]], axis=0), 16, axis=0)
    mods = _mod_call(cc, w_mod, b_mod)
    dft = {S: _dft_consts(S), TC: _dft_consts(TC)}
    s0_rw = jnp.zeros((B, _RW_HEADS // 2, _LANE, _LANE), F32)
    s0_gla = jnp.zeros((B, _GLA_HEADS, _LANE, _LANE), F32)

    h_lat, h_ctx = x, ctx
    for i in range(L):
        col_major = i % 2 == 1
        need_ctx = i < L - 1
        lw = _layer_weights(i, w_in, conv_w, rw_w0, rw_w_up, rw_a0, rw_a_up, rw_k_k, rw_k_a, rw_r_k, rw_g_up,
                            rw_ln_w, rw_ln_b, gla_a_up, gla_a_b, gla_norm_w, proj_a, proj_b, proj_c, w_out)
        m_lat = mods[i, :B].reshape(B, 1, _N_MOD, D)
        m_ctx = jnp.broadcast_to(mods[i, B:B + 1].reshape(1, 1, _N_MOD, D), (B, 1, _N_MOD, D))
        n1 = norm1_w[i][None]
        n2 = norm2_w[i][None]

        def in_proj(h, m, img_rows, img_w):
            z = {k: _norm_matmul(h, n1, m[:, :, 1], m[:, :, 0], lw['in_' + k])
                 for k in ('rw', 'gla', 'lora', 'post')}
            z['rw'] = _conv_call(z['rw'], lw['conv_rw'], img_rows, img_w)
            z['gla'] = _conv_call(z['gla'], lw['conv_gla'], img_rows, img_w)
            return z

        z_lat = in_proj(h_lat, m_lat, rows, W)
        z_ctx = in_proj(h_ctx, m_ctx, 1, TC)

        rw_lat, rw_ctx, gl_lat, gl_ctx = [], [], [], []
        for d in range(2):
            rev = d == 1
            o_c, b_c, st = _rwkv_call(z_ctx['rw'], z_ctx['lora'], s0_rw, lw['rwkv'][d], False, rev, 1)
            o_l, b_l, _ = _rwkv_call(z_lat['rw'], z_lat['lora'], st, lw['rwkv'][d], col_major, rev, rows)
            rw_ctx.append((o_c, b_c))
            rw_lat.append((o_l, b_l))
            g_c, st = _gla_call(z_ctx['gla'], z_ctx['lora'], s0_gla, *lw['gla'][d], False, rev, 1)
            g_l, _ = _gla_call(z_lat['gla'], z_lat['lora'], st, *lw['gla'][d], col_major, rev, rows)
            gl_ctx.append(g_c)
            gl_lat.append(g_l)

        def mix(h, m, z, rwo, glo, T):
            wt, cg, sg = dft[T]
            fn = _fourier_call(z['post'], cg, sg, wt)
            return _merge_call(rwo[0][0], rwo[1][0], rwo[0][1], rwo[1][1], glo[0], glo[1], z['post'], fn, h,
                               m[:, :, 2], m[:, :, 1], m[:, :, 0], (n1, lw['in_gates']) + lw['merge'])

        router_t = router[i].T
        w1 = exp_w1[i].astype(BF16)
        w3 = exp_w3[i].astype(BF16)
        w2 = exp_w2[i].astype(BF16)
        h_lat = mix(h_lat, m_lat, z_lat, rw_lat, gl_lat, S)
        h_lat = _moe(h_lat, n2, m_lat[:, :, 4], m_lat[:, :, 3], m_lat[:, :, 5], router_t, w1, w3, w2)
        if need_ctx:
            h_ctx = mix(h_ctx, m_ctx, z_ctx, rw_ctx, gl_ctx, TC)
            h_ctx = _moe(h_ctx, n2, m_ctx[:, :, 4], m_ctx[:, :, 3], m_ctx[:, :, 5], router_t, w1, w3, w2)
    return _final_call(h_lat, final_norm_w[None])
```
